```python
import math
import jax, jax.numpy as jnp
from jax import lax
import numpy as np

D_MODEL = 1024
BATCH = 4
SEQ = 4096
DEPTH = 2
DEC_BATCH = 128
DEC_SEQ = 1
PAST_LEN = 16384
PAGE_SIZE = 128

N_A_LAYERS = DEPTH // 2
N_B_LAYERS = DEPTH - N_A_LAYERS
GROUP_CH = 16
N_GROUPS = D_MODEL // GROUP_CH
STATE_P = 64
DT_MIN = 0.001
DT_MAX = 0.1
N_HEADS = 16
HEAD_NOPE = 64
HEAD_ROPE = 32
HEAD_V = 64
KV_RANK = 256
Q_RANK = 384
ROPE_BASE = 10000.0
Q_BLOCK = 128
D_FF = 2816
FFN_RES = 0.5
EPS = 1e-6
NEG_BIG = -1e30

kernel_name = "yoco_s5_mla_macaron_step"


def rmsnorm(x, g):
    xf = x.astype(jnp.float32)
    y = xf * lax.rsqrt(jnp.mean(xf * xf, axis=-1, keepdims=True) + EPS)
    return (y * g.astype(jnp.float32)).astype(x.dtype)


def swiglu(x, w_gate, w_up, w_down):
    return (jax.nn.silu(x @ w_gate) * (x @ w_up)) @ w_down


def rope(x, pos):
    half = x.shape[-1] // 2
    inv = ROPE_BASE ** (-jnp.arange(half, dtype=jnp.float32) / half)
    ang = pos.astype(jnp.float32)[:, None] * inv[None, :]
    cos, sin = jnp.cos(ang), jnp.sin(ang)
    if x.ndim == 4:
        cos, sin = cos[:, None, :], sin[:, None, :]
    xf = x.astype(jnp.float32)
    x1, x2 = xf[..., :half], xf[..., half:]
    return jnp.concatenate([x1 * cos - x2 * sin, x1 * sin + x2 * cos], axis=-1).astype(x.dtype)


def s5_mixer(u, s_re0, s_im0, a_re, a_im, log_dt, b_re, b_im, c_re, c_im, d, w_glu_v, w_glu_g):
    f32 = jnp.float32
    b_, n, _ = u.shape
    uf = u.astype(f32)
    ug = uf.reshape(b_, n, N_GROUPS, GROUP_CH)
    a_re, a_im = a_re.astype(f32), a_im.astype(f32)
    dt = jnp.exp(log_dt.astype(f32))[:, None]
    mag = jnp.exp(dt * a_re)
    lam_re, lam_im = mag * jnp.cos(dt * a_im), mag * jnp.sin(dt * a_im)
    den = a_re * a_re + a_im * a_im
    f_re = ((lam_re - 1.0) * a_re + lam_im * a_im) / den
    f_im = (lam_im * a_re - (lam_re - 1.0) * a_im) / den
    b_re, b_im = b_re.astype(f32), b_im.astype(f32)
    bb_re = f_re[..., None] * b_re - f_im[..., None] * b_im
    bb_im = f_re[..., None] * b_im + f_im[..., None] * b_re
    x_re = jnp.einsum('bngc,gpc->bngp', ug, bb_re)
    x_im = jnp.einsum('bngc,gpc->bngp', ug, bb_im)
    s_re0, s_im0 = s_re0.astype(f32), s_im0.astype(f32)
    x_re = x_re.at[:, 0].add(lam_re * s_re0 - lam_im * s_im0)
    x_im = x_im.at[:, 0].add(lam_re * s_im0 + lam_im * s_re0)
    l_re = jnp.broadcast_to(lam_re, x_re.shape)
    l_im = jnp.broadcast_to(lam_im, x_im.shape)

    def combine(e1, e2):
        a1r, a1i, b1r, b1i = e1
        a2r, a2i, b2r, b2i = e2
        return (a2r * a1r - a2i * a1i, a2r * a1i + a2i * a1r,
                a2r * b1r - a2i * b1i + b2r, a2r * b1i + a2i * b1r + b2i)

    _, _, s_re, s_im = lax.associative_scan(combine, (l_re, l_im, x_re, x_im), axis=1)
    y = (jnp.einsum('bngp,gcp->bngc', s_re, c_re.astype(f32))
         - jnp.einsum('bngp,gcp->bngc', s_im, c_im.astype(f32))).reshape(b_, n, D_MODEL)
    y = jax.nn.gelu(y + d.astype(f32) * uf).astype(u.dtype)
    out = (y @ w_glu_v) * jax.nn.sigmoid(y @ w_glu_g)
    return out, s_re[:, -1], s_im[:, -1]


def shared_kv(h, pos, kv_in_norm, w_dkv, c_norm, kpe_norm):
    hk = rmsnorm(h, kv_in_norm)
    ck = hk @ w_dkv
    c = rmsnorm(ck[..., :KV_RANK], c_norm)
    kpe = rope(rmsnorm(ck[..., KV_RANK:], kpe_norm), pos)
    return c, kpe


def latent_attention(q_lat, q_pe, q_pos, key_sets):
    scale = (HEAD_NOPE + HEAD_ROPE) ** -0.5
    scores = []
    for c, kpe, k_pos in key_sets:
        s = jnp.einsum('bqhr,bkr->bhqk', q_lat, c) + jnp.einsum('bqhe,bke->bhqk', q_pe, kpe)
        s = s.astype(jnp.float32) * scale
        scores.append(jnp.where(k_pos[None, :] <= q_pos[:, None], s, NEG_BIG))
    p = jax.nn.softmax(jnp.concatenate(scores, axis=-1), axis=-1)
    out = None
    off = 0
    for c, _, _ in key_sets:
        nk = c.shape[1]
        term = jnp.einsum('bhqk,bkr->bqhr', p[..., off:off + nk].astype(c.dtype), c)
        out = term if out is None else out + term
        off += nk
    return out


def blocked_latent_attention(q_lat, q_pe, q_pos, key_sets):
    b_, n = q_lat.shape[:2]
    if n <= Q_BLOCK or n % Q_BLOCK:
        return latent_attention(q_lat, q_pe, q_pos, key_sets)
    nb = n // Q_BLOCK
    ql = q_lat.reshape(b_, nb, Q_BLOCK, N_HEADS, KV_RANK).transpose(1, 0, 2, 3, 4)
    qp = q_pe.reshape(b_, nb, Q_BLOCK, N_HEADS, HEAD_ROPE).transpose(1, 0, 2, 3, 4)
    pp = q_pos.reshape(nb, Q_BLOCK)
    out = lax.map(lambda a: latent_attention(a[0], a[1], a[2], key_sets), (ql, qp, pp))
    return out.transpose(1, 0, 2, 3, 4).reshape(b_, n, N_HEADS, KV_RANK)


def mla_mixer(h, pos, past_sets, c_new, kpe_new, w_dq, q_norm, w_uq, qn_norm, qr_norm, w_uk, w_uv, w_o):
    b_, n, _ = h.shape
    cq = rmsnorm(h @ w_dq, q_norm)
    q = jnp.einsum('bnr,rhd->bnhd', cq, w_uq)
    q_nope = rmsnorm(q[..., :HEAD_NOPE], qn_norm)
    q_pe = rope(rmsnorm(q[..., HEAD_NOPE:], qr_norm), pos)
    q_lat = jnp.einsum('bnhd,rhd->bnhr', q_nope, w_uk)
    key_sets = past_sets + ((c_new, kpe_new, pos),)
    o_lat = blocked_latent_attention(q_lat, q_pe, pos, key_sets)
    o = jnp.einsum('bnhr,rhv->bnhv', o_lat, w_uv).reshape(b_, n, N_HEADS * HEAD_V)
    return o @ w_o


def setup_inputs(seed: int = 0) -> dict:
    key = jax.random.key(seed)
    ks = jax.random.split(key, 40)
    f32 = jnp.float32
    nrm = lambda k, shape, s: jax.random.normal(k, shape, f32) * s
    gain = lambda k, shape: 1.0 + 0.01 * jax.random.normal(k, shape, f32)
    n_pages = PAST_LEN // PAGE_SIZE
    n_used = DEC_BATCH * n_pages
    n_pool = n_used + n_used // 4
    page_table = jax.random.permutation(ks[0], n_pool)[:n_used].reshape(DEC_BATCH, n_pages).astype(jnp.int32)
    a_im = jnp.pi * jnp.arange(STATE_P, dtype=f32)[None, None, :] + 0.01 * jax.random.normal(ks[14], (N_A_LAYERS, N_GROUPS, STATE_P), f32)
    return {
        "x_prompt": nrm(ks[1], (BATCH, SEQ, D_MODEL), 1.0),
        "x_sample": nrm(ks[2], (DEC_BATCH, DEC_SEQ, D_MODEL), 1.0),
        "state_s5_re": nrm(ks[3], (N_A_LAYERS, DEC_BATCH, N_GROUPS, STATE_P), 0.1),
        "state_s5_im": nrm(ks[4], (N_A_LAYERS, DEC_BATCH, N_GROUPS, STATE_P), 0.1),
        "cache_ckv": nrm(ks[5], (n_pool, PAGE_SIZE, KV_RANK), 1.0),
        "cache_kpe": nrm(ks[6], (n_pool, PAGE_SIZE, HEAD_ROPE), 1.0),
        "page_table": page_table,
        "ffn_norm": gain(ks[7], (DEPTH, 2, D_MODEL)),
        "ffn_w_gate": nrm(ks[8], (DEPTH, 2, D_MODEL, D_FF), D_MODEL ** -0.5),
        "ffn_w_up": nrm(ks[9], (DEPTH, 2, D_MODEL, D_FF), D_MODEL ** -0.5),
        "ffn_w_down": nrm(ks[10], (DEPTH, 2, D_FF, D_MODEL), D_FF ** -0.5),
        "mix_norm": gain(ks[11], (DEPTH, D_MODEL)),
        "s5_a_re": -0.5 - 0.01 * jnp.abs(jax.random.normal(ks[13], (N_A_LAYERS, N_GROUPS, STATE_P), f32)),
        "s5_a_im": a_im,
        "s5_log_dt": jax.random.uniform(ks[15], (N_A_LAYERS, N_GROUPS), f32, minval=math.log(DT_MIN), maxval=math.log(DT_MAX)),
        "s5_b_re": nrm(ks[16], (N_A_LAYERS, N_GROUPS, STATE_P, GROUP_CH), GROUP_CH ** -0.5),
        "s5_b_im": nrm(ks[17], (N_A_LAYERS, N_GROUPS, STATE_P, GROUP_CH), GROUP_CH ** -0.5),
        "s5_c_re": nrm(ks[18], (N_A_LAYERS, N_GROUPS, GROUP_CH, STATE_P), 0.5),
        "s5_c_im": nrm(ks[19], (N_A_LAYERS, N_GROUPS, GROUP_CH, STATE_P), 0.5),
        "s5_d": nrm(ks[20], (N_A_LAYERS, D_MODEL), 1.0),
        "s5_w_glu_v": nrm(ks[21], (N_A_LAYERS, D_MODEL, D_MODEL), D_MODEL ** -0.5),
        "s5_w_glu_g": nrm(ks[22], (N_A_LAYERS, D_MODEL, D_MODEL), D_MODEL ** -0.5),
        "mla_w_dq": nrm(ks[23], (N_B_LAYERS, D_MODEL, Q_RANK), D_MODEL ** -0.5),
        "mla_q_norm": gain(ks[24], (N_B_LAYERS, Q_RANK)),
        "mla_w_uq": nrm(ks[25], (N_B_LAYERS, Q_RANK, N_HEADS, HEAD_NOPE + HEAD_ROPE), Q_RANK ** -0.5),
        "mla_qn_norm": gain(ks[26], (N_B_LAYERS, HEAD_NOPE)),
        "mla_qr_norm": gain(ks[27], (N_B_LAYERS, HEAD_ROPE)),
        "mla_w_o": nrm(ks[28], (N_B_LAYERS, N_HEADS * HEAD_V, D_MODEL), (N_HEADS * HEAD_V) ** -0.5),
        "kv_in_norm": gain(ks[29], (D_MODEL,)),
        "kv_w_dkv": nrm(ks[30], (D_MODEL, KV_RANK + HEAD_ROPE), D_MODEL ** -0.5),
        "kv_c_norm": gain(ks[31], (KV_RANK,)),
        "kv_kpe_norm": gain(ks[32], (HEAD_ROPE,)),
        "kv_w_uk": nrm(ks[33], (KV_RANK, N_HEADS, HEAD_NOPE), KV_RANK ** -0.5),
        "kv_w_uv": nrm(ks[34], (KV_RANK, N_HEADS, HEAD_V), KV_RANK ** -0.5),
    }


def reference(x_prompt, x_sample, state_s5_re, state_s5_im, cache_ckv, cache_kpe, page_table,
              ffn_norm, ffn_w_gate, ffn_w_up, ffn_w_down, mix_norm,
              s5_a_re, s5_a_im, s5_log_dt, s5_b_re, s5_b_im, s5_c_re, s5_c_im, s5_d, s5_w_glu_v, s5_w_glu_g,
              mla_w_dq, mla_q_norm, mla_w_uq, mla_qn_norm, mla_qr_norm, mla_w_o,
              kv_in_norm, kv_w_dkv, kv_c_norm, kv_kpe_norm, kv_w_uk, kv_w_uv):

    def trunk(x, pos, s_re0, s_im0, past_sets):
        new_re, new_im = [], []
        c_sh, kpe_sh = None, None
        for layer in range(DEPTH):
            x = x + FFN_RES * swiglu(rmsnorm(x, ffn_norm[layer, 0]), ffn_w_gate[layer, 0],
                                     ffn_w_up[layer, 0], ffn_w_down[layer, 0])
            h = rmsnorm(x, mix_norm[layer])
            if layer < N_A_LAYERS:
                a = layer
                m, s_re, s_im = s5_mixer(h, s_re0[a], s_im0[a], s5_a_re[a], s5_a_im[a], s5_log_dt[a],
                                         s5_b_re[a], s5_b_im[a], s5_c_re[a], s5_c_im[a], s5_d[a],
                                         s5_w_glu_v[a], s5_w_glu_g[a])
                new_re.append(s_re)
                new_im.append(s_im)
            else:
                bl = layer - N_A_LAYERS
                m = mla_mixer(h, pos, past_sets, c_sh, kpe_sh, mla_w_dq[bl], mla_q_norm[bl], mla_w_uq[bl],
                              mla_qn_norm[bl], mla_qr_norm[bl], kv_w_uk, kv_w_uv, mla_w_o[bl])
            x = x + m
            x = x + FFN_RES * swiglu(rmsnorm(x, ffn_norm[layer, 1]), ffn_w_gate[layer, 1],
                                     ffn_w_up[layer, 1], ffn_w_down[layer, 1])
            if layer == N_A_LAYERS - 1:
                c_sh, kpe_sh = shared_kv(x, pos, kv_in_norm, kv_w_dkv, kv_c_norm, kv_kpe_norm)
        return x, jnp.stack(new_re), jnp.stack(new_im), c_sh, kpe_sh

    b_p, n_p, _ = x_prompt.shape
    pos_p = jnp.arange(n_p, dtype=jnp.int32)
    zero_state = jnp.zeros((N_A_LAYERS, b_p, N_GROUPS, STATE_P), jnp.float32)
    y_p, s5_re_p, s5_im_p, ckv_p, kpe_p = trunk(x_prompt, pos_p, zero_state, zero_state, ())

    b_s, n_s, _ = x_sample.shape
    n_pages = page_table.shape[1]
    past_len = n_pages * PAGE_SIZE
    c_past = cache_ckv[page_table].reshape(b_s, past_len, KV_RANK)
    kpe_past = cache_kpe[page_table].reshape(b_s, past_len, HEAD_ROPE)
    past_pos = jnp.arange(past_len, dtype=jnp.int32)
    pos_s = past_len + jnp.arange(n_s, dtype=jnp.int32)
    y_s, s5_re_s, s5_im_s, ckv_s, kpe_s = trunk(x_sample, pos_s, state_s5_re, state_s5_im,
                                                ((c_past, kpe_past, past_pos),))
    return (y_p, y_s, s5_re_p, s5_im_p, s5_re_s, s5_im_s, ckv_p, kpe_p, ckv_s, kpe_s)
```

```python
import functools
import math

import jax
import jax.numpy as jnp
from jax import lax
from jax.experimental import pallas as pl
from jax.experimental.pallas import tpu as pltpu

F32 = jnp.float32
BF16 = jnp.bfloat16

EPS = 1e-6
FFN_RES = 0.5
NEG_BIG = -1e30
ROPE_BASE = 10000.0

GROUP_CH = 16
LANES = 128
SUBLANES = 8
S5_SLAB = 256
VMEM_LIMIT = 56 * 1024 * 1024


def _tile(n, pref):
    if n <= pref:
        return n
    for t in range(pref, 7, -1):
        if n % t == 0 and t % 8 == 0:
            return t
    return n


def _rms(x, gain):
    ms = jnp.mean(x * x, axis=-1, keepdims=True)
    return x * lax.rsqrt(ms + EPS) * gain


def _cparams(sem):
    return pltpu.CompilerParams(dimension_semantics=sem, vmem_limit_bytes=VMEM_LIMIT)


def _ffn_body(x_ref, g_ref, wg_ref, wu_ref, wd_ref, o_ref, xn_ref, acc_ref):
    j = pl.program_id(1)

    @pl.when(j == 0)
    def _():
        xn_ref[...] = _rms(x_ref[...], g_ref[...]).astype(BF16)
        acc_ref[...] = jnp.zeros_like(acc_ref)

    xn = xn_ref[...]
    gate = jnp.dot(xn, wg_ref[...], preferred_element_type=F32)
    up = jnp.dot(xn, wu_ref[...], preferred_element_type=F32)
    hid = (gate * jax.nn.sigmoid(gate) * up).astype(BF16)
    acc_ref[...] += jnp.dot(hid, wd_ref[...], preferred_element_type=F32)

    @pl.when(j == pl.num_programs(1) - 1)
    def _():
        o_ref[...] = x_ref[...] + FFN_RES * acc_ref[...]


def _ffn(x, gain, w_gate, w_up, w_down, *, tm_pref=512, tf_pref=1408):
    m, d = x.shape
    ff = w_gate.shape[1]
    tm = _tile(m, tm_pref)
    tf = tf_pref if ff % tf_pref == 0 else ff
    return pl.pallas_call(
        _ffn_body,
        grid=(m // tm, ff // tf),
        in_specs=[
            pl.BlockSpec((tm, d), lambda i, j: (i, 0)),
            pl.BlockSpec((1, d), lambda i, j: (0, 0)),
            pl.BlockSpec((d, tf), lambda i, j: (0, j)),
            pl.BlockSpec((d, tf), lambda i, j: (0, j)),
            pl.BlockSpec((tf, d), lambda i, j: (j, 0)),
        ],
        out_specs=pl.BlockSpec((tm, d), lambda i, j: (i, 0)),
        out_shape=jax.ShapeDtypeStruct((m, d), F32),
        scratch_shapes=[pltpu.VMEM((tm, d), BF16), pltpu.VMEM((tm, d), F32)],
        compiler_params=_cparams(("parallel", "arbitrary")),
        name="ffn",
    )(x, gain.reshape(1, d), w_gate, w_up, w_down)


def _s5_params(a_re, a_im, log_dt, b_re, b_im, c_re, c_im):
    g, p = a_re.shape
    dt = jnp.exp(log_dt)[:, None]
    mag = jnp.exp(dt * a_re)
    lam_re, lam_im = mag * jnp.cos(dt * a_im), mag * jnp.sin(dt * a_im)
    den = a_re * a_re + a_im * a_im
    f_re = ((lam_re - 1.0) * a_re + lam_im * a_im) / den
    f_im = (lam_im * a_re - (lam_re - 1.0) * a_im) / den
    bb_re = f_re[..., None] * b_re - f_im[..., None] * b_im
    bb_im = f_re[..., None] * b_im + f_im[..., None] * b_re
    gs = S5_SLAB // GROUP_CH
    ns = g // gs
    eye = jnp.eye(gs, dtype=F32)
    nk = gs * p // LANES

    def pack_b(bb):
        bb = bb.reshape(ns, gs, p, GROUP_CH)
        w = jnp.einsum("sgpc,gh->sgchp", bb, eye)
        return w.reshape(ns, S5_SLAB, nk, LANES)

    wb = jnp.stack([pack_b(bb_re), pack_b(bb_im)], axis=3)
    wb = wb.reshape(ns, S5_SLAB, 2 * gs * p).astype(BF16)

    def pack_c(cc):
        cc = cc.reshape(ns, gs, GROUP_CH, p)
        w = jnp.einsum("sgcp,gh->shpgc", cc, eye)
        return w.reshape(ns, nk, LANES, S5_SLAB)

    wc = jnp.stack([pack_c(c_re), -pack_c(c_im)], axis=2)
    wc = wc.reshape(ns, 2 * gs * p, S5_SLAB).astype(BF16)
    return lam_re, lam_im, wb, wc


def _s5_prompt_body(x_ref, g_ref, wb_ref, wc_ref, lr_ref, li_ref, d_ref, wv_ref, wgl_ref,
                    o_ref, sre_ref, sim_ref, xr_ref, xi_ref, cr_ref, ci_ref, y_ref, *, tc, ns, nk):
    t_idx = pl.program_id(1)

    @pl.when(t_idx == 0)
    def _():
        cr_ref[...] = jnp.zeros_like(cr_ref)
        ci_ref[...] = jnp.zeros_like(ci_ref)

    x = x_ref[0]
    h = _rms(x, g_ref[...])
    hb = h.astype(BF16)
    for s in range(ns):
        us = hb[:, S5_SLAB * s:S5_SLAB * (s + 1)]
        for k in range(nk):
            xk = jnp.dot(us, wb_ref[s, :, 2 * LANES * k:2 * LANES * (k + 1)],
                         preferred_element_type=F32)
            xr_ref[s, pl.ds(k, tc, stride=nk), :] = xk[:, :LANES]
            xi_ref[s, pl.ds(k, tc, stride=nk), :] = xk[:, LANES:]

    lam = [(lr_ref[s], li_ref[s]) for s in range(ns)]

    def step(t, carry):
        row = pl.multiple_of(t * nk, nk)
        out = []
        for s in range(ns):
            sr, si = carry[s]
            lr, li = lam[s]
            nr = lr * sr - li * si + xr_ref[s, pl.ds(row, nk), :]
            ni = lr * si + li * sr + xi_ref[s, pl.ds(row, nk), :]
            xr_ref[s, pl.ds(row, nk), :] = nr
            xi_ref[s, pl.ds(row, nk), :] = ni
            out.append((nr, ni))
        return tuple(out)

    init = tuple((cr_ref[s], ci_ref[s]) for s in range(ns))
    fin = lax.fori_loop(0, tc, step, init, unroll=4)
    for s in range(ns):
        cr_ref[s] = fin[s][0]
        ci_ref[s] = fin[s][1]

    for s in range(ns):
        acc = None
        for k in range(nk):
            sk = jnp.concatenate([xr_ref[s, pl.ds(k, tc, stride=nk), :],
                                  xi_ref[s, pl.ds(k, tc, stride=nk), :]], axis=1).astype(BF16)
            part = jnp.dot(sk, wc_ref[s, 2 * LANES * k:2 * LANES * (k + 1), :],
                           preferred_element_type=F32)
            acc = part if acc is None else acc + part
        y_ref[:, S5_SLAB * s:S5_SLAB * (s + 1)] = acc

    y = jax.nn.gelu(y_ref[...] + d_ref[...] * h, approximate=True).astype(BF16)
    val = jnp.dot(y, wv_ref[...], preferred_element_type=F32)
    gate = jnp.dot(y, wgl_ref[...], preferred_element_type=F32)
    o_ref[0] = x + val * jax.nn.sigmoid(gate)

    @pl.when(t_idx == pl.num_programs(1) - 1)
    def _():
        sre_ref[0] = cr_ref[...].reshape(ns * nk, LANES)
        sim_ref[0] = ci_ref[...].reshape(ns * nk, LANES)


def _s5_prompt(x, gain, lam_re, lam_im, wb, wc, d_skip, w_v, w_g, *, tc_pref=128):
    b, t, d = x.shape
    ns = wb.shape[0]
    nk = wb.shape[2] // (2 * LANES)
    assert nk == SUBLANES, "one token's slab state must fill whole vregs"
    tc = _tile(t, tc_pref)
    lr = lam_re.reshape(ns, nk, LANES)
    li = lam_im.reshape(ns, nk, LANES)
    const3 = lambda i, j: (0, 0, 0)
    const2 = lambda i, j: (0, 0)
    out, s_re, s_im = pl.pallas_call(
        functools.partial(_s5_prompt_body, tc=tc, ns=ns, nk=nk),
        grid=(b, t // tc),
        in_specs=[
            pl.BlockSpec((1, tc, d), lambda i, j: (i, j, 0)),
            pl.BlockSpec((1, d), const2),
            pl.BlockSpec(wb.shape, const3),
            pl.BlockSpec(wc.shape, const3),
            pl.BlockSpec(lr.shape, const3),
            pl.BlockSpec(li.shape, const3),
            pl.BlockSpec((1, d), const2),
            pl.BlockSpec(w_v.shape, const2),
            pl.BlockSpec(w_g.shape, const2),
        ],
        out_specs=[
            pl.BlockSpec((1, tc, d), lambda i, j: (i, j, 0)),
            pl.BlockSpec((1, ns * nk, LANES), lambda i, j: (i, 0, 0)),
            pl.BlockSpec((1, ns * nk, LANES), lambda i, j: (i, 0, 0)),
        ],
        out_shape=[
            jax.ShapeDtypeStruct((b, t, d), F32),
            jax.ShapeDtypeStruct((b, ns * nk, LANES), F32),
            jax.ShapeDtypeStruct((b, ns * nk, LANES), F32),
        ],
        scratch_shapes=[
            pltpu.VMEM((ns, tc * nk, LANES), F32),
            pltpu.VMEM((ns, tc * nk, LANES), F32),
            pltpu.VMEM((ns, nk, LANES), F32),
            pltpu.VMEM((ns, nk, LANES), F32),
            pltpu.VMEM((tc, d), F32),
        ],
        compiler_params=_cparams(("parallel", "arbitrary")),
        name="s5_prompt",
    )(x, gain.reshape(1, d), wb, wc, lr, li, d_skip.reshape(1, d), w_v, w_g)
    return out, s_re, s_im


def _s5_step_body(x_ref, g_ref, s0r_ref, s0i_ref, wb_ref, wc_ref, lr_ref, li_ref, d_ref,
                  wv_ref, wgl_ref, o_ref, sre_ref, sim_ref, y_ref, *, ns, nk):
    x = x_ref[...]
    h = _rms(x, g_ref[...])
    hb = h.astype(BF16)
    for s in range(ns):
        xs = jnp.dot(hb[:, S5_SLAB * s:S5_SLAB * (s + 1)], wb_ref[s], preferred_element_type=F32)
        acc = None
        for k in range(nk):
            col = (s * nk + k) * LANES
            lr = lr_ref[:, col:col + LANES]
            li = li_ref[:, col:col + LANES]
            s0r = s0r_ref[:, col:col + LANES]
            s0i = s0i_ref[:, col:col + LANES]
            nr = lr * s0r - li * s0i + xs[:, 2 * LANES * k:2 * LANES * k + LANES]
            ni = lr * s0i + li * s0r + xs[:, 2 * LANES * k + LANES:2 * LANES * (k + 1)]
            sre_ref[:, col:col + LANES] = nr
            sim_ref[:, col:col + LANES] = ni
            part = (jnp.dot(nr.astype(BF16), wc_ref[s, 2 * LANES * k:2 * LANES * k + LANES, :],
                            preferred_element_type=F32)
                    + jnp.dot(ni.astype(BF16), wc_ref[s, 2 * LANES * k + LANES:2 * LANES * (k + 1), :],
                              preferred_element_type=F32))
            acc = part if acc is None else acc + part
        y_ref[:, S5_SLAB * s:S5_SLAB * (s + 1)] = acc
    y = jax.nn.gelu(y_ref[...] + d_ref[...] * h, approximate=True).astype(BF16)
    val = jnp.dot(y, wv_ref[...], preferred_element_type=F32)
    gate = jnp.dot(y, wgl_ref[...], preferred_element_type=F32)
    o_ref[...] = x + val * jax.nn.sigmoid(gate)


def _s5_step(x, gain, s0_re, s0_im, lam_re, lam_im, wb, wc, d_skip, w_v, w_g):
    m, d = x.shape
    ns = wb.shape[0]
    nk = wb.shape[2] // (2 * LANES)
    nstate = s0_re.shape[1]
    args = (x, gain.reshape(1, d), s0_re, s0_im, wb, wc, lam_re.reshape(1, nstate),
            lam_im.reshape(1, nstate), d_skip.reshape(1, d), w_v, w_g)
    return pl.pallas_call(
        functools.partial(_s5_step_body, ns=ns, nk=nk),
        out_shape=[
            jax.ShapeDtypeStruct((m, d), F32),
            jax.ShapeDtypeStruct((m, nstate), F32),
            jax.ShapeDtypeStruct((m, nstate), F32),
        ],
        scratch_shapes=[pltpu.VMEM((m, d), F32)],
        compiler_params=pltpu.CompilerParams(vmem_limit_bytes=VMEM_LIMIT),
        name="s5_step",
    )(*args)


def _rope_tables(pos, half):
    inv = ROPE_BASE ** (-jnp.arange(half, dtype=F32) / half)
    ang = pos.astype(F32)[:, None] * inv[None, :]
    return jnp.cos(ang), jnp.sin(ang)


def _lane_block(parts, n):
    out = jnp.zeros((n, LANES), F32)
    for off, arr in parts:
        out = out.at[:, off:off + arr.shape[1]].set(arr)
    return out


def _kv_body(x_ref, g_ref, w_ref, cn_ref, gpe_ref, cos_ref, sin_ref, *rest, rank, rope, with_heads, nh):
    if with_heads:
        wuk_ref, wuvt_ref, c_ref, kpe_ref, k_ref, vt_ref = rest
    else:
        c_ref, kpe_ref = rest
    hk = _rms(x_ref[...], g_ref[...]).astype(BF16)
    ck = jnp.dot(hk, w_ref[...], preferred_element_type=F32)
    c = _rms(ck[:, :rank], cn_ref[...])
    c_ref[...] = c

    def pe_block(i):
        pe = ck[:, rank + 2 * LANES * i:rank + 2 * LANES * i + LANES]
        rot = ck[:, rank + 2 * LANES * i + LANES:rank + 2 * LANES * (i + 1)]
        r = lax.rsqrt(jnp.sum(pe * pe, axis=-1, keepdims=True) * (1.0 / rope) + EPS)
        return (pe * gpe_ref[2 * i:2 * i + 1] * cos_ref[:, LANES * i:LANES * (i + 1)]
                + rot * gpe_ref[2 * i + 1:2 * i + 2] * sin_ref[:, LANES * i:LANES * (i + 1)]) * r

    kpe_ref[...] = pe_block(0)[:, :rope]
    if with_heads:
        cb = c.astype(BF16)
        pe_mid = pe_block(1)
        kn = jnp.dot(cb, wuk_ref[...], preferred_element_type=F32)
        for h in range(nh):
            k_ref[:, LANES * h:LANES * (h + 1)] = (kn[:, LANES * h:LANES * (h + 1)] + pe_mid).astype(BF16)
        vt_ref[...] = lax.dot_general(wuvt_ref[...], cb, (((1,), (1,)), ((), ())),
                                      preferred_element_type=F32).astype(BF16)


def _kv(x, pos, kv_in_norm, w_dkv, c_norm, kpe_norm, w_uk, w_uv, *, with_heads, tm_pref=512):
    b, t, d = x.shape
    rank = c_norm.shape[0]
    rope = kpe_norm.shape[0]
    half = rope // 2
    nh, nope = w_uk.shape[1], w_uk.shape[2]
    hv = w_uv.shape[2]
    tm = _tile(t, tm_pref)
    nt = t // tm
    m = b * t
    w_pe = w_dkv[:, rank:]
    w_rot = jnp.concatenate([-w_pe[:, half:], w_pe[:, :half]], axis=1)
    zero = lambda n: jnp.zeros((d, n), F32)
    w_ext = jnp.concatenate([
        w_dkv[:, :rank],
        w_pe, zero(LANES - rope), w_rot, zero(LANES - rope),
        zero(nope), w_pe, zero(LANES - nope - rope), zero(nope), w_rot, zero(LANES - nope - rope),
    ], axis=1).astype(BF16)
    g_pe = kpe_norm[None, :]
    g_rot = jnp.concatenate([kpe_norm[half:], kpe_norm[:half]])[None, :]
    gpe = jnp.concatenate([_lane_block([(0, g_pe)], 1), _lane_block([(0, g_rot)], 1),
                           _lane_block([(nope, g_pe)], 1), _lane_block([(nope, g_rot)], 1)], axis=0)
    cos, sin = _rope_tables(pos, half)
    cos2, sin2 = jnp.concatenate([cos, cos], 1), jnp.concatenate([sin, sin], 1)
    cos_t = jnp.concatenate([_lane_block([(0, cos2)], t), _lane_block([(nope, cos2)], t)], axis=1)
    sin_t = jnp.concatenate([_lane_block([(0, sin2)], t), _lane_block([(nope, sin2)], t)], axis=1)

    in_specs = [
        pl.BlockSpec((tm, d), lambda i: (i, 0)),
        pl.BlockSpec((1, d), lambda i: (0, 0)),
        pl.BlockSpec(w_ext.shape, lambda i: (0, 0)),
        pl.BlockSpec((1, rank), lambda i: (0, 0)),
        pl.BlockSpec(gpe.shape, lambda i: (0, 0)),
        pl.BlockSpec((tm, 2 * LANES), lambda i: (i % nt, 0)),
        pl.BlockSpec((tm, 2 * LANES), lambda i: (i % nt, 0)),
    ]
    args = [x.reshape(m, d), kv_in_norm.reshape(1, d), w_ext, c_norm.reshape(1, rank), gpe, cos_t, sin_t]
    out_specs = [pl.BlockSpec((tm, rank), lambda i: (i, 0)), pl.BlockSpec((tm, rope), lambda i: (i, 0))]
    out_shape = [jax.ShapeDtypeStruct((m, rank), F32), jax.ShapeDtypeStruct((m, rope), F32)]
    if with_heads:
        wuk = jnp.concatenate([w_uk, jnp.zeros((rank, nh, LANES - nope), F32)], axis=2)
        wuk = wuk.reshape(rank, nh * LANES).astype(BF16)
        wuvt = jnp.transpose(w_uv, (1, 2, 0)).reshape(nh * hv, rank).astype(BF16)
        in_specs += [pl.BlockSpec(wuk.shape, lambda i: (0, 0)), pl.BlockSpec(wuvt.shape, lambda i: (0, 0))]
        args += [wuk, wuvt]
        out_specs += [pl.BlockSpec((tm, nh * LANES), lambda i: (i, 0)),
                      pl.BlockSpec((None, nh * hv, tm), lambda i: (i // nt, 0, i % nt))]
        out_shape += [jax.ShapeDtypeStruct((m, nh * LANES), BF16),
                      jax.ShapeDtypeStruct((b, nh * hv, t), BF16)]
    return pl.pallas_call(
        functools.partial(_kv_body, rank=rank, rope=rope, with_heads=with_heads, nh=nh),
        grid=(m // tm,),
        in_specs=in_specs,
        out_specs=out_specs,
        out_shape=out_shape,
        compiler_params=_cparams(("parallel",)),
        name="shared_kv",
    )(*args)


def _q_body(x_ref, g_ref, wdq_ref, qn_ref, w1_ref, w2_ref, gain_ref, cos_ref, sin_ref, q_ref,
            *, nh, nope, rope):
    hb = _rms(x_ref[...], g_ref[...]).astype(BF16)
    cq = jnp.dot(hb, wdq_ref[...], preferred_element_type=F32)
    cqb = _rms(cq, qn_ref[...]).astype(BF16)
    qm = jnp.dot(cqb, w1_ref[...], preferred_element_type=F32)
    qr = jnp.dot(cqb, w2_ref[...], preferred_element_type=F32)
    lane = lax.broadcasted_iota(jnp.int32, (1, LANES), 1)
    is_nope = lane < nope
    is_pe = jnp.logical_and(lane >= nope, lane < nope + rope)
    gain, gain_rot = gain_ref[0:1], gain_ref[1:2]
    cos, sin = cos_ref[...], sin_ref[...]
    for h in range(nh):
        blk = qm[:, LANES * h:LANES * (h + 1)]
        rot = qr[:, LANES * h:LANES * (h + 1)]
        sq = blk * blk
        s_n = jnp.sum(jnp.where(is_nope, sq, 0.0), axis=-1, keepdims=True)
        s_p = jnp.sum(jnp.where(is_pe, sq, 0.0), axis=-1, keepdims=True)
        r = jnp.where(is_nope, lax.rsqrt(s_n * (1.0 / nope) + EPS), lax.rsqrt(s_p * (1.0 / rope) + EPS))
        q_ref[:, LANES * h:LANES * (h + 1)] = ((blk * gain * cos + rot * gain_rot * sin) * r).astype(BF16)


def _queries(x, pos, mix_norm, w_dq, q_norm, w_uq, qn_norm, qr_norm, *, tm_pref=512):
    b, t, d = x.shape
    qrank = w_dq.shape[1]
    nh = w_uq.shape[1]
    nope, rope = qn_norm.shape[0], qr_norm.shape[0]
    half = rope // 2
    scale = (nope + rope) ** -0.5
    tm = _tile(t, tm_pref)
    nt = t // tm
    m = b * t
    w_n, w_p = w_uq[:, :, :nope], w_uq[:, :, nope:]
    pad = jnp.zeros((qrank, nh, LANES - nope - rope), F32)
    w1 = jnp.concatenate([w_n, w_p, pad], axis=2).reshape(qrank, nh * LANES).astype(BF16)
    w_rot = jnp.concatenate([-w_p[:, :, half:], w_p[:, :, :half]], axis=2)
    w2 = jnp.concatenate([jnp.zeros_like(w_n), w_rot, pad], axis=2).reshape(qrank, nh * LANES).astype(BF16)
    g_rot = jnp.concatenate([qr_norm[half:], qr_norm[:half]])
    gain = jnp.concatenate([
        _lane_block([(0, qn_norm[None, :] * scale), (nope, qr_norm[None, :] * scale)], 1),
        _lane_block([(nope, g_rot[None, :] * scale)], 1)], axis=0)
    cos, sin = _rope_tables(pos, half)
    cos_t = _lane_block([(0, jnp.ones((t, nope), F32)), (nope, jnp.concatenate([cos, cos], 1))], t)
    sin_t = _lane_block([(nope, jnp.concatenate([sin, sin], 1))], t)
    return pl.pallas_call(
        functools.partial(_q_body, nh=nh, nope=nope, rope=rope),
        grid=(m // tm,),
        in_specs=[
            pl.BlockSpec((tm, d), lambda i: (i, 0)),
            pl.BlockSpec((1, d), lambda i: (0, 0)),
            pl.BlockSpec(w_dq.shape, lambda i: (0, 0)),
            pl.BlockSpec((1, qrank), lambda i: (0, 0)),
            pl.BlockSpec(w1.shape, lambda i: (0, 0)),
            pl.BlockSpec(w2.shape, lambda i: (0, 0)),
            pl.BlockSpec(gain.shape, lambda i: (0, 0)),
            pl.BlockSpec((tm, LANES), lambda i: (i % nt, 0)),
            pl.BlockSpec((tm, LANES), lambda i: (i % nt, 0)),
        ],
        out_specs=pl.BlockSpec((tm, nh * LANES), lambda i: (i, 0)),
        out_shape=jax.ShapeDtypeStruct((m, nh * LANES), BF16),
        compiler_params=_cparams(("parallel",)),
        name="mla_queries",
    )(x.reshape(m, d), mix_norm.reshape(1, d), w_dq.astype(BF16), q_norm.reshape(1, qrank),
      w1, w2, gain, cos_t, sin_t)


def _attn_prompt_body(q_ref, k_ref, vt_ref, o_ref, *, tq):
    qi = pl.program_id(2)
    q = q_ref[...]
    hv = vt_ref.shape[0]

    def block(ki, carry, masked):
        m_i, l_i, acc = carry
        off = pl.multiple_of(ki * tq, tq)
        k = k_ref[pl.ds(off, tq), :]
        s = lax.dot_general(k, q, (((1,), (1,)), ((), ())), preferred_element_type=F32)
        if masked:
            kpos = lax.broadcasted_iota(jnp.int32, (tq, tq), 0)
            qpos = lax.broadcasted_iota(jnp.int32, (tq, tq), 1)
            s = jnp.where(kpos <= qpos, s, NEG_BIG)
        m_new = jnp.maximum(m_i, jnp.max(s, axis=0, keepdims=True))
        alpha = jnp.exp(m_i - m_new)
        p = jnp.exp(s - m_new)
        l_new = alpha * l_i + jnp.sum(p, axis=0, keepdims=True)
        pv = jnp.dot(vt_ref[:, pl.ds(off, tq)], p.astype(BF16), preferred_element_type=F32)
        return m_new, l_new, alpha * acc + pv

    init = (jnp.full((1, tq), NEG_BIG, F32), jnp.zeros((1, tq), F32), jnp.zeros((hv, tq), F32))
    carry = lax.fori_loop(0, qi, lambda ki, c: block(ki, c, False), init)
    _, l_i, acc = block(qi, carry, True)
    o_ref[...] = (acc / l_i).astype(BF16)


def _attn_prompt(q, k, vt, *, nh, hv, tq_pref=512):
    b, t, _ = q.shape
    tq = _tile(t, tq_pref)
    return pl.pallas_call(
        functools.partial(_attn_prompt_body, tq=tq),
        grid=(b, nh, t // tq),
        in_specs=[
            pl.BlockSpec((None, tq, LANES), lambda i, h, j: (i, j, h)),
            pl.BlockSpec((None, t, LANES), lambda i, h, j: (i, 0, h)),
            pl.BlockSpec((None, hv, t), lambda i, h, j: (i, h, 0)),
        ],
        out_specs=pl.BlockSpec((None, hv, tq), lambda i, h, j: (i, h, j)),
        out_shape=jax.ShapeDtypeStruct((b, nh * hv, t), BF16),
        compiler_params=_cparams(("parallel", "parallel", "arbitrary")),
        name="attn_prompt",
    )(q, k, vt)


def _oproj_t_body(ot_ref, x_ref, wo_ref, o_ref):
    m = lax.dot_general(ot_ref[...], wo_ref[...], (((0,), (0,)), ((), ())), preferred_element_type=F32)
    o_ref[...] = x_ref[...] + m


def _oproj_t(ot, x, w_o, *, tm_pref=512):
    b, t, d = x.shape
    e = ot.shape[1]
    tm = _tile(t, tm_pref)
    return pl.pallas_call(
        _oproj_t_body,
        grid=(b, t // tm),
        in_specs=[
            pl.BlockSpec((None, e, tm), lambda i, j: (i, 0, j)),
            pl.BlockSpec((None, tm, d), lambda i, j: (i, j, 0)),
            pl.BlockSpec((e, d), lambda i, j: (0, 0)),
        ],
        out_specs=pl.BlockSpec((None, tm, d), lambda i, j: (i, j, 0)),
        out_shape=jax.ShapeDtypeStruct((b, t, d), F32),
        compiler_params=_cparams(("parallel", "parallel")),
        name="attn_out_prompt",
    )(ot, x, w_o.astype(BF16))


def _absorb_body(q_ref, w_ref, o_ref):
    o_ref[...] = jnp.dot(q_ref[...], w_ref[...], preferred_element_type=F32).astype(BF16)


def _absorb(q, w_uk, rope):
    m = q.shape[0]
    rank, nh, nope = w_uk.shape
    width = rank + LANES
    w = jnp.zeros((nh, LANES, width), F32)
    w = w.at[:, :nope, :rank].set(jnp.transpose(w_uk, (1, 2, 0)))
    w = w.at[:, nope:nope + rope, rank:rank + rope].set(jnp.broadcast_to(jnp.eye(rope, dtype=F32), (nh, rope, rope)))
    out = pl.pallas_call(
        _absorb_body,
        grid=(nh,),
        in_specs=[pl.BlockSpec((m, LANES), lambda h: (0, h)),
                  pl.BlockSpec((None, LANES, width), lambda h: (h, 0, 0))],
        out_specs=pl.BlockSpec((m, width), lambda h: (0, h)),
        out_shape=jax.ShapeDtypeStruct((m, nh * width), BF16),
        compiler_params=_cparams(("parallel",)),
        name="absorb_queries",
    )(q, w.astype(BF16))
    return out.reshape(m, nh, width)


def _attn_sample_body(pt_ref, q_ref, cn_ref, pn_ref, *rest, npg, rank, rope):
    c_refs = rest[:npg]
    p_refs = rest[npg:2 * npg]
    o_ref, m_ref, l_ref, acc_ref = rest[2 * npg:]
    j = pl.program_id(1)

    @pl.when(j == 0)
    def _():
        m_ref[...] = jnp.full_like(m_ref, NEG_BIG)
        l_ref[...] = jnp.zeros_like(l_ref)
        acc_ref[...] = jnp.zeros_like(acc_ref)

    q_lat = q_ref[0, :, :rank]
    q_pe = q_ref[0, :, rank:rank + rope]
    contract_last = (((1,), (1,)), ((), ()))
    cs = [c_refs[i][0].astype(BF16) for i in range(npg)]
    s = jnp.concatenate([
        lax.dot_general(q_lat, cs[i], contract_last, preferred_element_type=F32)
        + lax.dot_general(q_pe, p_refs[i][0].astype(BF16), contract_last, preferred_element_type=F32)
        for i in range(npg)], axis=1)
    m_new = jnp.maximum(m_ref[...], jnp.max(s, axis=1, keepdims=True))
    alpha = jnp.exp(m_ref[...] - m_new)
    p = jnp.exp(s - m_new)
    l_ref[...] = alpha * l_ref[...] + jnp.sum(p, axis=1, keepdims=True)
    pb = p.astype(BF16)
    page = cs[0].shape[0]
    pv = None
    for i in range(npg):
        part = jnp.dot(pb[:, page * i:page * (i + 1)], cs[i], preferred_element_type=F32)
        pv = part if pv is None else pv + part
    acc_ref[...] = alpha * acc_ref[...] + pv
    m_ref[...] = m_new

    @pl.when(j == pl.num_programs(1) - 1)
    def _():
        c_new = cn_ref[0].astype(BF16)
        p_new = pn_ref[0].astype(BF16)
        s_new = (jnp.sum(q_lat.astype(F32) * c_new.astype(F32), axis=1, keepdims=True)
                 + jnp.sum(q_pe.astype(F32) * p_new.astype(F32), axis=1, keepdims=True))
        m_fin = jnp.maximum(m_ref[...], s_new)
        a_old = jnp.exp(m_ref[...] - m_fin)
        p_n = jnp.exp(s_new - m_fin)
        l_fin = a_old * l_ref[...] + p_n
        acc = a_old * acc_ref[...] + p_n.astype(BF16).astype(F32) * c_new.astype(F32)
        o_ref[0] = acc / l_fin


def _attn_sample(qabs, cache_ckv, cache_kpe, page_table, c_new, kpe_new, *, npg_pref=16):
    m, nh, width = qabs.shape
    _, page, rank = cache_ckv.shape
    rope = cache_kpe.shape[2]
    n_pages = page_table.shape[1]
    npg = npg_pref if n_pages % npg_pref == 0 else n_pages
    nchunk = n_pages // npg

    def page_map(i):
        return lambda b, j, pt: (pt[b * n_pages + j * npg + i], 0, 0)

    in_specs = [
        pl.BlockSpec((1, nh, width), lambda b, j, pt: (b, 0, 0)),
        pl.BlockSpec((1, 1, rank), lambda b, j, pt: (b, 0, 0)),
        pl.BlockSpec((1, 1, rope), lambda b, j, pt: (b, 0, 0)),
    ]
    in_specs += [pl.BlockSpec((1, page, rank), page_map(i)) for i in range(npg)]
    in_specs += [pl.BlockSpec((1, page, rope), page_map(i)) for i in range(npg)]
    grid_spec = pltpu.PrefetchScalarGridSpec(
        num_scalar_prefetch=1,
        grid=(m, nchunk),
        in_specs=in_specs,
        out_specs=pl.BlockSpec((1, nh, rank), lambda b, j, pt: (b, 0, 0)),
        scratch_shapes=[pltpu.VMEM((nh, 1), F32), pltpu.VMEM((nh, 1), F32), pltpu.VMEM((nh, rank), F32)],
    )
    return pl.pallas_call(
        functools.partial(_attn_sample_body, npg=npg, rank=rank, rope=rope),
        grid_spec=grid_spec,
        out_shape=jax.ShapeDtypeStruct((m, nh, rank), F32),
        compiler_params=_cparams(("parallel", "arbitrary")),
        name="attn_sample",
    )(page_table.reshape(-1), qabs, c_new.reshape(m, 1, rank), kpe_new.reshape(m, 1, rope),
      *([cache_ckv] * npg), *([cache_kpe] * npg))


def _oproj_lat_body(ol_ref, x_ref, wuv_ref, wo_ref, o_ref, *, nh, rank, hv):
    acc = x_ref[...]
    for h in range(nh):
        oh = jnp.dot(ol_ref[:, rank * h:rank * (h + 1)].astype(BF16), wuv_ref[h], preferred_element_type=F32)
        acc = acc + jnp.dot(oh.astype(BF16), wo_ref[hv * h:hv * (h + 1), :], preferred_element_type=F32)
    o_ref[...] = acc


def _oproj_lat(o_lat, x, w_uv, w_o):
    m, nh, rank = o_lat.shape
    hv = w_uv.shape[2]
    return pl.pallas_call(
        functools.partial(_oproj_lat_body, nh=nh, rank=rank, hv=hv),
        out_shape=jax.ShapeDtypeStruct(x.shape, F32),
        compiler_params=pltpu.CompilerParams(vmem_limit_bytes=VMEM_LIMIT),
        name="attn_out_sample",
    )(o_lat.reshape(m, nh * rank), x, jnp.transpose(w_uv, (1, 0, 2)).astype(BF16), w_o.astype(BF16))


def kernel(x_prompt, x_sample, state_s5_re, state_s5_im, cache_ckv, cache_kpe, page_table, ffn_norm, ffn_w_gate, ffn_w_up, ffn_w_down, mix_norm, s5_a_re, s5_a_im, s5_log_dt, s5_b_re, s5_b_im, s5_c_re, s5_c_im, s5_d, s5_w_glu_v, s5_w_glu_g, mla_w_dq, mla_q_norm, mla_w_uq, mla_qn_norm, mla_qr_norm, mla_w_o, kv_in_norm, kv_w_dkv, kv_c_norm, kv_kpe_norm, kv_w_uk, kv_w_uv):
    bp, tp, d = x_prompt.shape
    bs, ts, _ = x_sample.shape
    assert ts == 1, "the sample group decodes one token per sequence"
    depth = ffn_norm.shape[0]
    n_a = s5_a_re.shape[0]
    assert depth == 2 and n_a == 1 and mla_w_dq.shape[0] == 1, "one S5 layer followed by one MLA layer"
    g, p = s5_a_re.shape[1], s5_a_re.shape[2]
    nh, hv = kv_w_uv.shape[1], kv_w_uv.shape[2]
    rope = kv_kpe_norm.shape[0]
    past_len = page_table.shape[1] * cache_ckv.shape[1]

    wg, wu, wd = ffn_w_gate.astype(BF16), ffn_w_up.astype(BF16), ffn_w_down.astype(BF16)

    def ffn(x, layer, idx):
        return _ffn(x, ffn_norm[layer, idx], wg[layer, idx], wu[layer, idx], wd[layer, idx])

    xp = x_prompt.reshape(bp * tp, d)
    xs = x_sample.reshape(bs, d)

    lam_re, lam_im, wb, wc = _s5_params(s5_a_re[0], s5_a_im[0], s5_log_dt[0], s5_b_re[0], s5_b_im[0],
                                        s5_c_re[0], s5_c_im[0])
    wv, wgl = s5_w_glu_v[0].astype(BF16), s5_w_glu_g[0].astype(BF16)
    xp = ffn(xp, 0, 0)
    xs = ffn(xs, 0, 0)
    xp, sre_p, sim_p = _s5_prompt(xp.reshape(bp, tp, d), mix_norm[0], lam_re, lam_im, wb, wc, s5_d[0], wv, wgl)
    xs, sre_s, sim_s = _s5_step(xs, mix_norm[0], state_s5_re[0].reshape(bs, g * p),
                                state_s5_im[0].reshape(bs, g * p), lam_re, lam_im, wb, wc, s5_d[0], wv, wgl)
    xp = ffn(xp.reshape(bp * tp, d), 0, 1)
    xs = ffn(xs, 0, 1)

    pos_p = jnp.arange(tp, dtype=jnp.int32)
    pos_s = jnp.full((bs,), past_len, dtype=jnp.int32)
    kv_args = (kv_in_norm, kv_w_dkv, kv_c_norm, kv_kpe_norm, kv_w_uk, kv_w_uv)
    ckv_p, kpe_p, k_p, vt_p = _kv(xp.reshape(bp, tp, d), pos_p, *kv_args, with_heads=True)
    ckv_s, kpe_s = _kv(xs.reshape(1, bs, d), pos_s, *kv_args, with_heads=False)

    xp = ffn(xp, 1, 0)
    xs = ffn(xs, 1, 0)
    q_args = (mix_norm[1], mla_w_dq[0], mla_q_norm[0], mla_w_uq[0], mla_qn_norm[0], mla_qr_norm[0])
    q_p = _queries(xp.reshape(bp, tp, d), pos_p, *q_args)
    ot = _attn_prompt(q_p.reshape(bp, tp, nh * LANES), k_p.reshape(bp, tp, nh * LANES), vt_p, nh=nh, hv=hv)
    xp = _oproj_t(ot, xp.reshape(bp, tp, d), mla_w_o[0]).reshape(bp * tp, d)

    q_s = _queries(xs.reshape(1, bs, d), pos_s, *q_args)
    qabs = _absorb(q_s, kv_w_uk, rope)
    o_lat = _attn_sample(qabs, cache_ckv, cache_kpe, page_table, ckv_s, kpe_s)
    xs = _oproj_lat(o_lat, xs, kv_w_uv, mla_w_o[0])

    xp = ffn(xp, 1, 1)
    xs = ffn(xs, 1, 1)

    return (xp.reshape(bp, tp, d), xs.reshape(bs, 1, d),
            sre_p.reshape(n_a, bp, g, p), sim_p.reshape(n_a, bp, g, p),
            sre_s.reshape(n_a, bs, g, p), sim_s.reshape(n_a, bs, g, p),
            ckv_p.reshape(bp, tp, -1), kpe_p.reshape(bp, tp, rope),
            ckv_s.reshape(bs, 1, -1), kpe_s.reshape(bs, 1, rope))
```

```python
import functools
import math

import jax
import jax.numpy as jnp
from jax import lax
from jax.experimental import pallas as pl
from jax.experimental.pallas import tpu as pltpu

F32 = jnp.float32
BF16 = jnp.bfloat16

EPS = 1e-6
FFN_RES = 0.5
NEG_BIG = -1e30
ROPE_BASE = 10000.0

GROUP_CH = 16
LANES = 128
SUBLANES = 8
S5_SLAB = 256
VT_PAD = 16
VMEM_LIMIT = 56 * 1024 * 1024


def _tile(n, pref):
    if n <= pref:
        return n
    for t in range(pref, 7, -1):
        if n % t == 0 and t % 8 == 0:
            return t
    return n


def _rms(x, gain):
    ms = jnp.mean(x * x, axis=-1, keepdims=True)
    return x * lax.rsqrt(ms + EPS) * gain


def _cparams(sem):
    return pltpu.CompilerParams(dimension_semantics=sem, vmem_limit_bytes=VMEM_LIMIT)


def _ffn_body(x_ref, g_ref, wg_ref, wu_ref, wd_ref, o_ref):
    x = x_ref[...]
    xn = _rms(x, g_ref[...]).astype(BF16)
    gate = jnp.dot(xn, wg_ref[...], preferred_element_type=F32)
    up = jnp.dot(xn, wu_ref[...], preferred_element_type=F32)
    hid = (gate * jax.nn.sigmoid(gate) * up).astype(BF16)
    o_ref[...] = x + FFN_RES * jnp.dot(hid, wd_ref[...], preferred_element_type=F32)


def _resident(shape, index_map):
    return pl.BlockSpec(shape, index_map, pipeline_mode=pl.Buffered(1))


def _ffn(x, gain, w_gate, w_up, w_down, *, tm_pref=512):
    m, d = x.shape
    ff = w_gate.shape[1]
    tm = _tile(m, tm_pref)
    return pl.pallas_call(
        _ffn_body,
        grid=(m // tm,),
        in_specs=[
            pl.BlockSpec((tm, d), lambda i: (i, 0)),
            _resident((1, d), lambda i: (0, 0)),
            _resident((d, ff), lambda i: (0, 0)),
            _resident((d, ff), lambda i: (0, 0)),
            _resident((ff, d), lambda i: (0, 0)),
        ],
        out_specs=pl.BlockSpec((tm, d), lambda i: (i, 0)),
        out_shape=jax.ShapeDtypeStruct((m, d), F32),
        compiler_params=_cparams(("parallel",)),
        name="ffn",
    )(x, gain.reshape(1, d), w_gate, w_up, w_down)


def _s5_params(a_re, a_im, log_dt, b_re, b_im, c_re, c_im):
    g, p = a_re.shape
    dt = jnp.exp(log_dt)[:, None]
    mag = jnp.exp(dt * a_re)
    lam_re, lam_im = mag * jnp.cos(dt * a_im), mag * jnp.sin(dt * a_im)
    den = a_re * a_re + a_im * a_im
    f_re = ((lam_re - 1.0) * a_re + lam_im * a_im) / den
    f_im = (lam_im * a_re - (lam_re - 1.0) * a_im) / den
    bb_re = f_re[..., None] * b_re - f_im[..., None] * b_im
    bb_im = f_re[..., None] * b_im + f_im[..., None] * b_re
    gs = S5_SLAB // GROUP_CH
    ns = g // gs
    eye = jnp.eye(gs, dtype=F32)
    nk = gs * p // LANES

    def pack_b(bb):
        bb = bb.reshape(ns, gs, p, GROUP_CH)
        w = jnp.einsum("sgpc,gh->sgchp", bb, eye)
        return w.reshape(ns, S5_SLAB, nk, LANES)

    wb = jnp.stack([pack_b(bb_re), pack_b(bb_im)], axis=3)
    wb = wb.reshape(ns, S5_SLAB, 2 * gs * p).astype(BF16)

    def pack_c(cc):
        cc = cc.reshape(ns, gs, GROUP_CH, p)
        w = jnp.einsum("sgcp,gh->shpgc", cc, eye)
        return w.reshape(ns, nk, LANES, S5_SLAB)

    wc = jnp.stack([pack_c(c_re), -pack_c(c_im)], axis=2)
    wc = wc.reshape(ns, 2 * gs * p, S5_SLAB).astype(BF16)
    return lam_re, lam_im, wb, wc


def _s5_prompt_body(x_ref, g_ref, wb_ref, wc_ref, lr_ref, li_ref, d_ref, wv_ref, wgl_ref,
                    o_ref, sre_ref, sim_ref, xr_ref, xi_ref, cr_ref, ci_ref, y_ref, *, tc, ns, nk):
    t_idx = pl.program_id(1)

    @pl.when(t_idx == 0)
    def _():
        cr_ref[...] = jnp.zeros_like(cr_ref)
        ci_ref[...] = jnp.zeros_like(ci_ref)

    x = x_ref[0]
    h = _rms(x, g_ref[...])
    hb = h.astype(BF16)
    for s in range(ns):
        us = hb[:, S5_SLAB * s:S5_SLAB * (s + 1)]
        for k in range(nk):
            xk = jnp.dot(us, wb_ref[s, :, 2 * LANES * k:2 * LANES * (k + 1)],
                         preferred_element_type=F32)
            xr_ref[s, pl.ds(k, tc, stride=nk), :] = xk[:, :LANES]
            xi_ref[s, pl.ds(k, tc, stride=nk), :] = xk[:, LANES:]

    lam = [(lr_ref[s], li_ref[s]) for s in range(ns)]

    def step(t, carry):
        row = pl.multiple_of(t * nk, nk)
        out = []
        for s in range(ns):
            sr, si = carry[s]
            lr, li = lam[s]
            nr = lr * sr - li * si + xr_ref[s, pl.ds(row, nk), :]
            ni = lr * si + li * sr + xi_ref[s, pl.ds(row, nk), :]
            xr_ref[s, pl.ds(row, nk), :] = nr
            xi_ref[s, pl.ds(row, nk), :] = ni
            out.append((nr, ni))
        return tuple(out)

    init = tuple((cr_ref[s], ci_ref[s]) for s in range(ns))
    fin = lax.fori_loop(0, tc, step, init, unroll=4)
    for s in range(ns):
        cr_ref[s] = fin[s][0]
        ci_ref[s] = fin[s][1]

    for s in range(ns):
        acc = None
        for k in range(nk):
            sk = jnp.concatenate([xr_ref[s, pl.ds(k, tc, stride=nk), :],
                                  xi_ref[s, pl.ds(k, tc, stride=nk), :]], axis=1).astype(BF16)
            part = jnp.dot(sk, wc_ref[s, 2 * LANES * k:2 * LANES * (k + 1), :],
                           preferred_element_type=F32)
            acc = part if acc is None else acc + part
        y_ref[:, S5_SLAB * s:S5_SLAB * (s + 1)] = acc

    y = jax.nn.gelu(y_ref[...] + d_ref[...] * h, approximate=True).astype(BF16)
    val = jnp.dot(y, wv_ref[...], preferred_element_type=F32)
    gate = jnp.dot(y, wgl_ref[...], preferred_element_type=F32)
    o_ref[0] = x + val * jax.nn.sigmoid(gate)

    @pl.when(t_idx == pl.num_programs(1) - 1)
    def _():
        sre_ref[0] = cr_ref[...].reshape(ns * nk, LANES)
        sim_ref[0] = ci_ref[...].reshape(ns * nk, LANES)


def _s5_prompt(x, gain, lam_re, lam_im, wb, wc, d_skip, w_v, w_g, *, tc_pref=512):
    b, t, d = x.shape
    ns = wb.shape[0]
    nk = wb.shape[2] // (2 * LANES)
    assert nk == SUBLANES, "one token's slab state must fill whole vregs"
    tc = _tile(t, tc_pref)
    lr = lam_re.reshape(ns, nk, LANES)
    li = lam_im.reshape(ns, nk, LANES)
    const3 = lambda i, j: (0, 0, 0)
    const2 = lambda i, j: (0, 0)
    out, s_re, s_im = pl.pallas_call(
        functools.partial(_s5_prompt_body, tc=tc, ns=ns, nk=nk),
        grid=(b, t // tc),
        in_specs=[
            pl.BlockSpec((1, tc, d), lambda i, j: (i, j, 0)),
            _resident((1, d), const2),
            _resident(wb.shape, const3),
            _resident(wc.shape, const3),
            _resident(lr.shape, const3),
            _resident(li.shape, const3),
            _resident((1, d), const2),
            _resident(w_v.shape, const2),
            _resident(w_g.shape, const2),
        ],
        out_specs=[
            pl.BlockSpec((1, tc, d), lambda i, j: (i, j, 0)),
            pl.BlockSpec((1, ns * nk, LANES), lambda i, j: (i, 0, 0)),
            pl.BlockSpec((1, ns * nk, LANES), lambda i, j: (i, 0, 0)),
        ],
        out_shape=[
            jax.ShapeDtypeStruct((b, t, d), F32),
            jax.ShapeDtypeStruct((b, ns * nk, LANES), F32),
            jax.ShapeDtypeStruct((b, ns * nk, LANES), F32),
        ],
        scratch_shapes=[
            pltpu.VMEM((ns, tc * nk, LANES), F32),
            pltpu.VMEM((ns, tc * nk, LANES), F32),
            pltpu.VMEM((ns, nk, LANES), F32),
            pltpu.VMEM((ns, nk, LANES), F32),
            pltpu.VMEM((tc, d), F32),
        ],
        compiler_params=_cparams(("parallel", "arbitrary")),
        name="s5_prompt",
    )(x, gain.reshape(1, d), wb, wc, lr, li, d_skip.reshape(1, d), w_v, w_g)
    return out, s_re, s_im


def _s5_step_body(x_ref, g_ref, s0r_ref, s0i_ref, wb_ref, wc_ref, lr_ref, li_ref, d_ref,
                  wv_ref, wgl_ref, o_ref, sre_ref, sim_ref, y_ref, *, ns, nk):
    x = x_ref[...]
    h = _rms(x, g_ref[...])
    hb = h.astype(BF16)
    for s in range(ns):
        xs = jnp.dot(hb[:, S5_SLAB * s:S5_SLAB * (s + 1)], wb_ref[s], preferred_element_type=F32)
        acc = None
        for k in range(nk):
            col = (s * nk + k) * LANES
            lr = lr_ref[:, col:col + LANES]
            li = li_ref[:, col:col + LANES]
            s0r = s0r_ref[:, col:col + LANES]
            s0i = s0i_ref[:, col:col + LANES]
            nr = lr * s0r - li * s0i + xs[:, 2 * LANES * k:2 * LANES * k + LANES]
            ni = lr * s0i + li * s0r + xs[:, 2 * LANES * k + LANES:2 * LANES * (k + 1)]
            sre_ref[:, col:col + LANES] = nr
            sim_ref[:, col:col + LANES] = ni
            part = (jnp.dot(nr.astype(BF16), wc_ref[s, 2 * LANES * k:2 * LANES * k + LANES, :],
                            preferred_element_type=F32)
                    + jnp.dot(ni.astype(BF16), wc_ref[s, 2 * LANES * k + LANES:2 * LANES * (k + 1), :],
                              preferred_element_type=F32))
            acc = part if acc is None else acc + part
        y_ref[:, S5_SLAB * s:S5_SLAB * (s + 1)] = acc
    y = jax.nn.gelu(y_ref[...] + d_ref[...] * h, approximate=True).astype(BF16)
    val = jnp.dot(y, wv_ref[...], preferred_element_type=F32)
    gate = jnp.dot(y, wgl_ref[...], preferred_element_type=F32)
    o_ref[...] = x + val * jax.nn.sigmoid(gate)


def _s5_step(x, gain, s0_re, s0_im, lam_re, lam_im, wb, wc, d_skip, w_v, w_g):
    m, d = x.shape
    ns = wb.shape[0]
    nk = wb.shape[2] // (2 * LANES)
    nstate = s0_re.shape[1]
    args = (x, gain.reshape(1, d), s0_re, s0_im, wb, wc, lam_re.reshape(1, nstate),
            lam_im.reshape(1, nstate), d_skip.reshape(1, d), w_v, w_g)
    return pl.pallas_call(
        functools.partial(_s5_step_body, ns=ns, nk=nk),
        out_shape=[
            jax.ShapeDtypeStruct((m, d), F32),
            jax.ShapeDtypeStruct((m, nstate), F32),
            jax.ShapeDtypeStruct((m, nstate), F32),
        ],
        scratch_shapes=[pltpu.VMEM((m, d), F32)],
        compiler_params=pltpu.CompilerParams(vmem_limit_bytes=VMEM_LIMIT),
        name="s5_step",
    )(*args)


def _rope_tables(pos, half):
    inv = ROPE_BASE ** (-jnp.arange(half, dtype=F32) / half)
    ang = pos.astype(F32)[:, None] * inv[None, :]
    return jnp.cos(ang), jnp.sin(ang)


def _lane_block(parts, n):
    out = jnp.zeros((n, LANES), F32)
    for off, arr in parts:
        out = out.at[:, off:off + arr.shape[1]].set(arr)
    return out


def _kv_body(x_ref, g_ref, w_ref, cn_ref, gpe_ref, cos_ref, sin_ref, *rest, rank, rope, with_heads, nh):
    if with_heads:
        wuk_ref, wuvt_ref, c_ref, kpe_ref, k_ref, vt_ref = rest
    else:
        c_ref, kpe_ref = rest
    hk = _rms(x_ref[...], g_ref[...]).astype(BF16)
    ck = jnp.dot(hk, w_ref[...], preferred_element_type=F32)
    c = _rms(ck[:, :rank], cn_ref[...])
    c_ref[...] = c

    def pe_block(i):
        pe = ck[:, rank + 2 * LANES * i:rank + 2 * LANES * i + LANES]
        rot = ck[:, rank + 2 * LANES * i + LANES:rank + 2 * LANES * (i + 1)]
        r = lax.rsqrt(jnp.sum(pe * pe, axis=-1, keepdims=True) * (1.0 / rope) + EPS)
        return (pe * gpe_ref[2 * i:2 * i + 1] * cos_ref[:, LANES * i:LANES * (i + 1)]
                + rot * gpe_ref[2 * i + 1:2 * i + 2] * sin_ref[:, LANES * i:LANES * (i + 1)]) * r

    kpe_ref[...] = pe_block(0)[:, :rope]
    if with_heads:
        cb = c.astype(BF16)
        pe_mid = pe_block(1)
        kn = jnp.dot(cb, wuk_ref[...], preferred_element_type=F32)
        for h in range(nh):
            k_ref[:, LANES * h:LANES * (h + 1)] = (kn[:, LANES * h:LANES * (h + 1)] + pe_mid).astype(BF16)
        vt = lax.dot_general(wuvt_ref[...], cb, (((1,), (1,)), ((), ())), preferred_element_type=F32)
        hv = vt.shape[0] // nh
        hvx = hv + VT_PAD
        ones_row = (lax.broadcasted_iota(jnp.int32, (VT_PAD, vt.shape[1]), 0) == 0).astype(BF16)
        for h in range(nh):
            vt_ref[hvx * h:hvx * h + hv, :] = vt[hv * h:hv * (h + 1)].astype(BF16)
            vt_ref[hvx * h + hv:hvx * (h + 1), :] = ones_row


def _kv(x, pos, kv_in_norm, w_dkv, c_norm, kpe_norm, w_uk, w_uv, *, with_heads, tm_pref=512):
    b, t, d = x.shape
    rank = c_norm.shape[0]
    rope = kpe_norm.shape[0]
    half = rope // 2
    nh, nope = w_uk.shape[1], w_uk.shape[2]
    hv = w_uv.shape[2]
    tm = _tile(t, tm_pref)
    nt = t // tm
    m = b * t
    w_pe = w_dkv[:, rank:]
    w_rot = jnp.concatenate([-w_pe[:, half:], w_pe[:, :half]], axis=1)
    zero = lambda n: jnp.zeros((d, n), F32)
    w_ext = jnp.concatenate([
        w_dkv[:, :rank],
        w_pe, zero(LANES - rope), w_rot, zero(LANES - rope),
        zero(nope), w_pe, zero(LANES - nope - rope), zero(nope), w_rot, zero(LANES - nope - rope),
    ], axis=1).astype(BF16)
    g_pe = kpe_norm[None, :]
    g_rot = jnp.concatenate([kpe_norm[half:], kpe_norm[:half]])[None, :]
    gpe = jnp.concatenate([_lane_block([(0, g_pe)], 1), _lane_block([(0, g_rot)], 1),
                           _lane_block([(nope, g_pe)], 1), _lane_block([(nope, g_rot)], 1)], axis=0)
    cos, sin = _rope_tables(pos, half)
    cos2, sin2 = jnp.concatenate([cos, cos], 1), jnp.concatenate([sin, sin], 1)
    cos_t = jnp.concatenate([_lane_block([(0, cos2)], t), _lane_block([(nope, cos2)], t)], axis=1)
    sin_t = jnp.concatenate([_lane_block([(0, sin2)], t), _lane_block([(nope, sin2)], t)], axis=1)

    in_specs = [
        pl.BlockSpec((tm, d), lambda i: (i, 0)),
        pl.BlockSpec((1, d), lambda i: (0, 0)),
        pl.BlockSpec(w_ext.shape, lambda i: (0, 0)),
        pl.BlockSpec((1, rank), lambda i: (0, 0)),
        pl.BlockSpec(gpe.shape, lambda i: (0, 0)),
        pl.BlockSpec((tm, 2 * LANES), lambda i: (i % nt, 0)),
        pl.BlockSpec((tm, 2 * LANES), lambda i: (i % nt, 0)),
    ]
    args = [x.reshape(m, d), kv_in_norm.reshape(1, d), w_ext, c_norm.reshape(1, rank), gpe, cos_t, sin_t]
    out_specs = [pl.BlockSpec((tm, rank), lambda i: (i, 0)), pl.BlockSpec((tm, rope), lambda i: (i, 0))]
    out_shape = [jax.ShapeDtypeStruct((m, rank), F32), jax.ShapeDtypeStruct((m, rope), F32)]
    if with_heads:
        wuk = jnp.concatenate([w_uk, jnp.zeros((rank, nh, LANES - nope), F32)], axis=2)
        wuk = wuk.reshape(rank, nh * LANES).astype(BF16)
        wuvt = jnp.transpose(w_uv, (1, 2, 0)).reshape(nh * hv, rank).astype(BF16)
        in_specs += [pl.BlockSpec(wuk.shape, lambda i: (0, 0)), pl.BlockSpec(wuvt.shape, lambda i: (0, 0))]
        args += [wuk, wuvt]
        out_specs += [pl.BlockSpec((tm, nh * LANES), lambda i: (i, 0)),
                      pl.BlockSpec((None, nh * (hv + VT_PAD), tm), lambda i: (i // nt, 0, i % nt))]
        out_shape += [jax.ShapeDtypeStruct((m, nh * LANES), BF16),
                      jax.ShapeDtypeStruct((b, nh * (hv + VT_PAD), t), BF16)]
    return pl.pallas_call(
        functools.partial(_kv_body, rank=rank, rope=rope, with_heads=with_heads, nh=nh),
        grid=(m // tm,),
        in_specs=in_specs,
        out_specs=out_specs,
        out_shape=out_shape,
        compiler_params=_cparams(("parallel",)),
        name="shared_kv",
    )(*args)


def _q_body(x_ref, g_ref, wdq_ref, qn_ref, w1_ref, w2_ref, gain_ref, cos_ref, sin_ref, q_ref,
            *, nh, nope, rope):
    hb = _rms(x_ref[...], g_ref[...]).astype(BF16)
    cq = jnp.dot(hb, wdq_ref[...], preferred_element_type=F32)
    cqb = _rms(cq, qn_ref[...]).astype(BF16)
    qm = jnp.dot(cqb, w1_ref[...], preferred_element_type=F32)
    qr = jnp.dot(cqb, w2_ref[...], preferred_element_type=F32)
    lane = lax.broadcasted_iota(jnp.int32, (1, LANES), 1)
    is_nope = lane < nope
    is_pe = jnp.logical_and(lane >= nope, lane < nope + rope)
    gain, gain_rot = gain_ref[0:1], gain_ref[1:2]
    cos, sin = cos_ref[...], sin_ref[...]
    for h in range(nh):
        blk = qm[:, LANES * h:LANES * (h + 1)]
        rot = qr[:, LANES * h:LANES * (h + 1)]
        sq = blk * blk
        s_n = jnp.sum(jnp.where(is_nope, sq, 0.0), axis=-1, keepdims=True)
        s_p = jnp.sum(jnp.where(is_pe, sq, 0.0), axis=-1, keepdims=True)
        r = jnp.where(is_nope, lax.rsqrt(s_n * (1.0 / nope) + EPS), lax.rsqrt(s_p * (1.0 / rope) + EPS))
        q_ref[:, LANES * h:LANES * (h + 1)] = ((blk * gain * cos + rot * gain_rot * sin) * r).astype(BF16)


def _queries(x, pos, mix_norm, w_dq, q_norm, w_uq, qn_norm, qr_norm, *, tm_pref=512):
    b, t, d = x.shape
    qrank = w_dq.shape[1]
    nh = w_uq.shape[1]
    nope, rope = qn_norm.shape[0], qr_norm.shape[0]
    half = rope // 2
    scale = (nope + rope) ** -0.5 * math.log2(math.e)
    tm = _tile(t, tm_pref)
    nt = t // tm
    m = b * t
    w_n, w_p = w_uq[:, :, :nope], w_uq[:, :, nope:]
    pad = jnp.zeros((qrank, nh, LANES - nope - rope), F32)
    w1 = jnp.concatenate([w_n, w_p, pad], axis=2).reshape(qrank, nh * LANES).astype(BF16)
    w_rot = jnp.concatenate([-w_p[:, :, half:], w_p[:, :, :half]], axis=2)
    w2 = jnp.concatenate([jnp.zeros_like(w_n), w_rot, pad], axis=2).reshape(qrank, nh * LANES).astype(BF16)
    g_rot = jnp.concatenate([qr_norm[half:], qr_norm[:half]])
    gain = jnp.concatenate([
        _lane_block([(0, qn_norm[None, :] * scale), (nope, qr_norm[None, :] * scale)], 1),
        _lane_block([(nope, g_rot[None, :] * scale)], 1)], axis=0)
    cos, sin = _rope_tables(pos, half)
    cos_t = _lane_block([(0, jnp.ones((t, nope), F32)), (nope, jnp.concatenate([cos, cos], 1))], t)
    sin_t = _lane_block([(nope, jnp.concatenate([sin, sin], 1))], t)
    return pl.pallas_call(
        functools.partial(_q_body, nh=nh, nope=nope, rope=rope),
        grid=(m // tm,),
        in_specs=[
            pl.BlockSpec((tm, d), lambda i: (i, 0)),
            pl.BlockSpec((1, d), lambda i: (0, 0)),
            pl.BlockSpec(w_dq.shape, lambda i: (0, 0)),
            pl.BlockSpec((1, qrank), lambda i: (0, 0)),
            pl.BlockSpec(w1.shape, lambda i: (0, 0)),
            pl.BlockSpec(w2.shape, lambda i: (0, 0)),
            pl.BlockSpec(gain.shape, lambda i: (0, 0)),
            pl.BlockSpec((tm, LANES), lambda i: (i % nt, 0)),
            pl.BlockSpec((tm, LANES), lambda i: (i % nt, 0)),
        ],
        out_specs=pl.BlockSpec((tm, nh * LANES), lambda i: (i, 0)),
        out_shape=jax.ShapeDtypeStruct((m, nh * LANES), BF16),
        compiler_params=_cparams(("parallel",)),
        name="mla_queries",
    )(x.reshape(m, d), mix_norm.reshape(1, d), w_dq.astype(BF16), q_norm.reshape(1, qrank),
      w1, w2, gain, cos_t, sin_t)


def _attn_prompt_body(q_ref, k_ref, vt_ref, o_ref, *, tq, hp, hv):
    qi = pl.program_id(2)
    hvx = hv + VT_PAD
    qs = [q_ref[:, LANES * a:LANES * (a + 1)] for a in range(hp)]

    def block(ki, carry, masked):
        off = pl.multiple_of(ki * tq, tq)
        ss = []
        for a in range(hp):
            k = k_ref[pl.ds(off, tq), LANES * a:LANES * (a + 1)]
            ss.append(lax.dot_general(k, qs[a], (((1,), (1,)), ((), ())),
                                      preferred_element_type=F32))
        ps, stats = [], []
        for a in range(hp):
            m_i = carry[a][0]
            s = ss[a]
            if masked:
                kpos = lax.broadcasted_iota(jnp.int32, (tq, tq), 0)
                qpos = lax.broadcasted_iota(jnp.int32, (tq, tq), 1)
                s = jnp.where(kpos <= qpos, s, NEG_BIG)
            m_new = jnp.maximum(m_i, jnp.max(s, axis=0, keepdims=True))
            stats.append((m_new, jnp.exp2(m_i - m_new)))
            ps.append(jnp.exp2(s - m_new).astype(BF16))
        out = []
        for a in range(hp):
            m_new, alpha = stats[a]
            pv = jnp.dot(vt_ref[hvx * a:hvx * (a + 1), pl.ds(off, tq)], ps[a], preferred_element_type=F32)
            out.append((m_new, alpha * carry[a][1] + pv))
        return tuple(out)

    init = tuple((jnp.full((1, tq), NEG_BIG, F32), jnp.zeros((hvx, tq), F32)) for _ in range(hp))
    carry = lax.fori_loop(0, qi, lambda ki, c: block(ki, c, False), init)
    fin = block(qi, carry, True)
    for a in range(hp):
        acc = fin[a][1]
        o_ref[hv * a:hv * (a + 1), :] = (acc[:hv] / acc[hv:hv + 1]).astype(BF16)


def _attn_prompt(q, k, vt, *, nh, hv, tq_pref=512, hp=4):
    b, t, _ = q.shape
    tq = _tile(t, tq_pref)
    assert nh % hp == 0
    return pl.pallas_call(
        functools.partial(_attn_prompt_body, tq=tq, hp=hp, hv=hv),
        grid=(b, nh // hp, t // tq),
        in_specs=[
            pl.BlockSpec((None, tq, hp * LANES), lambda i, h, j: (i, j, h)),
            pl.BlockSpec((None, t, hp * LANES), lambda i, h, j: (i, 0, h)),
            pl.BlockSpec((None, hp * (hv + VT_PAD), t), lambda i, h, j: (i, h, 0)),
        ],
        out_specs=pl.BlockSpec((None, hp * hv, tq), lambda i, h, j: (i, h, j)),
        out_shape=jax.ShapeDtypeStruct((b, nh * hv, t), BF16),
        compiler_params=_cparams(("parallel", "parallel", "arbitrary")),
        name="attn_prompt",
    )(q, k, vt)


def _oproj_t_body(ot_ref, x_ref, wo_ref, o_ref):
    m = lax.dot_general(ot_ref[...], wo_ref[...], (((0,), (0,)), ((), ())), preferred_element_type=F32)
    o_ref[...] = x_ref[...] + m


def _oproj_t(ot, x, w_o, *, tm_pref=512):
    b, t, d = x.shape
    e = ot.shape[1]
    tm = _tile(t, tm_pref)
    return pl.pallas_call(
        _oproj_t_body,
        grid=(b, t // tm),
        in_specs=[
            pl.BlockSpec((None, e, tm), lambda i, j: (i, 0, j)),
            pl.BlockSpec((None, tm, d), lambda i, j: (i, j, 0)),
            pl.BlockSpec((e, d), lambda i, j: (0, 0)),
        ],
        out_specs=pl.BlockSpec((None, tm, d), lambda i, j: (i, j, 0)),
        out_shape=jax.ShapeDtypeStruct((b, t, d), F32),
        compiler_params=_cparams(("parallel", "parallel")),
        name="attn_out_prompt",
    )(ot, x, w_o.astype(BF16))


def _absorb_body(q_ref, w_ref, o_ref):
    o_ref[...] = jnp.dot(q_ref[...], w_ref[...], preferred_element_type=F32).astype(BF16)


def _absorb(q, w_uk, rope):
    m = q.shape[0]
    rank, nh, nope = w_uk.shape
    width = rank + LANES
    w = jnp.zeros((nh, LANES, width), F32)
    w = w.at[:, :nope, :rank].set(jnp.transpose(w_uk, (1, 2, 0)))
    w = w.at[:, nope:nope + rope, rank:rank + rope].set(jnp.broadcast_to(jnp.eye(rope, dtype=F32), (nh, rope, rope)))
    out = pl.pallas_call(
        _absorb_body,
        grid=(nh,),
        in_specs=[pl.BlockSpec((m, LANES), lambda h: (0, h)),
                  pl.BlockSpec((None, LANES, width), lambda h: (h, 0, 0))],
        out_specs=pl.BlockSpec((m, width), lambda h: (0, h)),
        out_shape=jax.ShapeDtypeStruct((m, nh * width), BF16),
        compiler_params=_cparams(("parallel",)),
        name="absorb_queries",
    )(q, w.astype(BF16))
    return out.reshape(m, nh, width)


N_SLOTS = 3
SAMPLE_SPLIT = 4


def _attn_sample_body(pt_ref, q_ref, cn_ref, pn_ref, ckv_hbm, kpe_hbm, o_ref,
                      cbuf, pbuf, sem, m_ref, l_ref, acc_ref, *, npg, page, rank, rope):
    j = pl.program_id(1)
    nchunk = pl.num_programs(1)
    total = pl.num_programs(0) * nchunk
    g = pl.program_id(0) * nchunk + j

    def page_copies(chunk, slot):
        out = []
        for i in range(npg):
            pid = pt_ref[chunk * npg + i]
            out.append(pltpu.make_async_copy(ckv_hbm.at[pid], cbuf.at[slot, pl.ds(i * page, page)],
                                             sem.at[0, slot]))
            out.append(pltpu.make_async_copy(kpe_hbm.at[pid], pbuf.at[slot, :, pl.ds(i * page, page)],
                                             sem.at[1, slot]))
        return out

    @pl.when(g == 0)
    def _():
        for c in range(N_SLOTS - 1):
            for cp in page_copies(jnp.minimum(c, total - 1), c):
                cp.start()

    @pl.when(j == 0)
    def _():
        m_ref[...] = jnp.full_like(m_ref, NEG_BIG)
        l_ref[...] = jnp.zeros_like(l_ref)
        acc_ref[...] = jnp.zeros_like(acc_ref)

    slot = g % N_SLOTS
    for cp in page_copies(g, slot):
        cp.wait()
    q_lat = q_ref[0, :, :rank]
    q_pe = q_ref[0, :, rank:rank + rope]
    part = npg * page // SAMPLE_SPLIT
    cs = [cbuf[slot, pl.ds(part * i, part), :].astype(BF16) for i in range(SAMPLE_SPLIT)]
    s = jnp.concatenate(
        [lax.dot_general(q_lat, cs[i], (((1,), (1,)), ((), ())), preferred_element_type=F32)
         for i in range(SAMPLE_SPLIT)], axis=1)
    s = s + jnp.dot(q_pe, pbuf[slot].astype(BF16), preferred_element_type=F32)
    m_new = jnp.maximum(m_ref[...], jnp.max(s, axis=1, keepdims=True))
    alpha = jnp.exp2(m_ref[...] - m_new)
    p = jnp.exp2(s - m_new)
    l_ref[...] = alpha * l_ref[...] + jnp.sum(p, axis=1, keepdims=True)
    pb = p.astype(BF16)
    pvs = [jnp.dot(pb[:, part * i:part * (i + 1)], cs[i], preferred_element_type=F32)
           for i in range(SAMPLE_SPLIT)]
    acc_ref[...] = alpha * acc_ref[...] + functools.reduce(lambda a, b: a + b, pvs)
    m_ref[...] = m_new

    nxt = g + N_SLOTS - 1
    for cp in page_copies(jnp.minimum(nxt, total - 1), nxt % N_SLOTS):
        cp.start()

    @pl.when(g == total - 1)
    def _():
        for ahead in range(1, N_SLOTS):
            for cp in page_copies(total - 1, (g + ahead) % N_SLOTS):
                cp.wait()

    @pl.when(j == nchunk - 1)
    def _():
        c_new = cn_ref[0].astype(BF16).astype(F32)
        p_new = pn_ref[0].astype(BF16).astype(F32)
        s_new = (jnp.sum(q_lat.astype(F32) * c_new, axis=1, keepdims=True)
                 + jnp.sum(q_pe.astype(F32) * p_new, axis=1, keepdims=True))
        m_fin = jnp.maximum(m_ref[...], s_new)
        a_old = jnp.exp2(m_ref[...] - m_fin)
        p_n = jnp.exp2(s_new - m_fin)
        l_fin = a_old * l_ref[...] + p_n
        acc = a_old * acc_ref[...] + p_n.astype(BF16).astype(F32) * c_new
        o_ref[0] = acc / l_fin


def _attn_sample(qabs, cache_ckv, cache_kpe, page_table, c_new, kpe_new, *, npg_pref=32):
    m, nh, width = qabs.shape
    _, page, rank = cache_ckv.shape
    rope = cache_kpe.shape[2]
    n_pages = page_table.shape[1]
    npg = npg_pref if n_pages % npg_pref == 0 else n_pages
    nchunk = n_pages // npg
    kpe_t = jnp.swapaxes(cache_kpe, 1, 2)
    grid_spec = pltpu.PrefetchScalarGridSpec(
        num_scalar_prefetch=1,
        grid=(m, nchunk),
        in_specs=[
            pl.BlockSpec((1, nh, width), lambda b, j, pt: (b, 0, 0)),
            pl.BlockSpec((1, 1, rank), lambda b, j, pt: (b, 0, 0)),
            pl.BlockSpec((1, 1, rope), lambda b, j, pt: (b, 0, 0)),
            pl.BlockSpec(memory_space=pl.ANY),
            pl.BlockSpec(memory_space=pl.ANY),
        ],
        out_specs=pl.BlockSpec((1, nh, rank), lambda b, j, pt: (b, 0, 0)),
        scratch_shapes=[
            pltpu.VMEM((N_SLOTS, npg * page, rank), F32),
            pltpu.VMEM((N_SLOTS, rope, npg * page), F32),
            pltpu.SemaphoreType.DMA((2, N_SLOTS)),
            pltpu.VMEM((nh, 1), F32), pltpu.VMEM((nh, 1), F32), pltpu.VMEM((nh, rank), F32),
        ],
    )
    return pl.pallas_call(
        functools.partial(_attn_sample_body, npg=npg, page=page, rank=rank, rope=rope),
        grid_spec=grid_spec,
        out_shape=jax.ShapeDtypeStruct((m, nh, rank), F32),
        compiler_params=_cparams(("arbitrary", "arbitrary")),
        name="attn_sample",
    )(page_table.reshape(-1), qabs, c_new.reshape(m, 1, rank), kpe_new.reshape(m, 1, rope),
      cache_ckv, kpe_t)


def _oproj_lat_body(ol_ref, x_ref, wuv_ref, wo_ref, o_ref, *, nh, rank, hv):
    acc = x_ref[...]
    for h in range(nh):
        oh = jnp.dot(ol_ref[:, rank * h:rank * (h + 1)].astype(BF16), wuv_ref[h], preferred_element_type=F32)
        acc = acc + jnp.dot(oh.astype(BF16), wo_ref[hv * h:hv * (h + 1), :], preferred_element_type=F32)
    o_ref[...] = acc


def _oproj_lat(o_lat, x, w_uv, w_o):
    m, nh, rank = o_lat.shape
    hv = w_uv.shape[2]
    return pl.pallas_call(
        functools.partial(_oproj_lat_body, nh=nh, rank=rank, hv=hv),
        out_shape=jax.ShapeDtypeStruct(x.shape, F32),
        compiler_params=pltpu.CompilerParams(vmem_limit_bytes=VMEM_LIMIT),
        name="attn_out_sample",
    )(o_lat.reshape(m, nh * rank), x, jnp.transpose(w_uv, (1, 0, 2)).astype(BF16), w_o.astype(BF16))


def kernel(x_prompt, x_sample, state_s5_re, state_s5_im, cache_ckv, cache_kpe, page_table, ffn_norm, ffn_w_gate, ffn_w_up, ffn_w_down, mix_norm, s5_a_re, s5_a_im, s5_log_dt, s5_b_re, s5_b_im, s5_c_re, s5_c_im, s5_d, s5_w_glu_v, s5_w_glu_g, mla_w_dq, mla_q_norm, mla_w_uq, mla_qn_norm, mla_qr_norm, mla_w_o, kv_in_norm, kv_w_dkv, kv_c_norm, kv_kpe_norm, kv_w_uk, kv_w_uv):
    bp, tp, d = x_prompt.shape
    bs, ts, _ = x_sample.shape
    assert ts == 1, "the sample group decodes one token per sequence"
    depth = ffn_norm.shape[0]
    n_a = s5_a_re.shape[0]
    assert depth == 2 and n_a == 1 and mla_w_dq.shape[0] == 1, "one S5 layer followed by one MLA layer"
    g, p = s5_a_re.shape[1], s5_a_re.shape[2]
    nh, hv = kv_w_uv.shape[1], kv_w_uv.shape[2]
    rope = kv_kpe_norm.shape[0]
    past_len = page_table.shape[1] * cache_ckv.shape[1]

    wg, wu, wd = ffn_w_gate.astype(BF16), ffn_w_up.astype(BF16), ffn_w_down.astype(BF16)

    def ffn(x, layer, idx):
        return _ffn(x, ffn_norm[layer, idx], wg[layer, idx], wu[layer, idx], wd[layer, idx])

    xp = x_prompt.reshape(bp * tp, d)
    xs = x_sample.reshape(bs, d)

    lam_re, lam_im, wb, wc = _s5_params(s5_a_re[0], s5_a_im[0], s5_log_dt[0], s5_b_re[0], s5_b_im[0],
                                        s5_c_re[0], s5_c_im[0])
    wv, wgl = s5_w_glu_v[0].astype(BF16), s5_w_glu_g[0].astype(BF16)
    xp = ffn(xp, 0, 0)
    xs = ffn(xs, 0, 0)
    xp, sre_p, sim_p = _s5_prompt(xp.reshape(bp, tp, d), mix_norm[0], lam_re, lam_im, wb, wc, s5_d[0], wv, wgl)
    xs, sre_s, sim_s = _s5_step(xs, mix_norm[0], state_s5_re[0].reshape(bs, g * p),
                                state_s5_im[0].reshape(bs, g * p), lam_re, lam_im, wb, wc, s5_d[0], wv, wgl)
    xp = ffn(xp.reshape(bp * tp, d), 0, 1)
    xs = ffn(xs, 0, 1)

    pos_p = jnp.arange(tp, dtype=jnp.int32)
    pos_s = jnp.full((bs,), past_len, dtype=jnp.int32)
    kv_args = (kv_in_norm, kv_w_dkv, kv_c_norm, kv_kpe_norm, kv_w_uk, kv_w_uv)
    ckv_p, kpe_p, k_p, vt_p = _kv(xp.reshape(bp, tp, d), pos_p, *kv_args, with_heads=True)
    ckv_s, kpe_s = _kv(xs.reshape(1, bs, d), pos_s, *kv_args, with_heads=False)

    xp = ffn(xp, 1, 0)
    xs = ffn(xs, 1, 0)
    q_args = (mix_norm[1], mla_w_dq[0], mla_q_norm[0], mla_w_uq[0], mla_qn_norm[0], mla_qr_norm[0])
    q_p = _queries(xp.reshape(bp, tp, d), pos_p, *q_args)
    ot = _attn_prompt(q_p.reshape(bp, tp, nh * LANES), k_p.reshape(bp, tp, nh * LANES), vt_p, nh=nh, hv=hv)
    xp = _oproj_t(ot, xp.reshape(bp, tp, d), mla_w_o[0]).reshape(bp * tp, d)

    q_s = _queries(xs.reshape(1, bs, d), pos_s, *q_args)
    qabs = _absorb(q_s, kv_w_uk, rope)
    o_lat = _attn_sample(qabs, cache_ckv, cache_kpe, page_table, ckv_s, kpe_s)
    xs = _oproj_lat(o_lat, xs, kv_w_uv, mla_w_o[0])

    xp = ffn(xp, 1, 1)
    xs = ffn(xs, 1, 1)

    return (xp.reshape(bp, tp, d), xs.reshape(bs, 1, d),
            sre_p.reshape(n_a, bp, g, p), sim_p.reshape(n_a, bp, g, p),
            sre_s.reshape(n_a, bs, g, p), sim_s.reshape(n_a, bs, g, p),
            ckv_p.reshape(bp, tp, -1), kpe_p.reshape(bp, tp, rope),
            ckv_s.reshape(bs, 1, -1), kpe_s.reshape(bs, 1, rope))
```

```python
import functools
import math

import jax
import jax.numpy as jnp
from jax import lax
from jax.experimental import pallas as pl
from jax.experimental.pallas import tpu as pltpu

F32 = jnp.float32
BF16 = jnp.bfloat16

EPS = 1e-6
FFN_RES = 0.5
NEG_BIG = -1e30
ROPE_BASE = 10000.0

GROUP_CH = 16
LANES = 128
SUBLANES = 8
S5_SLAB = 256
S5_PITCH = 12
VT_PAD = 16
VMEM_LIMIT = 56 * 1024 * 1024


def _tile(n, pref):
    if n <= pref:
        return n
    for t in range(pref, 7, -1):
        if n % t == 0 and t % 8 == 0:
            return t
    return n


def _rms(x, gain):
    ms = jnp.mean(x * x, axis=-1, keepdims=True)
    return x * lax.rsqrt(ms + EPS) * gain


def _cparams(sem):
    return pltpu.CompilerParams(dimension_semantics=sem, vmem_limit_bytes=VMEM_LIMIT)


def _ffn_body(x_ref, g_ref, wg_ref, wu_ref, wd_ref, o_ref):
    x = x_ref[...]
    xn = _rms(x, g_ref[...]).astype(BF16)
    gate = jnp.dot(xn, wg_ref[...], preferred_element_type=F32)
    up = jnp.dot(xn, wu_ref[...], preferred_element_type=F32)
    hid = (gate * jax.nn.sigmoid(gate) * up).astype(BF16)
    o_ref[...] = x + FFN_RES * jnp.dot(hid, wd_ref[...], preferred_element_type=F32)


def _resident(shape, index_map):
    return pl.BlockSpec(shape, index_map, pipeline_mode=pl.Buffered(1))


def _ffn(x, gain, w_gate, w_up, w_down, layer, idx, *, tm_pref=512):
    m, d = x.shape
    ff = w_gate.shape[3]
    tm = _tile(m, tm_pref)
    pick = lambda i: (layer, idx, 0, 0)
    return pl.pallas_call(
        _ffn_body,
        grid=(m // tm,),
        in_specs=[
            pl.BlockSpec((tm, d), lambda i: (i, 0)),
            _resident((1, d), lambda i: (0, 0)),
            _resident((None, None, d, ff), pick),
            _resident((None, None, d, ff), pick),
            _resident((None, None, ff, d), pick),
        ],
        out_specs=pl.BlockSpec((tm, d), lambda i: (i, 0)),
        out_shape=jax.ShapeDtypeStruct((m, d), F32),
        compiler_params=_cparams(("parallel",)),
        name="ffn",
    )(x, gain.reshape(1, d), w_gate, w_up, w_down)


def _s5_params(a_re, a_im, log_dt, b_re, b_im, c_re, c_im):
    g, p = a_re.shape
    dt = jnp.exp(log_dt)[:, None]
    mag = jnp.exp(dt * a_re)
    lam_re, lam_im = mag * jnp.cos(dt * a_im), mag * jnp.sin(dt * a_im)
    den = a_re * a_re + a_im * a_im
    f_re = ((lam_re - 1.0) * a_re + lam_im * a_im) / den
    f_im = (lam_im * a_re - (lam_re - 1.0) * a_im) / den
    bb_re = f_re[..., None] * b_re - f_im[..., None] * b_im
    bb_im = f_re[..., None] * b_im + f_im[..., None] * b_re
    gs = S5_SLAB // GROUP_CH
    ns = g // gs
    eye = jnp.eye(gs, dtype=F32)
    nk = gs * p // LANES

    def pack_b(bb):
        bb = bb.reshape(ns, gs, p, GROUP_CH)
        w = jnp.einsum("sgpc,gh->sgchp", bb, eye)
        return w.reshape(ns, S5_SLAB, nk, LANES)

    wb = jnp.stack([pack_b(bb_re), pack_b(bb_im)], axis=3)
    wb = wb.reshape(ns, S5_SLAB, 2 * gs * p).astype(BF16)

    def pack_c(cc):
        cc = cc.reshape(ns, gs, GROUP_CH, p)
        w = jnp.einsum("sgcp,gh->shpgc", cc, eye)
        return w.reshape(ns, nk, LANES, S5_SLAB)

    wc = jnp.stack([pack_c(c_re), -pack_c(c_im)], axis=2)
    wc = wc.reshape(ns, 2 * gs * p, S5_SLAB).astype(BF16)
    return lam_re, lam_im, wb, wc


def _s5_prompt_body(x_ref, g_ref, wb_ref, wc_ref, lr_ref, li_ref, d_ref, wv_ref, wgl_ref,
                    o_ref, sre_ref, sim_ref, xr_ref, xi_ref, cr_ref, ci_ref, y_ref, *, tc, ns, nk):
    t_idx = pl.program_id(1)

    @pl.when(t_idx == 0)
    def _():
        cr_ref[...] = jnp.zeros_like(cr_ref)
        ci_ref[...] = jnp.zeros_like(ci_ref)

    x = x_ref[0]
    h = _rms(x, g_ref[...])
    hb = h.astype(BF16)
    for s in range(ns):
        us = hb[:, S5_SLAB * s:S5_SLAB * (s + 1)]
        for k in range(nk):
            xk = jnp.dot(us, wb_ref[s, :, 2 * LANES * k:2 * LANES * (k + 1)],
                         preferred_element_type=F32)
            xr_ref[s, pl.ds(k, tc, stride=S5_PITCH), :] = xk[:, :LANES]
            xi_ref[s, pl.ds(k, tc, stride=S5_PITCH), :] = xk[:, LANES:]

    lam = [(lr_ref[s], li_ref[s]) for s in range(ns)]

    def step(t, carry):
        row = pl.multiple_of(t * S5_PITCH, math.gcd(S5_PITCH, SUBLANES))
        out = []
        for s in range(ns):
            sr, si = carry[s]
            lr, li = lam[s]
            nr = lr * sr - li * si + xr_ref[s, pl.ds(row, nk), :]
            ni = lr * si + li * sr + xi_ref[s, pl.ds(row, nk), :]
            xr_ref[s, pl.ds(row, nk), :] = nr
            xi_ref[s, pl.ds(row, nk), :] = ni
            out.append((nr, ni))
        return tuple(out)

    init = tuple((cr_ref[s], ci_ref[s]) for s in range(ns))
    fin = lax.fori_loop(0, tc, step, init, unroll=4)
    for s in range(ns):
        cr_ref[s] = fin[s][0]
        ci_ref[s] = fin[s][1]

    for s in range(ns):
        acc = None
        for k in range(nk):
            sk = jnp.concatenate([xr_ref[s, pl.ds(k, tc, stride=S5_PITCH), :],
                                  xi_ref[s, pl.ds(k, tc, stride=S5_PITCH), :]], axis=1).astype(BF16)
            part = jnp.dot(sk, wc_ref[s, 2 * LANES * k:2 * LANES * (k + 1), :],
                           preferred_element_type=F32)
            acc = part if acc is None else acc + part
        y_ref[:, S5_SLAB * s:S5_SLAB * (s + 1)] = acc

    y = jax.nn.gelu(y_ref[...] + d_ref[...] * h, approximate=True).astype(BF16)
    val = jnp.dot(y, wv_ref[...], preferred_element_type=F32)
    gate = jnp.dot(y, wgl_ref[...], preferred_element_type=F32)
    o_ref[0] = x + val * jax.nn.sigmoid(gate)

    @pl.when(t_idx == pl.num_programs(1) - 1)
    def _():
        sre_ref[0] = cr_ref[...].reshape(ns * nk, LANES)
        sim_ref[0] = ci_ref[...].reshape(ns * nk, LANES)


def _s5_prompt(x, gain, lam_re, lam_im, wb, wc, d_skip, w_v, w_g, *, tc_pref=512):
    b, t, d = x.shape
    ns = wb.shape[0]
    nk = wb.shape[2] // (2 * LANES)
    assert nk == SUBLANES, "one token's slab state must fill whole vregs"
    tc = _tile(t, tc_pref)
    lr = lam_re.reshape(ns, nk, LANES)
    li = lam_im.reshape(ns, nk, LANES)
    const3 = lambda i, j: (0, 0, 0)
    const2 = lambda i, j: (0, 0)
    out, s_re, s_im = pl.pallas_call(
        functools.partial(_s5_prompt_body, tc=tc, ns=ns, nk=nk),
        grid=(b, t // tc),
        in_specs=[
            pl.BlockSpec((1, tc, d), lambda i, j: (i, j, 0)),
            _resident((1, d), const2),
            _resident(wb.shape, const3),
            _resident(wc.shape, const3),
            _resident(lr.shape, const3),
            _resident(li.shape, const3),
            _resident((1, d), const2),
            _resident(w_v.shape, const2),
            _resident(w_g.shape, const2),
        ],
        out_specs=[
            pl.BlockSpec((1, tc, d), lambda i, j: (i, j, 0)),
            pl.BlockSpec((1, ns * nk, LANES), lambda i, j: (i, 0, 0)),
            pl.BlockSpec((1, ns * nk, LANES), lambda i, j: (i, 0, 0)),
        ],
        out_shape=[
            jax.ShapeDtypeStruct((b, t, d), F32),
            jax.ShapeDtypeStruct((b, ns * nk, LANES), F32),
            jax.ShapeDtypeStruct((b, ns * nk, LANES), F32),
        ],
        scratch_shapes=[
            pltpu.VMEM((ns, tc * S5_PITCH, LANES), F32),
            pltpu.VMEM((ns, tc * S5_PITCH, LANES), F32),
            pltpu.VMEM((ns, nk, LANES), F32),
            pltpu.VMEM((ns, nk, LANES), F32),
            pltpu.VMEM((tc, d), F32),
        ],
        compiler_params=_cparams(("parallel", "arbitrary")),
        name="s5_prompt",
    )(x, gain.reshape(1, d), wb, wc, lr, li, d_skip.reshape(1, d), w_v, w_g)
    return out, s_re, s_im


def _s5_step_body(x_ref, g_ref, s0r_ref, s0i_ref, wb_ref, wc_ref, lr_ref, li_ref, d_ref,
                  wv_ref, wgl_ref, o_ref, sre_ref, sim_ref, y_ref, *, ns, nk):
    x = x_ref[...]
    h = _rms(x, g_ref[...])
    hb = h.astype(BF16)
    for s in range(ns):
        xs = jnp.dot(hb[:, S5_SLAB * s:S5_SLAB * (s + 1)], wb_ref[s], preferred_element_type=F32)
        acc = None
        for k in range(nk):
            col = (s * nk + k) * LANES
            lr = lr_ref[:, col:col + LANES]
            li = li_ref[:, col:col + LANES]
            s0r = s0r_ref[:, col:col + LANES]
            s0i = s0i_ref[:, col:col + LANES]
            nr = lr * s0r - li * s0i + xs[:, 2 * LANES * k:2 * LANES * k + LANES]
            ni = lr * s0i + li * s0r + xs[:, 2 * LANES * k + LANES:2 * LANES * (k + 1)]
            sre_ref[:, col:col + LANES] = nr
            sim_ref[:, col:col + LANES] = ni
            part = (jnp.dot(nr.astype(BF16), wc_ref[s, 2 * LANES * k:2 * LANES * k + LANES, :],
                            preferred_element_type=F32)
                    + jnp.dot(ni.astype(BF16), wc_ref[s, 2 * LANES * k + LANES:2 * LANES * (k + 1), :],
                              preferred_element_type=F32))
            acc = part if acc is None else acc + part
        y_ref[:, S5_SLAB * s:S5_SLAB * (s + 1)] = acc
    y = jax.nn.gelu(y_ref[...] + d_ref[...] * h, approximate=True).astype(BF16)
    val = jnp.dot(y, wv_ref[...], preferred_element_type=F32)
    gate = jnp.dot(y, wgl_ref[...], preferred_element_type=F32)
    o_ref[...] = x + val * jax.nn.sigmoid(gate)


def _s5_step(x, gain, s0_re, s0_im, lam_re, lam_im, wb, wc, d_skip, w_v, w_g):
    m, d = x.shape
    ns = wb.shape[0]
    nk = wb.shape[2] // (2 * LANES)
    nstate = s0_re.shape[1]
    args = (x, gain.reshape(1, d), s0_re, s0_im, wb, wc, lam_re.reshape(1, nstate),
            lam_im.reshape(1, nstate), d_skip.reshape(1, d), w_v, w_g)
    return pl.pallas_call(
        functools.partial(_s5_step_body, ns=ns, nk=nk),
        out_shape=[
            jax.ShapeDtypeStruct((m, d), F32),
            jax.ShapeDtypeStruct((m, nstate), F32),
            jax.ShapeDtypeStruct((m, nstate), F32),
        ],
        scratch_shapes=[pltpu.VMEM((m, d), F32)],
        compiler_params=pltpu.CompilerParams(vmem_limit_bytes=VMEM_LIMIT),
        name="s5_step",
    )(*args)


def _rope_tables(pos, half):
    inv = ROPE_BASE ** (-jnp.arange(half, dtype=F32) / half)
    ang = pos.astype(F32)[:, None] * inv[None, :]
    return jnp.cos(ang), jnp.sin(ang)


def _lane_block(parts, n):
    out = jnp.zeros((n, LANES), F32)
    for off, arr in parts:
        out = out.at[:, off:off + arr.shape[1]].set(arr)
    return out


def _kv_body(x_ref, g_ref, w_ref, cn_ref, gpe_ref, cos_ref, sin_ref, *rest, rank, rope, with_heads, nh):
    if with_heads:
        wuk_ref, wuvt_ref, c_ref, kpe_ref, k_ref, vt_ref = rest
    else:
        c_ref, kpe_ref = rest
    hk = _rms(x_ref[...], g_ref[...]).astype(BF16)
    ck = jnp.dot(hk, w_ref[...], preferred_element_type=F32)
    c = _rms(ck[:, :rank], cn_ref[...])
    c_ref[...] = c

    def pe_block(i):
        pe = ck[:, rank + 2 * LANES * i:rank + 2 * LANES * i + LANES]
        rot = ck[:, rank + 2 * LANES * i + LANES:rank + 2 * LANES * (i + 1)]
        r = lax.rsqrt(jnp.sum(pe * pe, axis=-1, keepdims=True) * (1.0 / rope) + EPS)
        return (pe * gpe_ref[2 * i:2 * i + 1] * cos_ref[:, LANES * i:LANES * (i + 1)]
                + rot * gpe_ref[2 * i + 1:2 * i + 2] * sin_ref[:, LANES * i:LANES * (i + 1)]) * r

    kpe_ref[...] = pe_block(0)[:, :rope]
    if with_heads:
        cb = c.astype(BF16)
        pe_mid = pe_block(1)
        kn = jnp.dot(cb, wuk_ref[...], preferred_element_type=F32)
        for h in range(nh):
            k_ref[:, LANES * h:LANES * (h + 1)] = (kn[:, LANES * h:LANES * (h + 1)] + pe_mid).astype(BF16)
        vt = lax.dot_general(wuvt_ref[...], cb, (((1,), (1,)), ((), ())), preferred_element_type=F32)
        hv = vt.shape[0] // nh
        hvx = hv + VT_PAD
        ones_row = (lax.broadcasted_iota(jnp.int32, (VT_PAD, vt.shape[1]), 0) == 0).astype(BF16)
        for h in range(nh):
            vt_ref[hvx * h:hvx * h + hv, :] = vt[hv * h:hv * (h + 1)].astype(BF16)
            vt_ref[hvx * h + hv:hvx * (h + 1), :] = ones_row


def _kv(x, pos, kv_in_norm, w_dkv, c_norm, kpe_norm, w_uk, w_uv, *, with_heads, tm_pref=512):
    b, t, d = x.shape
    rank = c_norm.shape[0]
    rope = kpe_norm.shape[0]
    half = rope // 2
    nh, nope = w_uk.shape[1], w_uk.shape[2]
    hv = w_uv.shape[2]
    tm = _tile(t, tm_pref)
    nt = t // tm
    m = b * t
    w_pe = w_dkv[:, rank:]
    w_rot = jnp.concatenate([-w_pe[:, half:], w_pe[:, :half]], axis=1)
    zero = lambda n: jnp.zeros((d, n), F32)
    w_ext = jnp.concatenate([
        w_dkv[:, :rank],
        w_pe, zero(LANES - rope), w_rot, zero(LANES - rope),
        zero(nope), w_pe, zero(LANES - nope - rope), zero(nope), w_rot, zero(LANES - nope - rope),
    ], axis=1).astype(BF16)
    g_pe = kpe_norm[None, :]
    g_rot = jnp.concatenate([kpe_norm[half:], kpe_norm[:half]])[None, :]
    gpe = jnp.concatenate([_lane_block([(0, g_pe)], 1), _lane_block([(0, g_rot)], 1),
                           _lane_block([(nope, g_pe)], 1), _lane_block([(nope, g_rot)], 1)], axis=0)
    cos, sin = _rope_tables(pos, half)
    cos2, sin2 = jnp.concatenate([cos, cos], 1), jnp.concatenate([sin, sin], 1)
    cos_t = jnp.concatenate([_lane_block([(0, cos2)], t), _lane_block([(nope, cos2)], t)], axis=1)
    sin_t = jnp.concatenate([_lane_block([(0, sin2)], t), _lane_block([(nope, sin2)], t)], axis=1)

    in_specs = [
        pl.BlockSpec((tm, d), lambda i: (i, 0)),
        pl.BlockSpec((1, d), lambda i: (0, 0)),
        pl.BlockSpec(w_ext.shape, lambda i: (0, 0)),
        pl.BlockSpec((1, rank), lambda i: (0, 0)),
        pl.BlockSpec(gpe.shape, lambda i: (0, 0)),
        pl.BlockSpec((tm, 2 * LANES), lambda i: (i % nt, 0)),
        pl.BlockSpec((tm, 2 * LANES), lambda i: (i % nt, 0)),
    ]
    args = [x.reshape(m, d), kv_in_norm.reshape(1, d), w_ext, c_norm.reshape(1, rank), gpe, cos_t, sin_t]
    out_specs = [pl.BlockSpec((tm, rank), lambda i: (i, 0)), pl.BlockSpec((tm, rope), lambda i: (i, 0))]
    out_shape = [jax.ShapeDtypeStruct((m, rank), F32), jax.ShapeDtypeStruct((m, rope), F32)]
    if with_heads:
        wuk = jnp.concatenate([w_uk, jnp.zeros((rank, nh, LANES - nope), F32)], axis=2)
        wuk = wuk.reshape(rank, nh * LANES).astype(BF16)
        wuvt = jnp.transpose(w_uv, (1, 2, 0)).reshape(nh * hv, rank).astype(BF16)
        in_specs += [pl.BlockSpec(wuk.shape, lambda i: (0, 0)), pl.BlockSpec(wuvt.shape, lambda i: (0, 0))]
        args += [wuk, wuvt]
        out_specs += [pl.BlockSpec((tm, nh * LANES), lambda i: (i, 0)),
                      pl.BlockSpec((None, nh * (hv + VT_PAD), tm), lambda i: (i // nt, 0, i % nt))]
        out_shape += [jax.ShapeDtypeStruct((m, nh * LANES), BF16),
                      jax.ShapeDtypeStruct((b, nh * (hv + VT_PAD), t), BF16)]
    return pl.pallas_call(
        functools.partial(_kv_body, rank=rank, rope=rope, with_heads=with_heads, nh=nh),
        grid=(m // tm,),
        in_specs=in_specs,
        out_specs=out_specs,
        out_shape=out_shape,
        compiler_params=_cparams(("parallel",)),
        name="shared_kv",
    )(*args)


def _q_body(x_ref, g_ref, wdq_ref, qn_ref, w1_ref, w2_ref, gain_ref, cos_ref, sin_ref, q_ref,
            *, nh, nope, rope):
    hb = _rms(x_ref[...], g_ref[...]).astype(BF16)
    cq = jnp.dot(hb, wdq_ref[...], preferred_element_type=F32)
    cqb = _rms(cq, qn_ref[...]).astype(BF16)
    qm = jnp.dot(cqb, w1_ref[...], preferred_element_type=F32)
    qr = jnp.dot(cqb, w2_ref[...], preferred_element_type=F32)
    lane = lax.broadcasted_iota(jnp.int32, (1, LANES), 1)
    is_nope = lane < nope
    is_pe = jnp.logical_and(lane >= nope, lane < nope + rope)
    gain, gain_rot = gain_ref[0:1], gain_ref[1:2]
    cos, sin = cos_ref[...], sin_ref[...]
    for h in range(nh):
        blk = qm[:, LANES * h:LANES * (h + 1)]
        rot = qr[:, LANES * h:LANES * (h + 1)]
        sq = blk * blk
        s_n = jnp.sum(jnp.where(is_nope, sq, 0.0), axis=-1, keepdims=True)
        s_p = jnp.sum(jnp.where(is_pe, sq, 0.0), axis=-1, keepdims=True)
        r = jnp.where(is_nope, lax.rsqrt(s_n * (1.0 / nope) + EPS), lax.rsqrt(s_p * (1.0 / rope) + EPS))
        q_ref[:, LANES * h:LANES * (h + 1)] = ((blk * gain * cos + rot * gain_rot * sin) * r).astype(BF16)


def _queries(x, pos, mix_norm, w_dq, q_norm, w_uq, qn_norm, qr_norm, *, tm_pref=512):
    b, t, d = x.shape
    qrank = w_dq.shape[1]
    nh = w_uq.shape[1]
    nope, rope = qn_norm.shape[0], qr_norm.shape[0]
    half = rope // 2
    scale = (nope + rope) ** -0.5 * math.log2(math.e)
    tm = _tile(t, tm_pref)
    nt = t // tm
    m = b * t
    w_n, w_p = w_uq[:, :, :nope], w_uq[:, :, nope:]
    pad = jnp.zeros((qrank, nh, LANES - nope - rope), F32)
    w1 = jnp.concatenate([w_n, w_p, pad], axis=2).reshape(qrank, nh * LANES).astype(BF16)
    w_rot = jnp.concatenate([-w_p[:, :, half:], w_p[:, :, :half]], axis=2)
    w2 = jnp.concatenate([jnp.zeros_like(w_n), w_rot, pad], axis=2).reshape(qrank, nh * LANES).astype(BF16)
    g_rot = jnp.concatenate([qr_norm[half:], qr_norm[:half]])
    gain = jnp.concatenate([
        _lane_block([(0, qn_norm[None, :] * scale), (nope, qr_norm[None, :] * scale)], 1),
        _lane_block([(nope, g_rot[None, :] * scale)], 1)], axis=0)
    cos, sin = _rope_tables(pos, half)
    cos_t = _lane_block([(0, jnp.ones((t, nope), F32)), (nope, jnp.concatenate([cos, cos], 1))], t)
    sin_t = _lane_block([(nope, jnp.concatenate([sin, sin], 1))], t)
    return pl.pallas_call(
        functools.partial(_q_body, nh=nh, nope=nope, rope=rope),
        grid=(m // tm,),
        in_specs=[
            pl.BlockSpec((tm, d), lambda i: (i, 0)),
            pl.BlockSpec((1, d), lambda i: (0, 0)),
            pl.BlockSpec(w_dq.shape, lambda i: (0, 0)),
            pl.BlockSpec((1, qrank), lambda i: (0, 0)),
            pl.BlockSpec(w1.shape, lambda i: (0, 0)),
            pl.BlockSpec(w2.shape, lambda i: (0, 0)),
            pl.BlockSpec(gain.shape, lambda i: (0, 0)),
            pl.BlockSpec((tm, LANES), lambda i: (i % nt, 0)),
            pl.BlockSpec((tm, LANES), lambda i: (i % nt, 0)),
        ],
        out_specs=pl.BlockSpec((tm, nh * LANES), lambda i: (i, 0)),
        out_shape=jax.ShapeDtypeStruct((m, nh * LANES), BF16),
        compiler_params=_cparams(("parallel",)),
        name="mla_queries",
    )(x.reshape(m, d), mix_norm.reshape(1, d), w_dq.astype(BF16), q_norm.reshape(1, qrank),
      w1, w2, gain, cos_t, sin_t)


def _attn_prompt_body(q_ref, k_ref, vt_ref, o_ref, *, tq, hp, hv):
    qi = pl.program_id(2)
    hvx = hv + VT_PAD
    qs = [q_ref[:, LANES * a:LANES * (a + 1)] for a in range(hp)]

    def block(ki, carry, masked):
        off = pl.multiple_of(ki * tq, tq)
        ss = []
        for a in range(hp):
            k = k_ref[pl.ds(off, tq), LANES * a:LANES * (a + 1)]
            ss.append(lax.dot_general(k, qs[a], (((1,), (1,)), ((), ())),
                                      preferred_element_type=F32))
        ps, stats = [], []
        for a in range(hp):
            m_i = carry[a][0]
            s = ss[a]
            if masked:
                kpos = lax.broadcasted_iota(jnp.int32, (tq, tq), 0)
                qpos = lax.broadcasted_iota(jnp.int32, (tq, tq), 1)
                s = jnp.where(kpos <= qpos, s, NEG_BIG)
            m_new = jnp.maximum(m_i, jnp.max(s, axis=0, keepdims=True))
            stats.append((m_new, jnp.exp2(m_i - m_new)))
            ps.append(jnp.exp2(s - m_new).astype(BF16))
        out = []
        for a in range(hp):
            m_new, alpha = stats[a]
            pv = jnp.dot(vt_ref[hvx * a:hvx * (a + 1), pl.ds(off, tq)], ps[a], preferred_element_type=F32)
            out.append((m_new, alpha * carry[a][1] + pv))
        return tuple(out)

    init = tuple((jnp.full((1, tq), NEG_BIG, F32), jnp.zeros((hvx, tq), F32)) for _ in range(hp))
    carry = lax.fori_loop(0, qi, lambda ki, c: block(ki, c, False), init)
    fin = block(qi, carry, True)
    for a in range(hp):
        acc = fin[a][1]
        o_ref[hv * a:hv * (a + 1), :] = (acc[:hv] / acc[hv:hv + 1]).astype(BF16)


def _attn_prompt(q, k, vt, *, nh, hv, tq_pref=512, hp=4):
    b, t, _ = q.shape
    tq = _tile(t, tq_pref)
    assert nh % hp == 0
    return pl.pallas_call(
        functools.partial(_attn_prompt_body, tq=tq, hp=hp, hv=hv),
        grid=(b, nh // hp, t // tq),
        in_specs=[
            pl.BlockSpec((None, tq, hp * LANES), lambda i, h, j: (i, j, h)),
            pl.BlockSpec((None, t, hp * LANES), lambda i, h, j: (i, 0, h)),
            pl.BlockSpec((None, hp * (hv + VT_PAD), t), lambda i, h, j: (i, h, 0)),
        ],
        out_specs=pl.BlockSpec((None, hp * hv, tq), lambda i, h, j: (i, h, j)),
        out_shape=jax.ShapeDtypeStruct((b, nh * hv, t), BF16),
        compiler_params=_cparams(("parallel", "parallel", "arbitrary")),
        name="attn_prompt",
    )(q, k, vt)


def _oproj_t_body(ot_ref, x_ref, wo_ref, o_ref):
    m = lax.dot_general(ot_ref[...], wo_ref[...], (((0,), (0,)), ((), ())), preferred_element_type=F32)
    o_ref[...] = x_ref[...] + m


def _oproj_t(ot, x, w_o, *, tm_pref=512):
    b, t, d = x.shape
    e = ot.shape[1]
    tm = _tile(t, tm_pref)
    return pl.pallas_call(
        _oproj_t_body,
        grid=(b, t // tm),
        in_specs=[
            pl.BlockSpec((None, e, tm), lambda i, j: (i, 0, j)),
            pl.BlockSpec((None, tm, d), lambda i, j: (i, j, 0)),
            pl.BlockSpec((e, d), lambda i, j: (0, 0)),
        ],
        out_specs=pl.BlockSpec((None, tm, d), lambda i, j: (i, j, 0)),
        out_shape=jax.ShapeDtypeStruct((b, t, d), F32),
        compiler_params=_cparams(("parallel", "parallel")),
        name="attn_out_prompt",
    )(ot, x, w_o.astype(BF16))


def _absorb_body(q_ref, w_ref, o_ref):
    o_ref[...] = jnp.dot(q_ref[...], w_ref[...], preferred_element_type=F32).astype(BF16)


def _absorb(q, w_uk, rope):
    m = q.shape[0]
    rank, nh, nope = w_uk.shape
    width = rank + LANES
    w = jnp.zeros((nh, LANES, width), F32)
    w = w.at[:, :nope, :rank].set(jnp.transpose(w_uk, (1, 2, 0)))
    w = w.at[:, nope:nope + rope, rank:rank + rope].set(jnp.broadcast_to(jnp.eye(rope, dtype=F32), (nh, rope, rope)))
    out = pl.pallas_call(
        _absorb_body,
        grid=(nh,),
        in_specs=[pl.BlockSpec((m, LANES), lambda h: (0, h)),
                  pl.BlockSpec((None, LANES, width), lambda h: (h, 0, 0))],
        out_specs=pl.BlockSpec((m, width), lambda h: (0, h)),
        out_shape=jax.ShapeDtypeStruct((m, nh * width), BF16),
        compiler_params=_cparams(("parallel",)),
        name="absorb_queries",
    )(q, w.astype(BF16))
    return out.reshape(m, nh, width)


CHUNK_AHEAD = 2


def _attn_sample_body(pt_ref, q_ref, cn_ref, pn_ref, ckv_hbm, kpe_hbm, o_ref, cbuf, pbuf, sem,
                      *, npg, nch, page, rank, rope):
    b = pl.program_id(0)
    total = pl.num_programs(0) * nch

    def page_copies(chunk, slot):
        out = []
        for i in range(npg):
            pid = pt_ref[chunk * npg + i]
            out.append(pltpu.make_async_copy(ckv_hbm.at[pid], cbuf.at[slot, pl.ds(i * page, page)],
                                             sem.at[0, slot]))
            out.append(pltpu.make_async_copy(kpe_hbm.at[pid], pbuf.at[slot, :, pl.ds(i * page, page)],
                                             sem.at[1, slot]))
        return out

    @pl.when(b == 0)
    def _():
        for c in range(CHUNK_AHEAD):
            for cp in page_copies(jnp.minimum(c, total - 1), c % nch):
                cp.start()

    q_lat = q_ref[0, :, :rank]
    q_pe = q_ref[0, :, rank:rank + rope]
    nh = q_lat.shape[0]
    m_i = jnp.full((nh, 1), NEG_BIG, F32)
    l_i = jnp.zeros((nh, 1), F32)
    acc = jnp.zeros((nh, rank), F32)
    pending = None
    for j in range(nch):
        for cp in page_copies(b * nch + j, j):
            cp.wait()
        c = cbuf[j].astype(BF16)
        s = (lax.dot_general(q_lat, c, (((1,), (1,)), ((), ())), preferred_element_type=F32)
             + jnp.dot(q_pe, pbuf[j].astype(BF16), preferred_element_type=F32))
        if pending is not None:
            acc = pending[0] * acc + jnp.dot(pending[1], pending[2], preferred_element_type=F32)
        m_new = jnp.maximum(m_i, jnp.max(s, axis=1, keepdims=True))
        alpha = jnp.exp2(m_i - m_new)
        p = jnp.exp2(s - m_new)
        l_i = alpha * l_i + jnp.sum(p, axis=1, keepdims=True)
        m_i = m_new
        pending = (alpha, p.astype(BF16), c)
        nxt = b * nch + j + CHUNK_AHEAD
        for cp in page_copies(jnp.minimum(nxt, total - 1), (j + CHUNK_AHEAD) % nch):
            cp.start()
    acc = pending[0] * acc + jnp.dot(pending[1], pending[2], preferred_element_type=F32)

    @pl.when(b == pl.num_programs(0) - 1)
    def _():
        for j in range(nch - CHUNK_AHEAD, nch):
            for cp in page_copies(total - 1, (j + CHUNK_AHEAD) % nch):
                cp.wait()

    c_new = cn_ref[0].astype(BF16).astype(F32)
    p_new = pn_ref[0].astype(BF16).astype(F32)
    s_new = (jnp.sum(q_lat.astype(F32) * c_new, axis=1, keepdims=True)
             + jnp.sum(q_pe.astype(F32) * p_new, axis=1, keepdims=True))
    m_fin = jnp.maximum(m_i, s_new)
    a_old = jnp.exp2(m_i - m_fin)
    p_n = jnp.exp2(s_new - m_fin)
    l_fin = a_old * l_i + p_n
    o_ref[0] = (a_old * acc + p_n.astype(BF16).astype(F32) * c_new) / l_fin


def _attn_sample(qabs, cache_ckv, cache_kpe, page_table, c_new, kpe_new, *, npg_pref=32):
    m, nh, width = qabs.shape
    _, page, rank = cache_ckv.shape
    rope = cache_kpe.shape[2]
    n_pages = page_table.shape[1]
    npg = npg_pref if n_pages % npg_pref == 0 else n_pages
    nch = n_pages // npg
    assert nch > CHUNK_AHEAD, "a slot must not be refilled while its chunk is in use"
    kpe_t = jnp.swapaxes(cache_kpe, 1, 2)
    grid_spec = pltpu.PrefetchScalarGridSpec(
        num_scalar_prefetch=1,
        grid=(m,),
        in_specs=[
            pl.BlockSpec((1, nh, width), lambda b, pt: (b, 0, 0)),
            pl.BlockSpec((1, 1, rank), lambda b, pt: (b, 0, 0)),
            pl.BlockSpec((1, 1, rope), lambda b, pt: (b, 0, 0)),
            pl.BlockSpec(memory_space=pl.ANY),
            pl.BlockSpec(memory_space=pl.ANY),
        ],
        out_specs=pl.BlockSpec((1, nh, rank), lambda b, pt: (b, 0, 0)),
        scratch_shapes=[
            pltpu.VMEM((nch, npg * page, rank), F32),
            pltpu.VMEM((nch, rope, npg * page), F32),
            pltpu.SemaphoreType.DMA((2, nch)),
        ],
    )
    return pl.pallas_call(
        functools.partial(_attn_sample_body, npg=npg, nch=nch, page=page, rank=rank, rope=rope),
        grid_spec=grid_spec,
        out_shape=jax.ShapeDtypeStruct((m, nh, rank), F32),
        compiler_params=_cparams(("arbitrary",)),
        name="attn_sample",
    )(page_table.reshape(-1), qabs, c_new.reshape(m, 1, rank), kpe_new.reshape(m, 1, rope),
      cache_ckv, kpe_t)


def _oproj_lat_body(ol_ref, x_ref, wuv_ref, wo_ref, o_ref, *, nh, rank, hv):
    acc = x_ref[...]
    for h in range(nh):
        oh = jnp.dot(ol_ref[:, rank * h:rank * (h + 1)].astype(BF16), wuv_ref[h], preferred_element_type=F32)
        acc = acc + jnp.dot(oh.astype(BF16), wo_ref[hv * h:hv * (h + 1), :], preferred_element_type=F32)
    o_ref[...] = acc


def _oproj_lat(o_lat, x, w_uv, w_o):
    m, nh, rank = o_lat.shape
    hv = w_uv.shape[2]
    return pl.pallas_call(
        functools.partial(_oproj_lat_body, nh=nh, rank=rank, hv=hv),
        out_shape=jax.ShapeDtypeStruct(x.shape, F32),
        compiler_params=pltpu.CompilerParams(vmem_limit_bytes=VMEM_LIMIT),
        name="attn_out_sample",
    )(o_lat.reshape(m, nh * rank), x, jnp.transpose(w_uv, (1, 0, 2)).astype(BF16), w_o.astype(BF16))


def kernel(x_prompt, x_sample, state_s5_re, state_s5_im, cache_ckv, cache_kpe, page_table, ffn_norm, ffn_w_gate, ffn_w_up, ffn_w_down, mix_norm, s5_a_re, s5_a_im, s5_log_dt, s5_b_re, s5_b_im, s5_c_re, s5_c_im, s5_d, s5_w_glu_v, s5_w_glu_g, mla_w_dq, mla_q_norm, mla_w_uq, mla_qn_norm, mla_qr_norm, mla_w_o, kv_in_norm, kv_w_dkv, kv_c_norm, kv_kpe_norm, kv_w_uk, kv_w_uv):
    bp, tp, d = x_prompt.shape
    bs, ts, _ = x_sample.shape
    assert ts == 1, "the sample group decodes one token per sequence"
    depth = ffn_norm.shape[0]
    n_a = s5_a_re.shape[0]
    assert depth == 2 and n_a == 1 and mla_w_dq.shape[0] == 1, "one S5 layer followed by one MLA layer"
    g, p = s5_a_re.shape[1], s5_a_re.shape[2]
    nh, hv = kv_w_uv.shape[1], kv_w_uv.shape[2]
    rope = kv_kpe_norm.shape[0]
    past_len = page_table.shape[1] * cache_ckv.shape[1]

    wg, wu, wd = ffn_w_gate.astype(BF16), ffn_w_up.astype(BF16), ffn_w_down.astype(BF16)

    def ffn(x, layer, idx):
        return _ffn(x, ffn_norm[layer, idx], wg, wu, wd, layer, idx)

    xp = x_prompt.reshape(bp * tp, d)
    xs = x_sample.reshape(bs, d)

    lam_re, lam_im, wb, wc = _s5_params(s5_a_re[0], s5_a_im[0], s5_log_dt[0], s5_b_re[0], s5_b_im[0],
                                        s5_c_re[0], s5_c_im[0])
    wv, wgl = s5_w_glu_v[0].astype(BF16), s5_w_glu_g[0].astype(BF16)
    xp = ffn(xp, 0, 0)
    xs = ffn(xs, 0, 0)
    xp, sre_p, sim_p = _s5_prompt(xp.reshape(bp, tp, d), mix_norm[0], lam_re, lam_im, wb, wc, s5_d[0], wv, wgl)
    xs, sre_s, sim_s = _s5_step(xs, mix_norm[0], state_s5_re[0].reshape(bs, g * p),
                                state_s5_im[0].reshape(bs, g * p), lam_re, lam_im, wb, wc, s5_d[0], wv, wgl)
    xp = ffn(xp.reshape(bp * tp, d), 0, 1)
    xs = ffn(xs, 0, 1)

    pos_p = jnp.arange(tp, dtype=jnp.int32)
    pos_s = jnp.full((bs,), past_len, dtype=jnp.int32)
    kv_args = (kv_in_norm, kv_w_dkv, kv_c_norm, kv_kpe_norm, kv_w_uk, kv_w_uv)
    ckv_p, kpe_p, k_p, vt_p = _kv(xp.reshape(bp, tp, d), pos_p, *kv_args, with_heads=True)
    ckv_s, kpe_s = _kv(xs.reshape(1, bs, d), pos_s, *kv_args, with_heads=False)

    xp = ffn(xp, 1, 0)
    xs = ffn(xs, 1, 0)
    q_args = (mix_norm[1], mla_w_dq[0], mla_q_norm[0], mla_w_uq[0], mla_qn_norm[0], mla_qr_norm[0])
    q_p = _queries(xp.reshape(bp, tp, d), pos_p, *q_args)
    ot = _attn_prompt(q_p.reshape(bp, tp, nh * LANES), k_p.reshape(bp, tp, nh * LANES), vt_p, nh=nh, hv=hv)
    xp = _oproj_t(ot, xp.reshape(bp, tp, d), mla_w_o[0]).reshape(bp * tp, d)

    q_s = _queries(xs.reshape(1, bs, d), pos_s, *q_args)
    qabs = _absorb(q_s, kv_w_uk, rope)
    o_lat = _attn_sample(qabs, cache_ckv, cache_kpe, page_table, ckv_s, kpe_s)
    xs = _oproj_lat(o_lat, xs, kv_w_uv, mla_w_o[0])

    xp = ffn(xp, 1, 1)
    xs = ffn(xs, 1, 1)

    return (xp.reshape(bp, tp, d), xs.reshape(bs, 1, d),
            sre_p.reshape(n_a, bp, g, p), sim_p.reshape(n_a, bp, g, p),
            sre_s.reshape(n_a, bs, g, p), sim_s.reshape(n_a, bs, g, p),
            ckv_p.reshape(bp, tp, -1), kpe_p.reshape(bp, tp, rope),
            ckv_s.reshape(bs, 1, -1), kpe_s.reshape(bs, 1, rope))
```

```python
import functools
import math

import jax
import jax.numpy as jnp
from jax import lax
from jax.experimental import pallas as pl
from jax.experimental.pallas import tpu as pltpu

F32 = jnp.float32
BF16 = jnp.bfloat16

EPS = 1e-6
FFN_RES = 0.5
NEG_BIG = -1e30
ROPE_BASE = 10000.0

GROUP_CH = 16
LANES = 128
SUBLANES = 8
S5_SLAB = 256
S5_PITCH = 12
VT_PAD = 16
VMEM_LIMIT = 56 * 1024 * 1024


def _tile(n, pref):
    if n <= pref:
        return n
    for t in range(pref, 7, -1):
        if n % t == 0 and t % 8 == 0:
            return t
    return n


def _rms(x, gain):
    ms = jnp.mean(x * x, axis=-1, keepdims=True)
    return x * lax.rsqrt(ms + EPS) * gain


def _cparams(sem):
    return pltpu.CompilerParams(dimension_semantics=sem, vmem_limit_bytes=VMEM_LIMIT)


def _ffn_body(x_ref, g_ref, wg_ref, wu_ref, wd_ref, o_ref):
    x = x_ref[...]
    xn = _rms(x, g_ref[...]).astype(BF16)
    gate = jnp.dot(xn, wg_ref[...], preferred_element_type=F32)
    up = jnp.dot(xn, wu_ref[...], preferred_element_type=F32)
    hid = (gate * jax.nn.sigmoid(gate) * up).astype(BF16)
    o_ref[...] = x + FFN_RES * jnp.dot(hid, wd_ref[...], preferred_element_type=F32)


def _resident(shape, index_map):
    return pl.BlockSpec(shape, index_map, pipeline_mode=pl.Buffered(1))


def _ffn(x, gain, w_gate, w_up, w_down, layer, idx, *, tm_pref=512):
    m, d = x.shape
    ff = w_gate.shape[3]
    tm = _tile(m, tm_pref)
    pick = lambda i: (layer, idx, 0, 0)
    return pl.pallas_call(
        _ffn_body,
        grid=(m // tm,),
        in_specs=[
            pl.BlockSpec((tm, d), lambda i: (i, 0)),
            _resident((1, d), lambda i: (0, 0)),
            _resident((None, None, d, ff), pick),
            _resident((None, None, d, ff), pick),
            _resident((None, None, ff, d), pick),
        ],
        out_specs=pl.BlockSpec((tm, d), lambda i: (i, 0)),
        out_shape=jax.ShapeDtypeStruct((m, d), F32),
        compiler_params=_cparams(("parallel",)),
        name="ffn",
    )(x, gain.reshape(1, d), w_gate, w_up, w_down)


def _s5_params(a_re, a_im, log_dt, b_re, b_im, c_re, c_im):
    g, p = a_re.shape
    dt = jnp.exp(log_dt)[:, None]
    mag = jnp.exp(dt * a_re)
    lam_re, lam_im = mag * jnp.cos(dt * a_im), mag * jnp.sin(dt * a_im)
    den = a_re * a_re + a_im * a_im
    f_re = ((lam_re - 1.0) * a_re + lam_im * a_im) / den
    f_im = (lam_im * a_re - (lam_re - 1.0) * a_im) / den
    bb_re = f_re[..., None] * b_re - f_im[..., None] * b_im
    bb_im = f_re[..., None] * b_im + f_im[..., None] * b_re
    gs = S5_SLAB // GROUP_CH
    ns = g // gs
    same_group = jnp.eye(gs, dtype=bool)[None, :, None, :, None]
    nk = gs * p // LANES

    def pack_b(bb):
        bb = jnp.transpose(bb.reshape(ns, gs, p, GROUP_CH), (0, 1, 3, 2))
        w = jnp.where(same_group, bb[:, :, :, None, :], 0.0)
        return w.reshape(ns, S5_SLAB, nk, LANES)

    wb = jnp.stack([pack_b(bb_re), pack_b(bb_im)], axis=3)
    wb = wb.reshape(ns, S5_SLAB, 2 * gs * p).astype(BF16)

    def pack_c(cc):
        cc = jnp.transpose(cc.reshape(ns, gs, GROUP_CH, p), (0, 3, 1, 2))
        w = jnp.where(same_group, cc[:, None, :, :, :], 0.0)
        return w.reshape(ns, nk, LANES, S5_SLAB)

    wc = jnp.stack([pack_c(c_re), -pack_c(c_im)], axis=2)
    wc = wc.reshape(ns, 2 * gs * p, S5_SLAB).astype(BF16)
    return lam_re, lam_im, wb, wc


def _s5_prompt_body(x_ref, g_ref, wb_ref, wc_ref, lr_ref, li_ref, d_ref, wv_ref, wgl_ref,
                    o_ref, sre_ref, sim_ref, xr_ref, xi_ref, cr_ref, ci_ref, y_ref, *, tc, ns, nk):
    t_idx = pl.program_id(1)

    @pl.when(t_idx == 0)
    def _():
        cr_ref[...] = jnp.zeros_like(cr_ref)
        ci_ref[...] = jnp.zeros_like(ci_ref)

    x = x_ref[0]
    h = _rms(x, g_ref[...])
    hb = h.astype(BF16)
    for s in range(ns):
        us = hb[:, S5_SLAB * s:S5_SLAB * (s + 1)]
        for k in range(nk):
            xk = jnp.dot(us, wb_ref[s, :, 2 * LANES * k:2 * LANES * (k + 1)],
                         preferred_element_type=F32)
            xr_ref[s, pl.ds(k, tc, stride=S5_PITCH), :] = xk[:, :LANES]
            xi_ref[s, pl.ds(k, tc, stride=S5_PITCH), :] = xk[:, LANES:]

    lam = [(lr_ref[s], li_ref[s]) for s in range(ns)]

    def step(t, carry):
        row = pl.multiple_of(t * S5_PITCH, math.gcd(S5_PITCH, SUBLANES))
        out = []
        for s in range(ns):
            sr, si = carry[s]
            lr, li = lam[s]
            nr = lr * sr - li * si + xr_ref[s, pl.ds(row, nk), :]
            ni = lr * si + li * sr + xi_ref[s, pl.ds(row, nk), :]
            xr_ref[s, pl.ds(row, nk), :] = nr
            xi_ref[s, pl.ds(row, nk), :] = ni
            out.append((nr, ni))
        return tuple(out)

    init = tuple((cr_ref[s], ci_ref[s]) for s in range(ns))
    fin = lax.fori_loop(0, tc, step, init, unroll=8)
    for s in range(ns):
        cr_ref[s] = fin[s][0]
        ci_ref[s] = fin[s][1]

    for s in range(ns):
        acc = None
        for k in range(nk):
            sk = jnp.concatenate([xr_ref[s, pl.ds(k, tc, stride=S5_PITCH), :],
                                  xi_ref[s, pl.ds(k, tc, stride=S5_PITCH), :]], axis=1).astype(BF16)
            part = jnp.dot(sk, wc_ref[s, 2 * LANES * k:2 * LANES * (k + 1), :],
                           preferred_element_type=F32)
            acc = part if acc is None else acc + part
        y_ref[:, S5_SLAB * s:S5_SLAB * (s + 1)] = acc

    y = jax.nn.gelu(y_ref[...] + d_ref[...] * h, approximate=True).astype(BF16)
    val = jnp.dot(y, wv_ref[...], preferred_element_type=F32)
    gate = jnp.dot(y, wgl_ref[...], preferred_element_type=F32)
    o_ref[0] = x + val * jax.nn.sigmoid(gate)

    @pl.when(t_idx == pl.num_programs(1) - 1)
    def _():
        sre_ref[0] = cr_ref[...].reshape(ns * nk, LANES)
        sim_ref[0] = ci_ref[...].reshape(ns * nk, LANES)


def _s5_prompt(x, gain, lam_re, lam_im, wb, wc, d_skip, w_v, w_g, *, tc_pref=512):
    b, t, d = x.shape
    ns = wb.shape[0]
    nk = wb.shape[2] // (2 * LANES)
    assert nk == SUBLANES, "one token's slab state must fill whole vregs"
    tc = _tile(t, tc_pref)
    lr = lam_re.reshape(ns, nk, LANES)
    li = lam_im.reshape(ns, nk, LANES)
    const3 = lambda i, j: (0, 0, 0)
    const2 = lambda i, j: (0, 0)
    out, s_re, s_im = pl.pallas_call(
        functools.partial(_s5_prompt_body, tc=tc, ns=ns, nk=nk),
        grid=(b, t // tc),
        in_specs=[
            pl.BlockSpec((1, tc, d), lambda i, j: (i, j, 0)),
            _resident((1, d), const2),
            _resident(wb.shape, const3),
            _resident(wc.shape, const3),
            _resident(lr.shape, const3),
            _resident(li.shape, const3),
            _resident((1, d), const2),
            _resident(w_v.shape, const2),
            _resident(w_g.shape, const2),
        ],
        out_specs=[
            pl.BlockSpec((1, tc, d), lambda i, j: (i, j, 0)),
            pl.BlockSpec((1, ns * nk, LANES), lambda i, j: (i, 0, 0)),
            pl.BlockSpec((1, ns * nk, LANES), lambda i, j: (i, 0, 0)),
        ],
        out_shape=[
            jax.ShapeDtypeStruct((b, t, d), F32),
            jax.ShapeDtypeStruct((b, ns * nk, LANES), F32),
            jax.ShapeDtypeStruct((b, ns * nk, LANES), F32),
        ],
        scratch_shapes=[
            pltpu.VMEM((ns, tc * S5_PITCH, LANES), F32),
            pltpu.VMEM((ns, tc * S5_PITCH, LANES), F32),
            pltpu.VMEM((ns, nk, LANES), F32),
            pltpu.VMEM((ns, nk, LANES), F32),
            pltpu.VMEM((tc, d), F32),
        ],
        compiler_params=_cparams(("parallel", "arbitrary")),
        name="s5_prompt",
    )(x, gain.reshape(1, d), wb, wc, lr, li, d_skip.reshape(1, d), w_v, w_g)
    return out, s_re, s_im


def _s5_step_body(x_ref, g_ref, s0r_ref, s0i_ref, wb_ref, wc_ref, lr_ref, li_ref, d_ref,
                  wv_ref, wgl_ref, o_ref, sre_ref, sim_ref, y_ref, *, ns, nk):
    x = x_ref[...]
    h = _rms(x, g_ref[...])
    hb = h.astype(BF16)
    for s in range(ns):
        xs = jnp.dot(hb[:, S5_SLAB * s:S5_SLAB * (s + 1)], wb_ref[s], preferred_element_type=F32)
        acc = None
        for k in range(nk):
            col = (s * nk + k) * LANES
            lr = lr_ref[:, col:col + LANES]
            li = li_ref[:, col:col + LANES]
            s0r = s0r_ref[:, col:col + LANES]
            s0i = s0i_ref[:, col:col + LANES]
            nr = lr * s0r - li * s0i + xs[:, 2 * LANES * k:2 * LANES * k + LANES]
            ni = lr * s0i + li * s0r + xs[:, 2 * LANES * k + LANES:2 * LANES * (k + 1)]
            sre_ref[:, col:col + LANES] = nr
            sim_ref[:, col:col + LANES] = ni
            part = (jnp.dot(nr.astype(BF16), wc_ref[s, 2 * LANES * k:2 * LANES * k + LANES, :],
                            preferred_element_type=F32)
                    + jnp.dot(ni.astype(BF16), wc_ref[s, 2 * LANES * k + LANES:2 * LANES * (k + 1), :],
                              preferred_element_type=F32))
            acc = part if acc is None else acc + part
        y_ref[:, S5_SLAB * s:S5_SLAB * (s + 1)] = acc
    y = jax.nn.gelu(y_ref[...] + d_ref[...] * h, approximate=True).astype(BF16)
    val = jnp.dot(y, wv_ref[...], preferred_element_type=F32)
    gate = jnp.dot(y, wgl_ref[...], preferred_element_type=F32)
    o_ref[...] = x + val * jax.nn.sigmoid(gate)


def _s5_step(x, gain, s0_re, s0_im, lam_re, lam_im, wb, wc, d_skip, w_v, w_g):
    m, d = x.shape
    ns = wb.shape[0]
    nk = wb.shape[2] // (2 * LANES)
    nstate = s0_re.shape[1]
    args = (x, gain.reshape(1, d), s0_re, s0_im, wb, wc, lam_re.reshape(1, nstate),
            lam_im.reshape(1, nstate), d_skip.reshape(1, d), w_v, w_g)
    return pl.pallas_call(
        functools.partial(_s5_step_body, ns=ns, nk=nk),
        out_shape=[
            jax.ShapeDtypeStruct((m, d), F32),
            jax.ShapeDtypeStruct((m, nstate), F32),
            jax.ShapeDtypeStruct((m, nstate), F32),
        ],
        scratch_shapes=[pltpu.VMEM((m, d), F32)],
        compiler_params=pltpu.CompilerParams(vmem_limit_bytes=VMEM_LIMIT),
        name="s5_step",
    )(*args)


def _rope_tables(pos, half):
    inv = ROPE_BASE ** (-jnp.arange(half, dtype=F32) / half)
    ang = pos.astype(F32)[:, None] * inv[None, :]
    return jnp.cos(ang), jnp.sin(ang)


def _kv_body(x_ref, g_ref, w_ref, cn_ref, cos_ref, sin_ref, *rest, rank, rope, with_heads, nh):
    if with_heads:
        wuk_ref, wuvt_ref, c_ref, kpe_ref, k_ref, vt_ref = rest
    else:
        c_ref, kpe_ref = rest
    hk = _rms(x_ref[...], g_ref[...]).astype(BF16)
    ck = jnp.dot(hk, w_ref[...], preferred_element_type=F32)
    c = _rms(ck[:, :rank], cn_ref[...])
    c_ref[...] = c

    def pe_block(i, copies):
        pe = ck[:, rank + 2 * LANES * i:rank + 2 * LANES * i + LANES]
        rot = ck[:, rank + 2 * LANES * i + LANES:rank + 2 * LANES * (i + 1)]
        r = lax.rsqrt(jnp.sum(pe * pe, axis=-1, keepdims=True) * (1.0 / (rope * copies)) + EPS)
        return (pe * cos_ref[:, LANES * i:LANES * (i + 1)] + rot * sin_ref[:, LANES * i:LANES * (i + 1)]) * r

    kpe_ref[...] = pe_block(0, 1)[:, :rope]
    if with_heads:
        cb = c.astype(BF16)
        pe_mid = pe_block(1, 2)
        kn = jnp.dot(cb, wuk_ref[...], preferred_element_type=F32)
        for h in range(nh):
            k_ref[:, LANES * h:LANES * (h + 1)] = (kn[:, LANES * h:LANES * (h + 1)] + pe_mid).astype(BF16)
        vt = lax.dot_general(wuvt_ref[...], cb, (((1,), (1,)), ((), ())), preferred_element_type=F32)
        hv = vt.shape[0] // nh
        hvx = hv + VT_PAD
        ones_row = (lax.broadcasted_iota(jnp.int32, (VT_PAD, vt.shape[1]), 0) == 0).astype(BF16)
        for h in range(nh):
            vt_ref[hvx * h:hvx * h + hv, :] = vt[hv * h:hv * (h + 1)].astype(BF16)
            vt_ref[hvx * h + hv:hvx * (h + 1), :] = ones_row


def _kv(x, pos, kv_in_norm, w_dkv, c_norm, kpe_norm, w_uk, w_uv, *, with_heads, tm_pref=512):
    b, t, d = x.shape
    rank = c_norm.shape[0]
    rope = kpe_norm.shape[0]
    half = rope // 2
    nh, nope = w_uk.shape[1], w_uk.shape[2]
    hv = w_uv.shape[2]
    tm = _tile(t, tm_pref)
    nt = t // tm
    m = b * t
    assert nope + 2 * rope == LANES
    w_pe = w_dkv[:, rank:]
    w_rot = jnp.concatenate([-w_pe[:, half:], w_pe[:, :half]], axis=1)
    zero = lambda n: jnp.zeros((d, n), F32)
    w_ext = jnp.concatenate([
        w_dkv[:, :rank],
        w_pe, zero(LANES - rope), w_rot, zero(LANES - rope),
        zero(nope), w_pe, w_pe, zero(nope), w_rot, w_rot,
    ], axis=1).astype(BF16)
    cos, sin = _rope_tables(pos, half)
    g_rot = jnp.concatenate([kpe_norm[half:], kpe_norm[:half]])
    gcos = jnp.concatenate([cos, cos], 1) * kpe_norm[None, :]
    gsin = jnp.concatenate([sin, sin], 1) * g_rot[None, :]
    zt = lambda n: jnp.zeros((t, n), F32)
    cos_t = jnp.concatenate([gcos, zt(LANES - rope), zt(nope), gcos, gcos], axis=1)
    sin_t = jnp.concatenate([gsin, zt(LANES - rope), zt(nope), gsin, gsin], axis=1)

    in_specs = [
        pl.BlockSpec((tm, d), lambda i: (i, 0)),
        pl.BlockSpec((1, d), lambda i: (0, 0)),
        pl.BlockSpec(w_ext.shape, lambda i: (0, 0)),
        pl.BlockSpec((1, rank), lambda i: (0, 0)),
        pl.BlockSpec((tm, 2 * LANES), lambda i: (i % nt, 0)),
        pl.BlockSpec((tm, 2 * LANES), lambda i: (i % nt, 0)),
    ]
    args = [x.reshape(m, d), kv_in_norm.reshape(1, d), w_ext, c_norm.reshape(1, rank), cos_t, sin_t]
    out_specs = [pl.BlockSpec((tm, rank), lambda i: (i, 0)), pl.BlockSpec((tm, rope), lambda i: (i, 0))]
    out_shape = [jax.ShapeDtypeStruct((m, rank), F32), jax.ShapeDtypeStruct((m, rope), F32)]
    if with_heads:
        wuk = jnp.concatenate([w_uk, jnp.zeros((rank, nh, LANES - nope), F32)], axis=2)
        wuk = wuk.reshape(rank, nh * LANES).astype(BF16)
        wuvt = jnp.transpose(w_uv, (1, 2, 0)).reshape(nh * hv, rank).astype(BF16)
        in_specs += [pl.BlockSpec(wuk.shape, lambda i: (0, 0)), pl.BlockSpec(wuvt.shape, lambda i: (0, 0))]
        args += [wuk, wuvt]
        out_specs += [pl.BlockSpec((tm, nh * LANES), lambda i: (i, 0)),
                      pl.BlockSpec((None, nh * (hv + VT_PAD), tm), lambda i: (i // nt, 0, i % nt))]
        out_shape += [jax.ShapeDtypeStruct((m, nh * LANES), BF16),
                      jax.ShapeDtypeStruct((b, nh * (hv + VT_PAD), t), BF16)]
    return pl.pallas_call(
        functools.partial(_kv_body, rank=rank, rope=rope, with_heads=with_heads, nh=nh),
        grid=(m // tm,),
        in_specs=in_specs,
        out_specs=out_specs,
        out_shape=out_shape,
        compiler_params=_cparams(("parallel",)),
        name="shared_kv",
    )(*args)


def _q_body(x_ref, g_ref, wdq_ref, qn_ref, w1_ref, tab_ref, q_ref, *, nh, nope, rope):
    hb = _rms(x_ref[...], g_ref[...]).astype(BF16)
    cq = jnp.dot(hb, wdq_ref[...], preferred_element_type=F32)
    cqb = _rms(cq, qn_ref[...]).astype(BF16)
    qm = jnp.dot(cqb, w1_ref[...], preferred_element_type=F32)
    lane = lax.broadcasted_iota(jnp.int32, (1, LANES), 1)
    is_nope = lane < nope
    is_pe = jnp.logical_and(lane >= nope, lane < nope + rope)
    tab = tab_ref[...]
    for h in range(nh):
        blk = qm[:, LANES * h:LANES * (h + 1)]
        sq = blk * blk
        s_n = jnp.sum(jnp.where(is_nope, sq, 0.0), axis=-1, keepdims=True)
        s_p = jnp.sum(jnp.where(is_pe, sq, 0.0), axis=-1, keepdims=True)
        r = jnp.where(is_nope, lax.rsqrt(s_n * (1.0 / nope) + EPS), lax.rsqrt(s_p * (1.0 / rope) + EPS))
        q_ref[:, LANES * h:LANES * (h + 1)] = (blk * tab * r).astype(BF16)


def _queries(x, pos, mix_norm, w_dq, q_norm, w_uq, qn_norm, qr_norm, *, tm_pref=512):
    b, t, d = x.shape
    qrank = w_dq.shape[1]
    nh = w_uq.shape[1]
    nope, rope = qn_norm.shape[0], qr_norm.shape[0]
    assert nope + 2 * rope == LANES
    half = rope // 2
    scale = (nope + rope) ** -0.5 * math.log2(math.e)
    tm = _tile(t, tm_pref)
    nt = t // tm
    m = b * t
    w_n, w_p = w_uq[:, :, :nope], w_uq[:, :, nope:]
    w_rot = jnp.concatenate([-w_p[:, :, half:], w_p[:, :, :half]], axis=2)
    w1 = jnp.concatenate([w_n, w_p, w_rot], axis=2).reshape(qrank, nh * LANES).astype(BF16)
    g_rot = jnp.concatenate([qr_norm[half:], qr_norm[:half]])
    cos, sin = _rope_tables(pos, half)
    tab = jnp.concatenate([
        jnp.broadcast_to(qn_norm[None, :], (t, nope)),
        jnp.concatenate([cos, cos], 1) * qr_norm[None, :],
        jnp.concatenate([sin, sin], 1) * g_rot[None, :]], axis=1) * scale
    return pl.pallas_call(
        functools.partial(_q_body, nh=nh, nope=nope, rope=rope),
        grid=(m // tm,),
        in_specs=[
            pl.BlockSpec((tm, d), lambda i: (i, 0)),
            pl.BlockSpec((1, d), lambda i: (0, 0)),
            pl.BlockSpec(w_dq.shape, lambda i: (0, 0)),
            pl.BlockSpec((1, qrank), lambda i: (0, 0)),
            pl.BlockSpec(w1.shape, lambda i: (0, 0)),
            pl.BlockSpec((tm, LANES), lambda i: (i % nt, 0)),
        ],
        out_specs=pl.BlockSpec((tm, nh * LANES), lambda i: (i, 0)),
        out_shape=jax.ShapeDtypeStruct((m, nh * LANES), BF16),
        compiler_params=_cparams(("parallel",)),
        name="mla_queries",
    )(x.reshape(m, d), mix_norm.reshape(1, d), w_dq.astype(BF16), q_norm.reshape(1, qrank), w1, tab)


def _attn_prompt_body(q_ref, k_ref, vt_ref, o_ref, *, tq, hp, hv):
    qi = pl.program_id(2)
    hvx = hv + VT_PAD
    qs = [q_ref[:, LANES * a:LANES * (a + 1)] for a in range(hp)]

    def block(ki, carry, masked):
        off = pl.multiple_of(ki * tq, tq)
        ss = []
        for a in range(hp):
            k = k_ref[pl.ds(off, tq), LANES * a:LANES * (a + 1)]
            ss.append(lax.dot_general(k, qs[a], (((1,), (1,)), ((), ())),
                                      preferred_element_type=F32))
        ps, stats = [], []
        for a in range(hp):
            m_i = carry[a][0]
            s = ss[a]
            if masked:
                kpos = lax.broadcasted_iota(jnp.int32, (tq, tq), 0)
                qpos = lax.broadcasted_iota(jnp.int32, (tq, tq), 1)
                s = jnp.where(kpos <= qpos, s, NEG_BIG)
            m_new = jnp.maximum(m_i, jnp.max(s, axis=0, keepdims=True))
            stats.append((m_new, jnp.exp2(m_i - m_new)))
            ps.append(jnp.exp2(s - m_new).astype(BF16))
        out = []
        for a in range(hp):
            m_new, alpha = stats[a]
            pv = jnp.dot(vt_ref[hvx * a:hvx * (a + 1), pl.ds(off, tq)], ps[a], preferred_element_type=F32)
            out.append((m_new, alpha * carry[a][1] + pv))
        return tuple(out)

    init = tuple((jnp.full((1, tq), NEG_BIG, F32), jnp.zeros((hvx, tq), F32)) for _ in range(hp))
    carry = lax.fori_loop(0, qi, lambda ki, c: block(ki, c, False), init)
    fin = block(qi, carry, True)
    for a in range(hp):
        acc = fin[a][1]
        o_ref[hv * a:hv * (a + 1), :] = (acc[:hv] / acc[hv:hv + 1]).astype(BF16)


def _attn_prompt(q, k, vt, *, nh, hv, tq_pref=512, hp=4):
    b, t, _ = q.shape
    tq = _tile(t, tq_pref)
    assert nh % hp == 0
    return pl.pallas_call(
        functools.partial(_attn_prompt_body, tq=tq, hp=hp, hv=hv),
        grid=(b, nh // hp, t // tq),
        in_specs=[
            pl.BlockSpec((None, tq, hp * LANES), lambda i, h, j: (i, j, h)),
            pl.BlockSpec((None, t, hp * LANES), lambda i, h, j: (i, 0, h)),
            pl.BlockSpec((None, hp * (hv + VT_PAD), t), lambda i, h, j: (i, h, 0)),
        ],
        out_specs=pl.BlockSpec((None, hp * hv, tq), lambda i, h, j: (i, h, j)),
        out_shape=jax.ShapeDtypeStruct((b, nh * hv, t), BF16),
        compiler_params=_cparams(("parallel", "parallel", "arbitrary")),
        name="attn_prompt",
    )(q, k, vt)


def _oproj_t_body(ot_ref, x_ref, wo_ref, o_ref):
    m = lax.dot_general(ot_ref[...], wo_ref[...], (((0,), (0,)), ((), ())), preferred_element_type=F32)
    o_ref[...] = x_ref[...] + m


def _oproj_t(ot, x, w_o, *, tm_pref=512):
    b, t, d = x.shape
    e = ot.shape[1]
    tm = _tile(t, tm_pref)
    return pl.pallas_call(
        _oproj_t_body,
        grid=(b, t // tm),
        in_specs=[
            pl.BlockSpec((None, e, tm), lambda i, j: (i, 0, j)),
            pl.BlockSpec((None, tm, d), lambda i, j: (i, j, 0)),
            pl.BlockSpec((e, d), lambda i, j: (0, 0)),
        ],
        out_specs=pl.BlockSpec((None, tm, d), lambda i, j: (i, j, 0)),
        out_shape=jax.ShapeDtypeStruct((b, t, d), F32),
        compiler_params=_cparams(("parallel", "parallel")),
        name="attn_out_prompt",
    )(ot, x, w_o.astype(BF16))


def _absorb_body(q_ref, w_ref, o_ref):
    o_ref[...] = jnp.dot(q_ref[...], w_ref[...], preferred_element_type=F32).astype(BF16)


def _absorb(q, w_uk, rope):
    m = q.shape[0]
    rank, nh, nope = w_uk.shape
    width = rank + LANES
    w = jnp.zeros((nh, LANES, width), F32)
    w = w.at[:, :nope, :rank].set(jnp.transpose(w_uk, (1, 2, 0)))
    eye = jnp.broadcast_to(jnp.eye(rope, dtype=F32), (nh, rope, rope))
    w = w.at[:, nope:nope + rope, rank:rank + rope].set(eye)
    w = w.at[:, nope + rope:nope + 2 * rope, rank:rank + rope].set(eye)
    out = pl.pallas_call(
        _absorb_body,
        grid=(nh,),
        in_specs=[pl.BlockSpec((m, LANES), lambda h: (0, h)),
                  pl.BlockSpec((None, LANES, width), lambda h: (h, 0, 0))],
        out_specs=pl.BlockSpec((m, width), lambda h: (0, h)),
        out_shape=jax.ShapeDtypeStruct((m, nh * width), BF16),
        compiler_params=_cparams(("parallel",)),
        name="absorb_queries",
    )(q, w.astype(BF16))
    return out.reshape(m, nh, width)


CHUNK_AHEAD = 2


def _attn_sample_body(pt_ref, q_ref, cn_ref, pn_ref, ckv_hbm, kpe_hbm, o_ref, cbuf, pbuf, sem,
                      *, npg, nch, page, rank, rope):
    b = pl.program_id(0)
    total = pl.num_programs(0) * nch

    def page_copies(chunk, slot):
        out = []
        for i in range(npg):
            pid = pt_ref[chunk * npg + i]
            out.append(pltpu.make_async_copy(ckv_hbm.at[pid], cbuf.at[slot, pl.ds(i * page, page)],
                                             sem.at[0, slot]))
            out.append(pltpu.make_async_copy(kpe_hbm.at[pid], pbuf.at[slot, :, pl.ds(i * page, page)],
                                             sem.at[1, slot]))
        return out

    @pl.when(b == 0)
    def _():
        for c in range(CHUNK_AHEAD):
            for cp in page_copies(jnp.minimum(c, total - 1), c % nch):
                cp.start()

    q_lat = q_ref[0, :, :rank]
    q_pe = q_ref[0, :, rank:rank + rope]
    nh = q_lat.shape[0]
    m_i = jnp.full((nh, 1), NEG_BIG, F32)
    l_i = jnp.zeros((nh, 1), F32)
    half = npg * page // 2

    def add_pv(accs, alpha, p, c):
        return tuple(alpha * accs[i] + jnp.dot(p[:, half * i:half * (i + 1)], c[half * i:half * (i + 1)],
                                               preferred_element_type=F32) for i in range(2))

    accs = (jnp.zeros((nh, rank), F32), jnp.zeros((nh, rank), F32))
    pending = None
    for j in range(nch):
        for cp in page_copies(b * nch + j, j):
            cp.wait()
        c = cbuf[j].astype(BF16)
        s = (lax.dot_general(q_lat, c, (((1,), (1,)), ((), ())), preferred_element_type=F32)
             + jnp.dot(q_pe, pbuf[j].astype(BF16), preferred_element_type=F32))
        if pending is not None:
            accs = add_pv(accs, *pending)
        m_new = jnp.maximum(m_i, jnp.max(s, axis=1, keepdims=True))
        alpha = jnp.exp2(m_i - m_new)
        p = jnp.exp2(s - m_new)
        l_i = alpha * l_i + jnp.sum(p, axis=1, keepdims=True)
        m_i = m_new
        pending = (alpha, p.astype(BF16), c)
        nxt = b * nch + j + CHUNK_AHEAD
        for cp in page_copies(jnp.minimum(nxt, total - 1), (j + CHUNK_AHEAD) % nch):
            cp.start()
    accs = add_pv(accs, *pending)

    @pl.when(b == pl.num_programs(0) - 1)
    def _():
        for j in range(nch - CHUNK_AHEAD, nch):
            for cp in page_copies(total - 1, (j + CHUNK_AHEAD) % nch):
                cp.wait()

    c_new = cn_ref[0].astype(BF16).astype(F32)
    p_new = pn_ref[0].astype(BF16).astype(F32)
    s_new = (jnp.sum(q_lat.astype(F32) * c_new, axis=1, keepdims=True)
             + jnp.sum(q_pe.astype(F32) * p_new, axis=1, keepdims=True))
    m_fin = jnp.maximum(m_i, s_new)
    a_old = jnp.exp2(m_i - m_fin)
    p_n = jnp.exp2(s_new - m_fin)
    l_fin = a_old * l_i + p_n
    o_new = p_n.astype(BF16).astype(F32) * c_new
    o_ref[0] = (a_old * accs[0] + (a_old * accs[1] + o_new)) / l_fin


def _attn_sample(qabs, cache_ckv, cache_kpe, page_table, c_new, kpe_new, *, npg_pref=32):
    m, nh, width = qabs.shape
    _, page, rank = cache_ckv.shape
    rope = cache_kpe.shape[2]
    n_pages = page_table.shape[1]
    npg = npg_pref if n_pages % npg_pref == 0 else n_pages
    nch = n_pages // npg
    assert nch > CHUNK_AHEAD, "a slot must not be refilled while its chunk is in use"
    kpe_t = jnp.swapaxes(cache_kpe, 1, 2)
    grid_spec = pltpu.PrefetchScalarGridSpec(
        num_scalar_prefetch=1,
        grid=(m,),
        in_specs=[
            pl.BlockSpec((1, nh, width), lambda b, pt: (b, 0, 0)),
            pl.BlockSpec((1, 1, rank), lambda b, pt: (b, 0, 0)),
            pl.BlockSpec((1, 1, rope), lambda b, pt: (b, 0, 0)),
            pl.BlockSpec(memory_space=pl.ANY),
            pl.BlockSpec(memory_space=pl.ANY),
        ],
        out_specs=pl.BlockSpec((1, nh, rank), lambda b, pt: (b, 0, 0)),
        scratch_shapes=[
            pltpu.VMEM((nch, npg * page, rank), F32),
            pltpu.VMEM((nch, rope, npg * page), F32),
            pltpu.SemaphoreType.DMA((2, nch)),
        ],
    )
    return pl.pallas_call(
        functools.partial(_attn_sample_body, npg=npg, nch=nch, page=page, rank=rank, rope=rope),
        grid_spec=grid_spec,
        out_shape=jax.ShapeDtypeStruct((m, nh, rank), F32),
        compiler_params=_cparams(("arbitrary",)),
        name="attn_sample",
    )(page_table.reshape(-1), qabs, c_new.reshape(m, 1, rank), kpe_new.reshape(m, 1, rope),
      cache_ckv, kpe_t)


def _oproj_lat_body(ol_ref, x_ref, wuv_ref, wo_ref, o_ref, *, nh, rank, hv):
    acc = x_ref[...]
    for h in range(nh):
        oh = jnp.dot(ol_ref[:, rank * h:rank * (h + 1)].astype(BF16), wuv_ref[h], preferred_element_type=F32)
        acc = acc + jnp.dot(oh.astype(BF16), wo_ref[hv * h:hv * (h + 1), :], preferred_element_type=F32)
    o_ref[...] = acc


def _oproj_lat(o_lat, x, w_uv, w_o):
    m, nh, rank = o_lat.shape
    hv = w_uv.shape[2]
    return pl.pallas_call(
        functools.partial(_oproj_lat_body, nh=nh, rank=rank, hv=hv),
        out_shape=jax.ShapeDtypeStruct(x.shape, F32),
        compiler_params=pltpu.CompilerParams(vmem_limit_bytes=VMEM_LIMIT),
        name="attn_out_sample",
    )(o_lat.reshape(m, nh * rank), x, jnp.transpose(w_uv, (1, 0, 2)).astype(BF16), w_o.astype(BF16))


def kernel(x_prompt, x_sample, state_s5_re, state_s5_im, cache_ckv, cache_kpe, page_table, ffn_norm, ffn_w_gate, ffn_w_up, ffn_w_down, mix_norm, s5_a_re, s5_a_im, s5_log_dt, s5_b_re, s5_b_im, s5_c_re, s5_c_im, s5_d, s5_w_glu_v, s5_w_glu_g, mla_w_dq, mla_q_norm, mla_w_uq, mla_qn_norm, mla_qr_norm, mla_w_o, kv_in_norm, kv_w_dkv, kv_c_norm, kv_kpe_norm, kv_w_uk, kv_w_uv):
    bp, tp, d = x_prompt.shape
    bs, ts, _ = x_sample.shape
    assert ts == 1, "the sample group decodes one token per sequence"
    depth = ffn_norm.shape[0]
    n_a = s5_a_re.shape[0]
    assert depth == 2 and n_a == 1 and mla_w_dq.shape[0] == 1, "one S5 layer followed by one MLA layer"
    g, p = s5_a_re.shape[1], s5_a_re.shape[2]
    nh, hv = kv_w_uv.shape[1], kv_w_uv.shape[2]
    rope = kv_kpe_norm.shape[0]
    past_len = page_table.shape[1] * cache_ckv.shape[1]

    wg, wu, wd = ffn_w_gate.astype(BF16), ffn_w_up.astype(BF16), ffn_w_down.astype(BF16)

    def ffn(x, layer, idx):
        return _ffn(x, ffn_norm[layer, idx], wg, wu, wd, layer, idx)

    xp = x_prompt.reshape(bp * tp, d)
    xs = x_sample.reshape(bs, d)

    lam_re, lam_im, wb, wc = _s5_params(s5_a_re[0], s5_a_im[0], s5_log_dt[0], s5_b_re[0], s5_b_im[0],
                                        s5_c_re[0], s5_c_im[0])
    wv, wgl = s5_w_glu_v[0].astype(BF16), s5_w_glu_g[0].astype(BF16)
    xp = ffn(xp, 0, 0)
    xs = ffn(xs, 0, 0)
    xp, sre_p, sim_p = _s5_prompt(xp.reshape(bp, tp, d), mix_norm[0], lam_re, lam_im, wb, wc, s5_d[0], wv, wgl)
    xs, sre_s, sim_s = _s5_step(xs, mix_norm[0], state_s5_re[0].reshape(bs, g * p),
                                state_s5_im[0].reshape(bs, g * p), lam_re, lam_im, wb, wc, s5_d[0], wv, wgl)
    xp = ffn(xp.reshape(bp * tp, d), 0, 1)
    xs = ffn(xs, 0, 1)

    pos_p = jnp.arange(tp, dtype=jnp.int32)
    pos_s = jnp.full((bs,), past_len, dtype=jnp.int32)
    kv_args = (kv_in_norm, kv_w_dkv, kv_c_norm, kv_kpe_norm, kv_w_uk, kv_w_uv)
    ckv_p, kpe_p, k_p, vt_p = _kv(xp.reshape(bp, tp, d), pos_p, *kv_args, with_heads=True)
    ckv_s, kpe_s = _kv(xs.reshape(1, bs, d), pos_s, *kv_args, with_heads=False)

    xp = ffn(xp, 1, 0)
    xs = ffn(xs, 1, 0)
    q_args = (mix_norm[1], mla_w_dq[0], mla_q_norm[0], mla_w_uq[0], mla_qn_norm[0], mla_qr_norm[0])
    q_p = _queries(xp.reshape(bp, tp, d), pos_p, *q_args)
    ot = _attn_prompt(q_p.reshape(bp, tp, nh * LANES), k_p.reshape(bp, tp, nh * LANES), vt_p, nh=nh, hv=hv)
    xp = _oproj_t(ot, xp.reshape(bp, tp, d), mla_w_o[0]).reshape(bp * tp, d)

    q_s = _queries(xs.reshape(1, bs, d), pos_s, *q_args)
    qabs = _absorb(q_s, kv_w_uk, rope)
    o_lat = _attn_sample(qabs, cache_ckv, cache_kpe, page_table, ckv_s, kpe_s)
    xs = _oproj_lat(o_lat, xs, kv_w_uv, mla_w_o[0])

    xp = ffn(xp, 1, 1)
    xs = ffn(xs, 1, 1)

    return (xp.reshape(bp, tp, d), xs.reshape(bs, 1, d),
            sre_p.reshape(n_a, bp, g, p), sim_p.reshape(n_a, bp, g, p),
            sre_s.reshape(n_a, bs, g, p), sim_s.reshape(n_a, bs, g, p),
            ckv_p.reshape(bp, tp, -1), kpe_p.reshape(bp, tp, rope),
            ckv_s.reshape(bs, 1, -1), kpe_s.reshape(bs, 1, rope))
```

```python
import functools
import math

import jax
import jax.numpy as jnp
from jax import lax
from jax.experimental import pallas as pl
from jax.experimental.pallas import tpu as pltpu

F32 = jnp.float32
BF16 = jnp.bfloat16

EPS = 1e-6
FFN_RES = 0.5
NEG_BIG = -1e30
ROPE_BASE = 10000.0

GROUP_CH = 16
LANES = 128
SUBLANES = 8
S5_SLAB = 256
S5_PITCH = 12
VT_PAD = 16
VMEM_LIMIT = 56 * 1024 * 1024


def _tile(n, pref):
    if n <= pref:
        return n
    for t in range(pref, 7, -1):
        if n % t == 0 and t % 8 == 0:
            return t
    return n


def _rms(x, gain):
    ms = jnp.mean(x * x, axis=-1, keepdims=True)
    return x * lax.rsqrt(ms + EPS) * gain


def _cparams(sem):
    return pltpu.CompilerParams(dimension_semantics=sem, vmem_limit_bytes=VMEM_LIMIT)


def _ffn_body(x_ref, g_ref, wg_ref, wu_ref, wd_ref, o_ref):
    x = x_ref[...]
    xn = _rms(x, g_ref[...]).astype(BF16)
    gate = jnp.dot(xn, wg_ref[...], preferred_element_type=F32)
    up = jnp.dot(xn, wu_ref[...], preferred_element_type=F32)
    hid = (gate * jax.nn.sigmoid(gate) * up).astype(BF16)
    o_ref[...] = x + FFN_RES * jnp.dot(hid, wd_ref[...], preferred_element_type=F32)


def _resident(shape, index_map):
    return pl.BlockSpec(shape, index_map, pipeline_mode=pl.Buffered(1))


def _ffn(x, gain, w_gate, w_up, w_down, layer, idx, *, tm_pref=512):
    m, d = x.shape
    ff = w_gate.shape[3]
    tm = _tile(m, tm_pref)
    pick = lambda i: (layer, idx, 0, 0)
    return pl.pallas_call(
        _ffn_body,
        grid=(m // tm,),
        in_specs=[
            pl.BlockSpec((tm, d), lambda i: (i, 0)),
            _resident((1, d), lambda i: (0, 0)),
            _resident((None, None, d, ff), pick),
            _resident((None, None, d, ff), pick),
            _resident((None, None, ff, d), pick),
        ],
        out_specs=pl.BlockSpec((tm, d), lambda i: (i, 0)),
        out_shape=jax.ShapeDtypeStruct((m, d), F32),
        compiler_params=_cparams(("parallel",)),
        name="ffn",
    )(x, gain.reshape(1, d), w_gate, w_up, w_down)


def _s5_params(a_re, a_im, log_dt, b_re, b_im, c_re, c_im):
    g, p = a_re.shape
    dt = jnp.exp(log_dt)[:, None]
    mag = jnp.exp(dt * a_re)
    lam_re, lam_im = mag * jnp.cos(dt * a_im), mag * jnp.sin(dt * a_im)
    den = a_re * a_re + a_im * a_im
    f_re = ((lam_re - 1.0) * a_re + lam_im * a_im) / den
    f_im = (lam_im * a_re - (lam_re - 1.0) * a_im) / den
    bb_re = f_re[..., None] * b_re - f_im[..., None] * b_im
    bb_im = f_re[..., None] * b_im + f_im[..., None] * b_re
    gs = S5_SLAB // GROUP_CH
    ns = g // gs
    same_group = jnp.eye(gs, dtype=bool)[None, :, None, :, None]
    nk = gs * p // LANES

    def pack_b(bb):
        bb = jnp.transpose(bb.reshape(ns, gs, p, GROUP_CH), (0, 1, 3, 2))
        w = jnp.where(same_group, bb[:, :, :, None, :], 0.0)
        return w.reshape(ns, S5_SLAB, nk, LANES)

    wb = jnp.stack([pack_b(bb_re), pack_b(bb_im)], axis=3)
    wb = wb.reshape(ns, S5_SLAB, 2 * gs * p).astype(BF16)

    def pack_c(cc):
        cc = jnp.transpose(cc.reshape(ns, gs, GROUP_CH, p), (0, 3, 1, 2))
        w = jnp.where(same_group, cc[:, None, :, :, :], 0.0)
        return w.reshape(ns, nk, LANES, S5_SLAB)

    wc = jnp.stack([pack_c(c_re), -pack_c(c_im)], axis=2)
    wc = wc.reshape(ns, 2 * gs * p, S5_SLAB).astype(BF16)
    return lam_re, lam_im, wb, wc


def _s5_prompt_body(x_ref, g_ref, wb_ref, wc_ref, lr_ref, li_ref, d_ref, wv_ref, wgl_ref,
                    o_ref, sre_ref, sim_ref, xr_ref, xi_ref, cr_ref, ci_ref, y_ref, *, tc, ns, nk):
    t_idx = pl.program_id(1)

    @pl.when(t_idx == 0)
    def _():
        cr_ref[...] = jnp.zeros_like(cr_ref)
        ci_ref[...] = jnp.zeros_like(ci_ref)

    x = x_ref[0]
    h = _rms(x, g_ref[...])
    hb = h.astype(BF16)
    for s in range(ns):
        us = hb[:, S5_SLAB * s:S5_SLAB * (s + 1)]
        for k in range(nk):
            xk = jnp.dot(us, wb_ref[s, :, 2 * LANES * k:2 * LANES * (k + 1)],
                         preferred_element_type=F32)
            xr_ref[s, pl.ds(k, tc, stride=S5_PITCH), :] = xk[:, :LANES]
            xi_ref[s, pl.ds(k, tc, stride=S5_PITCH), :] = xk[:, LANES:]

    lam = [(lr_ref[s], li_ref[s]) for s in range(ns)]

    def step(t, carry):
        row = pl.multiple_of(t * S5_PITCH, math.gcd(S5_PITCH, SUBLANES))
        out = []
        for s in range(ns):
            sr, si = carry[s]
            lr, li = lam[s]
            nr = lr * sr - li * si + xr_ref[s, pl.ds(row, nk), :]
            ni = lr * si + li * sr + xi_ref[s, pl.ds(row, nk), :]
            xr_ref[s, pl.ds(row, nk), :] = nr
            xi_ref[s, pl.ds(row, nk), :] = ni
            out.append((nr, ni))
        return tuple(out)

    init = tuple((cr_ref[s], ci_ref[s]) for s in range(ns))
    fin = lax.fori_loop(0, tc, step, init, unroll=8)
    for s in range(ns):
        cr_ref[s] = fin[s][0]
        ci_ref[s] = fin[s][1]

    for s in range(ns):
        acc = None
        for k in range(nk):
            sk = jnp.concatenate([xr_ref[s, pl.ds(k, tc, stride=S5_PITCH), :],
                                  xi_ref[s, pl.ds(k, tc, stride=S5_PITCH), :]], axis=1).astype(BF16)
            part = jnp.dot(sk, wc_ref[s, 2 * LANES * k:2 * LANES * (k + 1), :],
                           preferred_element_type=F32)
            acc = part if acc is None else acc + part
        y_ref[:, S5_SLAB * s:S5_SLAB * (s + 1)] = acc

    y = jax.nn.gelu(y_ref[...] + d_ref[...] * h, approximate=True).astype(BF16)
    val = jnp.dot(y, wv_ref[...], preferred_element_type=F32)
    gate = jnp.dot(y, wgl_ref[...], preferred_element_type=F32)
    o_ref[0] = x + val * jax.nn.sigmoid(gate)

    @pl.when(t_idx == pl.num_programs(1) - 1)
    def _():
        sre_ref[0] = cr_ref[...].reshape(ns * nk, LANES)
        sim_ref[0] = ci_ref[...].reshape(ns * nk, LANES)


def _s5_prompt(x, gain, lam_re, lam_im, wb, wc, d_skip, w_v, w_g, *, tc_pref=512):
    b, t, d = x.shape
    ns = wb.shape[0]
    nk = wb.shape[2] // (2 * LANES)
    assert nk == SUBLANES, "one token's slab state must fill whole vregs"
    tc = _tile(t, tc_pref)
    lr = lam_re.reshape(ns, nk, LANES)
    li = lam_im.reshape(ns, nk, LANES)
    const3 = lambda i, j: (0, 0, 0)
    const2 = lambda i, j: (0, 0)
    out, s_re, s_im = pl.pallas_call(
        functools.partial(_s5_prompt_body, tc=tc, ns=ns, nk=nk),
        grid=(b, t // tc),
        in_specs=[
            pl.BlockSpec((1, tc, d), lambda i, j: (i, j, 0)),
            _resident((1, d), const2),
            _resident(wb.shape, const3),
            _resident(wc.shape, const3),
            _resident(lr.shape, const3),
            _resident(li.shape, const3),
            _resident((1, d), const2),
            _resident(w_v.shape, const2),
            _resident(w_g.shape, const2),
        ],
        out_specs=[
            pl.BlockSpec((1, tc, d), lambda i, j: (i, j, 0)),
            pl.BlockSpec((1, ns * nk, LANES), lambda i, j: (i, 0, 0)),
            pl.BlockSpec((1, ns * nk, LANES), lambda i, j: (i, 0, 0)),
        ],
        out_shape=[
            jax.ShapeDtypeStruct((b, t, d), F32),
            jax.ShapeDtypeStruct((b, ns * nk, LANES), F32),
            jax.ShapeDtypeStruct((b, ns * nk, LANES), F32),
        ],
        scratch_shapes=[
            pltpu.VMEM((ns, tc * S5_PITCH, LANES), F32),
            pltpu.VMEM((ns, tc * S5_PITCH, LANES), F32),
            pltpu.VMEM((ns, nk, LANES), F32),
            pltpu.VMEM((ns, nk, LANES), F32),
            pltpu.VMEM((tc, d), F32),
        ],
        compiler_params=_cparams(("parallel", "arbitrary")),
        name="s5_prompt",
    )(x, gain.reshape(1, d), wb, wc, lr, li, d_skip.reshape(1, d), w_v, w_g)
    return out, s_re, s_im


def _s5_step_body(x_ref, g_ref, s0r_ref, s0i_ref, wb_ref, wc_ref, lr_ref, li_ref, d_ref,
                  wv_ref, wgl_ref, o_ref, sre_ref, sim_ref, y_ref, *, ns, nk):
    x = x_ref[...]
    h = _rms(x, g_ref[...])
    hb = h.astype(BF16)
    for s in range(ns):
        xs = jnp.dot(hb[:, S5_SLAB * s:S5_SLAB * (s + 1)], wb_ref[s], preferred_element_type=F32)
        acc = None
        for k in range(nk):
            col = (s * nk + k) * LANES
            lr = lr_ref[:, col:col + LANES]
            li = li_ref[:, col:col + LANES]
            s0r = s0r_ref[:, col:col + LANES]
            s0i = s0i_ref[:, col:col + LANES]
            nr = lr * s0r - li * s0i + xs[:, 2 * LANES * k:2 * LANES * k + LANES]
            ni = lr * s0i + li * s0r + xs[:, 2 * LANES * k + LANES:2 * LANES * (k + 1)]
            sre_ref[:, col:col + LANES] = nr
            sim_ref[:, col:col + LANES] = ni
            part = (jnp.dot(nr.astype(BF16), wc_ref[s, 2 * LANES * k:2 * LANES * k + LANES, :],
                            preferred_element_type=F32)
                    + jnp.dot(ni.astype(BF16), wc_ref[s, 2 * LANES * k + LANES:2 * LANES * (k + 1), :],
                              preferred_element_type=F32))
            acc = part if acc is None else acc + part
        y_ref[:, S5_SLAB * s:S5_SLAB * (s + 1)] = acc
    y = jax.nn.gelu(y_ref[...] + d_ref[...] * h, approximate=True).astype(BF16)
    val = jnp.dot(y, wv_ref[...], preferred_element_type=F32)
    gate = jnp.dot(y, wgl_ref[...], preferred_element_type=F32)
    o_ref[...] = x + val * jax.nn.sigmoid(gate)


def _s5_step(x, gain, s0_re, s0_im, lam_re, lam_im, wb, wc, d_skip, w_v, w_g):
    m, d = x.shape
    ns = wb.shape[0]
    nk = wb.shape[2] // (2 * LANES)
    nstate = s0_re.shape[1]
    args = (x, gain.reshape(1, d), s0_re, s0_im, wb, wc, lam_re.reshape(1, nstate),
            lam_im.reshape(1, nstate), d_skip.reshape(1, d), w_v, w_g)
    return pl.pallas_call(
        functools.partial(_s5_step_body, ns=ns, nk=nk),
        out_shape=[
            jax.ShapeDtypeStruct((m, d), F32),
            jax.ShapeDtypeStruct((m, nstate), F32),
            jax.ShapeDtypeStruct((m, nstate), F32),
        ],
        scratch_shapes=[pltpu.VMEM((m, d), F32)],
        compiler_params=pltpu.CompilerParams(vmem_limit_bytes=VMEM_LIMIT),
        name="s5_step",
    )(*args)


def _rope_tables(pos, half):
    inv = ROPE_BASE ** (-jnp.arange(half, dtype=F32) / half)
    ang = pos.astype(F32)[:, None] * inv[None, :]
    return jnp.cos(ang), jnp.sin(ang)


def _kv_body(x_ref, g_ref, w_ref, cn_ref, cos_ref, sin_ref, *rest, rank, rope, with_heads, nh):
    if with_heads:
        wuk_ref, wuvt_ref, c_ref, kpe_ref, k_ref, vt_ref = rest
    else:
        c_ref, kpe_ref = rest
    hk = _rms(x_ref[...], g_ref[...]).astype(BF16)
    ck = jnp.dot(hk, w_ref[...], preferred_element_type=F32)
    c = _rms(ck[:, :rank], cn_ref[...])
    c_ref[...] = c

    def pe_block(i, copies):
        pe = ck[:, rank + 2 * LANES * i:rank + 2 * LANES * i + LANES]
        rot = ck[:, rank + 2 * LANES * i + LANES:rank + 2 * LANES * (i + 1)]
        r = lax.rsqrt(jnp.sum(pe * pe, axis=-1, keepdims=True) * (1.0 / (rope * copies)) + EPS)
        return (pe * cos_ref[:, LANES * i:LANES * (i + 1)] + rot * sin_ref[:, LANES * i:LANES * (i + 1)]) * r

    kpe_ref[...] = pe_block(0, 1)[:, :rope]
    if with_heads:
        cb = c.astype(BF16)
        pe_mid = pe_block(1, 2)
        kn = jnp.dot(cb, wuk_ref[...], preferred_element_type=F32)
        for h in range(nh):
            k_ref[:, LANES * h:LANES * (h + 1)] = (kn[:, LANES * h:LANES * (h + 1)] + pe_mid).astype(BF16)
        vt = lax.dot_general(wuvt_ref[...], cb, (((1,), (1,)), ((), ())), preferred_element_type=F32)
        hv = vt.shape[0] // nh
        hvx = hv + VT_PAD
        ones_row = (lax.broadcasted_iota(jnp.int32, (VT_PAD, vt.shape[1]), 0) == 0).astype(BF16)
        for h in range(nh):
            vt_ref[hvx * h:hvx * h + hv, :] = vt[hv * h:hv * (h + 1)].astype(BF16)
            vt_ref[hvx * h + hv:hvx * (h + 1), :] = ones_row


def _kv(x, pos, kv_in_norm, w_dkv, c_norm, kpe_norm, w_uk, w_uv, *, with_heads, tm_pref=512):
    b, t, d = x.shape
    rank = c_norm.shape[0]
    rope = kpe_norm.shape[0]
    half = rope // 2
    nh, nope = w_uk.shape[1], w_uk.shape[2]
    hv = w_uv.shape[2]
    tm = _tile(t, tm_pref)
    nt = t // tm
    m = b * t
    assert nope + 2 * rope == LANES
    w_pe = w_dkv[:, rank:]
    w_rot = jnp.concatenate([-w_pe[:, half:], w_pe[:, :half]], axis=1)
    zero = lambda n: jnp.zeros((d, n), F32)
    w_ext = jnp.concatenate([
        w_dkv[:, :rank],
        w_pe, zero(LANES - rope), w_rot, zero(LANES - rope),
        zero(nope), w_pe, w_pe, zero(nope), w_rot, w_rot,
    ], axis=1).astype(BF16)
    cos, sin = _rope_tables(pos, half)
    g_rot = jnp.concatenate([kpe_norm[half:], kpe_norm[:half]])
    gcos = jnp.concatenate([cos, cos], 1) * kpe_norm[None, :]
    gsin = jnp.concatenate([sin, sin], 1) * g_rot[None, :]
    zt = lambda n: jnp.zeros((t, n), F32)
    cos_t = jnp.concatenate([gcos, zt(LANES - rope), zt(nope), gcos, gcos], axis=1)
    sin_t = jnp.concatenate([gsin, zt(LANES - rope), zt(nope), gsin, gsin], axis=1)

    in_specs = [
        pl.BlockSpec((tm, d), lambda i: (i, 0)),
        pl.BlockSpec((1, d), lambda i: (0, 0)),
        pl.BlockSpec(w_ext.shape, lambda i: (0, 0)),
        pl.BlockSpec((1, rank), lambda i: (0, 0)),
        pl.BlockSpec((tm, 2 * LANES), lambda i: (i % nt, 0)),
        pl.BlockSpec((tm, 2 * LANES), lambda i: (i % nt, 0)),
    ]
    args = [x.reshape(m, d), kv_in_norm.reshape(1, d), w_ext, c_norm.reshape(1, rank), cos_t, sin_t]
    out_specs = [pl.BlockSpec((tm, rank), lambda i: (i, 0)), pl.BlockSpec((tm, rope), lambda i: (i, 0))]
    out_shape = [jax.ShapeDtypeStruct((m, rank), F32), jax.ShapeDtypeStruct((m, rope), F32)]
    if with_heads:
        wuk = jnp.concatenate([w_uk, jnp.zeros((rank, nh, LANES - nope), F32)], axis=2)
        wuk = wuk.reshape(rank, nh * LANES).astype(BF16)
        wuvt = jnp.transpose(w_uv, (1, 2, 0)).reshape(nh * hv, rank).astype(BF16)
        in_specs += [pl.BlockSpec(wuk.shape, lambda i: (0, 0)), pl.BlockSpec(wuvt.shape, lambda i: (0, 0))]
        args += [wuk, wuvt]
        out_specs += [pl.BlockSpec((tm, nh * LANES), lambda i: (i, 0)),
                      pl.BlockSpec((None, nh * (hv + VT_PAD), tm), lambda i: (i // nt, 0, i % nt))]
        out_shape += [jax.ShapeDtypeStruct((m, nh * LANES), BF16),
                      jax.ShapeDtypeStruct((b, nh * (hv + VT_PAD), t), BF16)]
    return pl.pallas_call(
        functools.partial(_kv_body, rank=rank, rope=rope, with_heads=with_heads, nh=nh),
        grid=(m // tm,),
        in_specs=in_specs,
        out_specs=out_specs,
        out_shape=out_shape,
        compiler_params=_cparams(("parallel",)),
        name="shared_kv",
    )(*args)


def _q_body(x_ref, g_ref, wdq_ref, qn_ref, w1t_ref, tabt_ref, qt_ref, *, nh, nope, rope):
    hb = _rms(x_ref[...], g_ref[...]).astype(BF16)
    cq = jnp.dot(hb, wdq_ref[...], preferred_element_type=F32)
    cqb = _rms(cq, qn_ref[...]).astype(BF16)
    qm = lax.dot_general(w1t_ref[...], cqb, (((1,), (1,)), ((), ())), preferred_element_type=F32)
    tab_n, tab_p = tabt_ref[:nope, :], tabt_ref[nope:, :]
    for h in range(nh):
        q_n = qm[LANES * h:LANES * h + nope, :]
        q_p = qm[LANES * h + nope:LANES * (h + 1), :]
        r_n = lax.rsqrt(jnp.mean(q_n * q_n, axis=0, keepdims=True) + EPS)
        q_pe = q_p[:rope]
        r_p = lax.rsqrt(jnp.mean(q_pe * q_pe, axis=0, keepdims=True) + EPS)
        qt_ref[LANES * h:LANES * h + nope, :] = (q_n * tab_n * r_n).astype(BF16)
        qt_ref[LANES * h + nope:LANES * (h + 1), :] = (q_p * tab_p * r_p).astype(BF16)


def _queries(x, pos, mix_norm, w_dq, q_norm, w_uq, qn_norm, qr_norm, *, tm_pref=512):
    b, t, d = x.shape
    qrank = w_dq.shape[1]
    nh = w_uq.shape[1]
    nope, rope = qn_norm.shape[0], qr_norm.shape[0]
    assert nope + 2 * rope == LANES
    half = rope // 2
    scale = (nope + rope) ** -0.5 * math.log2(math.e)
    tm = _tile(t, tm_pref)
    nt = t // tm
    m = b * t
    w_n, w_p = w_uq[:, :, :nope], w_uq[:, :, nope:]
    w_rot = jnp.concatenate([-w_p[:, :, half:], w_p[:, :, :half]], axis=2)
    w1t = jnp.concatenate([w_n, w_p, w_rot], axis=2).reshape(qrank, nh * LANES).T.astype(BF16)
    g_rot = jnp.concatenate([qr_norm[half:], qr_norm[:half]])
    cos, sin = _rope_tables(pos, half)
    tabt = jnp.concatenate([
        jnp.broadcast_to(qn_norm[:, None], (nope, t)),
        jnp.concatenate([cos, cos], 1).T * qr_norm[:, None],
        jnp.concatenate([sin, sin], 1).T * g_rot[:, None]], axis=0) * scale
    return pl.pallas_call(
        functools.partial(_q_body, nh=nh, nope=nope, rope=rope),
        grid=(m // tm,),
        in_specs=[
            pl.BlockSpec((tm, d), lambda i: (i, 0)),
            pl.BlockSpec((1, d), lambda i: (0, 0)),
            pl.BlockSpec(w_dq.shape, lambda i: (0, 0)),
            pl.BlockSpec((1, qrank), lambda i: (0, 0)),
            pl.BlockSpec(w1t.shape, lambda i: (0, 0)),
            pl.BlockSpec((LANES, tm), lambda i: (0, i % nt)),
        ],
        out_specs=pl.BlockSpec((nh * LANES, tm), lambda i: (0, i)),
        out_shape=jax.ShapeDtypeStruct((nh * LANES, m), BF16),
        compiler_params=_cparams(("parallel",)),
        name="mla_queries",
    )(x.reshape(m, d), mix_norm.reshape(1, d), w_dq.astype(BF16), q_norm.reshape(1, qrank), w1t, tabt)


def _attn_prompt_body(q_ref, k_ref, vt_ref, o_ref, *, tq, hp, hv):
    qi = pl.program_id(2)
    hvx = hv + VT_PAD
    qs = [q_ref[LANES * a:LANES * (a + 1), :] for a in range(hp)]

    def block(ki, carry, masked):
        off = pl.multiple_of(ki * tq, tq)
        ss = []
        for a in range(hp):
            k = k_ref[pl.ds(off, tq), LANES * a:LANES * (a + 1)]
            ss.append(jnp.dot(k, qs[a], preferred_element_type=F32))
        ps, stats = [], []
        for a in range(hp):
            m_i = carry[a][0]
            s = ss[a]
            if masked:
                kpos = lax.broadcasted_iota(jnp.int32, (tq, tq), 0)
                qpos = lax.broadcasted_iota(jnp.int32, (tq, tq), 1)
                s = jnp.where(kpos <= qpos, s, NEG_BIG)
            m_new = jnp.maximum(m_i, jnp.max(s, axis=0, keepdims=True))
            stats.append((m_new, jnp.exp2(m_i - m_new)))
            ps.append(jnp.exp2(s - m_new).astype(BF16))
        out = []
        for a in range(hp):
            m_new, alpha = stats[a]
            pv = jnp.dot(vt_ref[hvx * a:hvx * (a + 1), pl.ds(off, tq)], ps[a], preferred_element_type=F32)
            out.append((m_new, alpha * carry[a][1] + pv))
        return tuple(out)

    init = tuple((jnp.full((1, tq), NEG_BIG, F32), jnp.zeros((hvx, tq), F32)) for _ in range(hp))
    carry = lax.fori_loop(0, qi, lambda ki, c: block(ki, c, False), init)
    fin = block(qi, carry, True)
    for a in range(hp):
        acc = fin[a][1]
        o_ref[hv * a:hv * (a + 1), :] = (acc[:hv] / acc[hv:hv + 1]).astype(BF16)


def _attn_prompt(qt, k, vt, *, nh, hv, tq_pref=512, hp=4):
    b, t, _ = k.shape
    tq = _tile(t, tq_pref)
    nq = t // tq
    assert nh % hp == 0
    return pl.pallas_call(
        functools.partial(_attn_prompt_body, tq=tq, hp=hp, hv=hv),
        grid=(b, nh // hp, t // tq),
        in_specs=[
            pl.BlockSpec((hp * LANES, tq), lambda i, h, j: (h, i * nq + j)),
            pl.BlockSpec((None, t, hp * LANES), lambda i, h, j: (i, 0, h)),
            pl.BlockSpec((None, hp * (hv + VT_PAD), t), lambda i, h, j: (i, h, 0)),
        ],
        out_specs=pl.BlockSpec((None, hp * hv, tq), lambda i, h, j: (i, h, j)),
        out_shape=jax.ShapeDtypeStruct((b, nh * hv, t), BF16),
        compiler_params=_cparams(("parallel", "parallel", "arbitrary")),
        name="attn_prompt",
    )(qt, k, vt)


def _oproj_t_body(ot_ref, x_ref, wo_ref, o_ref):
    m = lax.dot_general(ot_ref[...], wo_ref[...], (((0,), (0,)), ((), ())), preferred_element_type=F32)
    o_ref[...] = x_ref[...] + m


def _oproj_t(ot, x, w_o, *, tm_pref=512):
    b, t, d = x.shape
    e = ot.shape[1]
    tm = _tile(t, tm_pref)
    return pl.pallas_call(
        _oproj_t_body,
        grid=(b, t // tm),
        in_specs=[
            pl.BlockSpec((None, e, tm), lambda i, j: (i, 0, j)),
            pl.BlockSpec((None, tm, d), lambda i, j: (i, j, 0)),
            pl.BlockSpec((e, d), lambda i, j: (0, 0)),
        ],
        out_specs=pl.BlockSpec((None, tm, d), lambda i, j: (i, j, 0)),
        out_shape=jax.ShapeDtypeStruct((b, t, d), F32),
        compiler_params=_cparams(("parallel", "parallel")),
        name="attn_out_prompt",
    )(ot, x, w_o.astype(BF16))


def _absorb_body(qt_ref, w_ref, o_ref):
    o_ref[...] = lax.dot_general(qt_ref[...], w_ref[...], (((0,), (0,)), ((), ())),
                                 preferred_element_type=F32).astype(BF16)


def _absorb(qt, w_uk, rope):
    m = qt.shape[1]
    rank, nh, nope = w_uk.shape
    width = rank + LANES
    w = jnp.zeros((nh, LANES, width), F32)
    w = w.at[:, :nope, :rank].set(jnp.transpose(w_uk, (1, 2, 0)))
    eye = jnp.broadcast_to(jnp.eye(rope, dtype=F32), (nh, rope, rope))
    w = w.at[:, nope:nope + rope, rank:rank + rope].set(eye)
    w = w.at[:, nope + rope:nope + 2 * rope, rank:rank + rope].set(eye)
    out = pl.pallas_call(
        _absorb_body,
        grid=(nh,),
        in_specs=[pl.BlockSpec((LANES, m), lambda h: (h, 0)),
                  pl.BlockSpec((None, LANES, width), lambda h: (h, 0, 0))],
        out_specs=pl.BlockSpec((m, width), lambda h: (0, h)),
        out_shape=jax.ShapeDtypeStruct((m, nh * width), BF16),
        compiler_params=_cparams(("parallel",)),
        name="absorb_queries",
    )(qt, w.astype(BF16))
    return out.reshape(m, nh, width)


CHUNK_AHEAD = 3


def _attn_sample_body(pt_ref, q_ref, cn_ref, pn_ref, ckv_hbm, kpe_hbm, o_ref, cbuf, pbuf, sem,
                      *, npg, nch, page, rank, rope):
    b = pl.program_id(0)
    total = pl.num_programs(0) * nch

    def page_copies(chunk, slot):
        out = []
        for i in range(npg):
            pid = pt_ref[chunk * npg + i]
            out.append(pltpu.make_async_copy(ckv_hbm.at[pid], cbuf.at[slot, pl.ds(i * page, page)],
                                             sem.at[0, slot]))
            out.append(pltpu.make_async_copy(kpe_hbm.at[pid], pbuf.at[slot, :, pl.ds(i * page, page)],
                                             sem.at[1, slot]))
        return out

    @pl.when(b == 0)
    def _():
        for c in range(CHUNK_AHEAD):
            for cp in page_copies(jnp.minimum(c, total - 1), c % nch):
                cp.start()

    q_lat = q_ref[0, :, :rank]
    q_pe = q_ref[0, :, rank:rank + rope]
    nh = q_lat.shape[0]
    m_i = jnp.full((nh, 1), NEG_BIG, F32)
    l_i = jnp.zeros((nh, 1), F32)
    half = npg * page // 2

    def add_pv(accs, alpha, p, c):
        return tuple(alpha * accs[i] + jnp.dot(p[:, half * i:half * (i + 1)], c[half * i:half * (i + 1)],
                                               preferred_element_type=F32) for i in range(2))

    accs = (jnp.zeros((nh, rank), F32), jnp.zeros((nh, rank), F32))
    pending = None
    for j in range(nch):
        for cp in page_copies(b * nch + j, j):
            cp.wait()
        c = cbuf[j].astype(BF16)
        s = (lax.dot_general(q_lat, c, (((1,), (1,)), ((), ())), preferred_element_type=F32)
             + jnp.dot(q_pe, pbuf[j].astype(BF16), preferred_element_type=F32))
        if pending is not None:
            accs = add_pv(accs, *pending)
        m_new = jnp.maximum(m_i, jnp.max(s, axis=1, keepdims=True))
        alpha = jnp.exp2(m_i - m_new)
        p = jnp.exp2(s - m_new)
        l_i = alpha * l_i + jnp.sum(p, axis=1, keepdims=True)
        m_i = m_new
        pending = (alpha, p.astype(BF16), c)
        nxt = b * nch + j + CHUNK_AHEAD
        for cp in page_copies(jnp.minimum(nxt, total - 1), (j + CHUNK_AHEAD) % nch):
            cp.start()
    accs = add_pv(accs, *pending)

    @pl.when(b == pl.num_programs(0) - 1)
    def _():
        for j in range(nch - CHUNK_AHEAD, nch):
            for cp in page_copies(total - 1, (j + CHUNK_AHEAD) % nch):
                cp.wait()

    c_new = cn_ref[0].astype(BF16).astype(F32)
    p_new = pn_ref[0].astype(BF16).astype(F32)
    s_new = (jnp.sum(q_lat.astype(F32) * c_new, axis=1, keepdims=True)
             + jnp.sum(q_pe.astype(F32) * p_new, axis=1, keepdims=True))
    m_fin = jnp.maximum(m_i, s_new)
    a_old = jnp.exp2(m_i - m_fin)
    p_n = jnp.exp2(s_new - m_fin)
    l_fin = a_old * l_i + p_n
    o_new = p_n.astype(BF16).astype(F32) * c_new
    o_ref[0] = (a_old * accs[0] + (a_old * accs[1] + o_new)) / l_fin


def _attn_sample(qabs, cache_ckv, cache_kpe, page_table, c_new, kpe_new, *, npg_pref=32):
    m, nh, width = qabs.shape
    _, page, rank = cache_ckv.shape
    rope = cache_kpe.shape[2]
    n_pages = page_table.shape[1]
    npg = npg_pref if n_pages % npg_pref == 0 else n_pages
    nch = n_pages // npg
    assert nch > CHUNK_AHEAD, "a slot must not be refilled while its chunk is in use"
    kpe_t = jnp.swapaxes(cache_kpe, 1, 2)
    grid_spec = pltpu.PrefetchScalarGridSpec(
        num_scalar_prefetch=1,
        grid=(m,),
        in_specs=[
            pl.BlockSpec((1, nh, width), lambda b, pt: (b, 0, 0)),
            pl.BlockSpec((1, 1, rank), lambda b, pt: (b, 0, 0)),
            pl.BlockSpec((1, 1, rope), lambda b, pt: (b, 0, 0)),
            pl.BlockSpec(memory_space=pl.ANY),
            pl.BlockSpec(memory_space=pl.ANY),
        ],
        out_specs=pl.BlockSpec((1, nh, rank), lambda b, pt: (b, 0, 0)),
        scratch_shapes=[
            pltpu.VMEM((nch, npg * page, rank), F32),
            pltpu.VMEM((nch, rope, npg * page), F32),
            pltpu.SemaphoreType.DMA((2, nch)),
        ],
    )
    return pl.pallas_call(
        functools.partial(_attn_sample_body, npg=npg, nch=nch, page=page, rank=rank, rope=rope),
        grid_spec=grid_spec,
        out_shape=jax.ShapeDtypeStruct((m, nh, rank), F32),
        compiler_params=_cparams(("arbitrary",)),
        name="attn_sample",
    )(page_table.reshape(-1), qabs, c_new.reshape(m, 1, rank), kpe_new.reshape(m, 1, rope),
      cache_ckv, kpe_t)


def _oproj_lat_body(ol_ref, x_ref, wuv_ref, wo_ref, o_ref, *, nh, rank, hv):
    acc = x_ref[...]
    for h in range(nh):
        oh = jnp.dot(ol_ref[:, rank * h:rank * (h + 1)].astype(BF16), wuv_ref[h], preferred_element_type=F32)
        acc = acc + jnp.dot(oh.astype(BF16), wo_ref[hv * h:hv * (h + 1), :], preferred_element_type=F32)
    o_ref[...] = acc


def _oproj_lat(o_lat, x, w_uv, w_o):
    m, nh, rank = o_lat.shape
    hv = w_uv.shape[2]
    return pl.pallas_call(
        functools.partial(_oproj_lat_body, nh=nh, rank=rank, hv=hv),
        out_shape=jax.ShapeDtypeStruct(x.shape, F32),
        compiler_params=pltpu.CompilerParams(vmem_limit_bytes=VMEM_LIMIT),
        name="attn_out_sample",
    )(o_lat.reshape(m, nh * rank), x, jnp.transpose(w_uv, (1, 0, 2)).astype(BF16), w_o.astype(BF16))


def kernel(x_prompt, x_sample, state_s5_re, state_s5_im, cache_ckv, cache_kpe, page_table, ffn_norm, ffn_w_gate, ffn_w_up, ffn_w_down, mix_norm, s5_a_re, s5_a_im, s5_log_dt, s5_b_re, s5_b_im, s5_c_re, s5_c_im, s5_d, s5_w_glu_v, s5_w_glu_g, mla_w_dq, mla_q_norm, mla_w_uq, mla_qn_norm, mla_qr_norm, mla_w_o, kv_in_norm, kv_w_dkv, kv_c_norm, kv_kpe_norm, kv_w_uk, kv_w_uv):
    bp, tp, d = x_prompt.shape
    bs, ts, _ = x_sample.shape
    assert ts == 1, "the sample group decodes one token per sequence"
    depth = ffn_norm.shape[0]
    n_a = s5_a_re.shape[0]
    assert depth == 2 and n_a == 1 and mla_w_dq.shape[0] == 1, "one S5 layer followed by one MLA layer"
    g, p = s5_a_re.shape[1], s5_a_re.shape[2]
    nh, hv = kv_w_uv.shape[1], kv_w_uv.shape[2]
    rope = kv_kpe_norm.shape[0]
    past_len = page_table.shape[1] * cache_ckv.shape[1]

    wg, wu, wd = ffn_w_gate.astype(BF16), ffn_w_up.astype(BF16), ffn_w_down.astype(BF16)

    def ffn(x, layer, idx):
        return _ffn(x, ffn_norm[layer, idx], wg, wu, wd, layer, idx)

    xp = x_prompt.reshape(bp * tp, d)
    xs = x_sample.reshape(bs, d)

    lam_re, lam_im, wb, wc = _s5_params(s5_a_re[0], s5_a_im[0], s5_log_dt[0], s5_b_re[0], s5_b_im[0],
                                        s5_c_re[0], s5_c_im[0])
    wv, wgl = s5_w_glu_v[0].astype(BF16), s5_w_glu_g[0].astype(BF16)
    xp = ffn(xp, 0, 0)
    xs = ffn(xs, 0, 0)
    xp, sre_p, sim_p = _s5_prompt(xp.reshape(bp, tp, d), mix_norm[0], lam_re, lam_im, wb, wc, s5_d[0], wv, wgl)
    xs, sre_s, sim_s = _s5_step(xs, mix_norm[0], state_s5_re[0].reshape(bs, g * p),
                                state_s5_im[0].reshape(bs, g * p), lam_re, lam_im, wb, wc, s5_d[0], wv, wgl)
    xp = ffn(xp.reshape(bp * tp, d), 0, 1)
    xs = ffn(xs, 0, 1)

    pos_p = jnp.arange(tp, dtype=jnp.int32)
    pos_s = jnp.full((bs,), past_len, dtype=jnp.int32)
    kv_args = (kv_in_norm, kv_w_dkv, kv_c_norm, kv_kpe_norm, kv_w_uk, kv_w_uv)
    ckv_p, kpe_p, k_p, vt_p = _kv(xp.reshape(bp, tp, d), pos_p, *kv_args, with_heads=True)
    ckv_s, kpe_s = _kv(xs.reshape(1, bs, d), pos_s, *kv_args, with_heads=False)

    xp = ffn(xp, 1, 0)
    xs = ffn(xs, 1, 0)
    q_args = (mix_norm[1], mla_w_dq[0], mla_q_norm[0], mla_w_uq[0], mla_qn_norm[0], mla_qr_norm[0])
    q_p = _queries(xp.reshape(bp, tp, d), pos_p, *q_args)
    ot = _attn_prompt(q_p, k_p.reshape(bp, tp, nh * LANES), vt_p, nh=nh, hv=hv)
    xp = _oproj_t(ot, xp.reshape(bp, tp, d), mla_w_o[0]).reshape(bp * tp, d)

    q_s = _queries(xs.reshape(1, bs, d), pos_s, *q_args)
    qabs = _absorb(q_s, kv_w_uk, rope)
    o_lat = _attn_sample(qabs, cache_ckv, cache_kpe, page_table, ckv_s, kpe_s)
    xs = _oproj_lat(o_lat, xs, kv_w_uv, mla_w_o[0])

    xp = ffn(xp, 1, 1)
    xs = ffn(xs, 1, 1)

    return (xp.reshape(bp, tp, d), xs.reshape(bs, 1, d),
            sre_p.reshape(n_a, bp, g, p), sim_p.reshape(n_a, bp, g, p),
            sre_s.reshape(n_a, bs, g, p), sim_s.reshape(n_a, bs, g, p),
            ckv_p.reshape(bp, tp, -1), kpe_p.reshape(bp, tp, rope),
            ckv_s.reshape(bs, 1, -1), kpe_s.reshape(bs, 1, rope))
```

```python
import functools
import math
from typing import Callable, NamedTuple

import jax
import jax.numpy as jnp
from jax import lax
from jax.experimental import pallas as pl
from jax.experimental.pallas import tpu as pltpu

F32 = jnp.float32
BF16 = jnp.bfloat16

EPS = 1e-6
FFN_RES = 0.5
NEG_BIG = -1e30
ROPE_BASE = 10000.0

GROUP_CH = 16
LANES = 128
SUBLANES = 8
S5_SLAB = 256
S5_PITCH = 12
VT_PAD = 16
VMEM_LIMIT = 56 * 1024 * 1024


def _tile(n, pref):
    if n <= pref:
        return n
    for t in range(pref, 7, -1):
        if n % t == 0 and t % 8 == 0:
            return t
    return n


def _rms(x, gain):
    ms = jnp.mean(x * x, axis=-1, keepdims=True)
    return x * lax.rsqrt(ms + EPS) * gain


def _cparams(sem):
    return pltpu.CompilerParams(dimension_semantics=sem, vmem_limit_bytes=VMEM_LIMIT)


class _Stage(NamedTuple):
    args: tuple
    in_specs: tuple
    out_shape: tuple
    out_specs: tuple
    fn: Callable


def _ffn_body(*refs, has_pre, post):
    x_ref, g_ref, wg_ref, wu_ref, wd_ref = refs[:5]
    refs = refs[5:]
    x = x_ref[...]
    if has_pre:
        ot_ref, wo_ref = refs[:2]
        refs = refs[2:]
        x = x + lax.dot_general(ot_ref[...], wo_ref[...], (((0,), (0,)), ((), ())), preferred_element_type=F32)
    n_in = len(post.args) if post else 0
    post_in, o_ref, post_out = refs[:n_in], refs[n_in], refs[n_in + 1:]
    xn = _rms(x, g_ref[...]).astype(BF16)
    gate = jnp.dot(xn, wg_ref[...], preferred_element_type=F32)
    up = jnp.dot(xn, wu_ref[...], preferred_element_type=F32)
    hid = (gate * jax.nn.sigmoid(gate) * up).astype(BF16)
    y = x + FFN_RES * jnp.dot(hid, wd_ref[...], preferred_element_type=F32)
    o_ref[...] = y
    if post:
        post.fn(y, post_in, post_out)


def _resident(shape, index_map):
    return pl.BlockSpec(shape, index_map, pipeline_mode=pl.Buffered(1))


def _ffn(x, gain, w_gate, w_up, w_down, layer, idx, tm, *, pre=None, post=None):
    m, d = x.shape
    ff = w_gate.shape[3]
    pick = lambda i: (layer, idx, 0, 0)
    args = [x, gain.reshape(1, d), w_gate, w_up, w_down]
    in_specs = [
        pl.BlockSpec((tm, d), lambda i: (i, 0)),
        _resident((1, d), lambda i: (0, 0)),
        _resident((None, None, d, ff), pick),
        _resident((None, None, d, ff), pick),
        _resident((None, None, ff, d), pick),
    ]
    if pre is not None:
        ot, w_o = pre
        nt = ot.shape[2] // tm
        args += [ot, w_o]
        in_specs += [pl.BlockSpec((None, ot.shape[1], tm), lambda i: (i // nt, 0, i % nt)),
                     _resident(w_o.shape, lambda i: (0, 0))]
    out_shape = [jax.ShapeDtypeStruct((m, d), F32)]
    out_specs = [pl.BlockSpec((tm, d), lambda i: (i, 0))]
    if post is not None:
        args += list(post.args)
        in_specs += list(post.in_specs)
        out_shape += list(post.out_shape)
        out_specs += list(post.out_specs)
    out = pl.pallas_call(
        functools.partial(_ffn_body, has_pre=pre is not None, post=post),
        grid=(m // tm,),
        in_specs=in_specs,
        out_specs=out_specs,
        out_shape=out_shape,
        compiler_params=_cparams(("parallel",)),
        name="ffn",
    )(*args)
    return out if post is not None else out[0]


def _s5_params(a_re, a_im, log_dt, b_re, b_im, c_re, c_im):
    g, p = a_re.shape
    dt = jnp.exp(log_dt)[:, None]
    mag = jnp.exp(dt * a_re)
    lam_re, lam_im = mag * jnp.cos(dt * a_im), mag * jnp.sin(dt * a_im)
    den = a_re * a_re + a_im * a_im
    f_re = ((lam_re - 1.0) * a_re + lam_im * a_im) / den
    f_im = (lam_im * a_re - (lam_re - 1.0) * a_im) / den
    bb_re = f_re[..., None] * b_re - f_im[..., None] * b_im
    bb_im = f_re[..., None] * b_im + f_im[..., None] * b_re
    gs = S5_SLAB // GROUP_CH
    ns = g // gs
    same_group = jnp.eye(gs, dtype=bool)[None, :, None, :, None]
    nk = gs * p // LANES

    def pack_b(bb):
        bb = jnp.transpose(bb.reshape(ns, gs, p, GROUP_CH), (0, 1, 3, 2))
        w = jnp.where(same_group, bb[:, :, :, None, :], 0.0)
        return w.reshape(ns, S5_SLAB, nk, LANES)

    wb = jnp.stack([pack_b(bb_re), pack_b(bb_im)], axis=3)
    wb = wb.reshape(ns, S5_SLAB, 2 * gs * p).astype(BF16)

    def pack_c(cc):
        cc = jnp.transpose(cc.reshape(ns, gs, GROUP_CH, p), (0, 3, 1, 2))
        w = jnp.where(same_group, cc[:, None, :, :, :], 0.0)
        return w.reshape(ns, nk, LANES, S5_SLAB)

    wc = jnp.stack([pack_c(c_re), -pack_c(c_im)], axis=2)
    wc = wc.reshape(ns, 2 * gs * p, S5_SLAB).astype(BF16)
    return lam_re, lam_im, wb, wc


def _s5_prompt_body(x_ref, g_ref, wb_ref, wc_ref, lr_ref, li_ref, d_ref, wv_ref, wgl_ref,
                    o_ref, sre_ref, sim_ref, xr_ref, xi_ref, cr_ref, ci_ref, y_ref, *, tc, ns, nk):
    t_idx = pl.program_id(1)

    @pl.when(t_idx == 0)
    def _():
        cr_ref[...] = jnp.zeros_like(cr_ref)
        ci_ref[...] = jnp.zeros_like(ci_ref)

    x = x_ref[0]
    h = _rms(x, g_ref[...])
    hb = h.astype(BF16)
    for s in range(ns):
        us = hb[:, S5_SLAB * s:S5_SLAB * (s + 1)]
        for k in range(nk):
            xk = jnp.dot(us, wb_ref[s, :, 2 * LANES * k:2 * LANES * (k + 1)],
                         preferred_element_type=F32)
            xr_ref[s, pl.ds(k, tc, stride=S5_PITCH), :] = xk[:, :LANES]
            xi_ref[s, pl.ds(k, tc, stride=S5_PITCH), :] = xk[:, LANES:]

    lam = [(lr_ref[s], li_ref[s]) for s in range(ns)]

    def step(t, carry):
        row = pl.multiple_of(t * S5_PITCH, math.gcd(S5_PITCH, SUBLANES))
        out = []
        for s in range(ns):
            sr, si = carry[s]
            lr, li = lam[s]
            nr = lr * sr - li * si + xr_ref[s, pl.ds(row, nk), :]
            ni = lr * si + li * sr + xi_ref[s, pl.ds(row, nk), :]
            xr_ref[s, pl.ds(row, nk), :] = nr
            xi_ref[s, pl.ds(row, nk), :] = ni
            out.append((nr, ni))
        return tuple(out)

    init = tuple((cr_ref[s], ci_ref[s]) for s in range(ns))
    fin = lax.fori_loop(0, tc, step, init, unroll=8)
    for s in range(ns):
        cr_ref[s] = fin[s][0]
        ci_ref[s] = fin[s][1]

    for s in range(ns):
        acc = None
        for k in range(nk):
            sk = jnp.concatenate([xr_ref[s, pl.ds(k, tc, stride=S5_PITCH), :],
                                  xi_ref[s, pl.ds(k, tc, stride=S5_PITCH), :]], axis=1).astype(BF16)
            part = jnp.dot(sk, wc_ref[s, 2 * LANES * k:2 * LANES * (k + 1), :],
                           preferred_element_type=F32)
            acc = part if acc is None else acc + part
        y_ref[:, S5_SLAB * s:S5_SLAB * (s + 1)] = acc

    y = jax.nn.gelu(y_ref[...] + d_ref[...] * h, approximate=True).astype(BF16)
    val = jnp.dot(y, wv_ref[...], preferred_element_type=F32)
    gate = jnp.dot(y, wgl_ref[...], preferred_element_type=F32)
    o_ref[0] = x + val * jax.nn.sigmoid(gate)

    @pl.when(t_idx == pl.num_programs(1) - 1)
    def _():
        sre_ref[0] = cr_ref[...].reshape(ns * nk, LANES)
        sim_ref[0] = ci_ref[...].reshape(ns * nk, LANES)


def _s5_prompt(x, gain, lam_re, lam_im, wb, wc, d_skip, w_v, w_g, *, tc_pref=512):
    b, t, d = x.shape
    ns = wb.shape[0]
    nk = wb.shape[2] // (2 * LANES)
    assert nk == SUBLANES, "one token's slab state must fill whole vregs"
    tc = _tile(t, tc_pref)
    lr = lam_re.reshape(ns, nk, LANES)
    li = lam_im.reshape(ns, nk, LANES)
    const3 = lambda i, j: (0, 0, 0)
    const2 = lambda i, j: (0, 0)
    out, s_re, s_im = pl.pallas_call(
        functools.partial(_s5_prompt_body, tc=tc, ns=ns, nk=nk),
        grid=(b, t // tc),
        in_specs=[
            pl.BlockSpec((1, tc, d), lambda i, j: (i, j, 0)),
            _resident((1, d), const2),
            _resident(wb.shape, const3),
            _resident(wc.shape, const3),
            _resident(lr.shape, const3),
            _resident(li.shape, const3),
            _resident((1, d), const2),
            _resident(w_v.shape, const2),
            _resident(w_g.shape, const2),
        ],
        out_specs=[
            pl.BlockSpec((1, tc, d), lambda i, j: (i, j, 0)),
            pl.BlockSpec((1, ns * nk, LANES), lambda i, j: (i, 0, 0)),
            pl.BlockSpec((1, ns * nk, LANES), lambda i, j: (i, 0, 0)),
        ],
        out_shape=[
            jax.ShapeDtypeStruct((b, t, d), F32),
            jax.ShapeDtypeStruct((b, ns * nk, LANES), F32),
            jax.ShapeDtypeStruct((b, ns * nk, LANES), F32),
        ],
        scratch_shapes=[
            pltpu.VMEM((ns, tc * S5_PITCH, LANES), F32),
            pltpu.VMEM((ns, tc * S5_PITCH, LANES), F32),
            pltpu.VMEM((ns, nk, LANES), F32),
            pltpu.VMEM((ns, nk, LANES), F32),
            pltpu.VMEM((tc, d), F32),
        ],
        compiler_params=_cparams(("parallel", "arbitrary")),
        name="s5_prompt",
    )(x, gain.reshape(1, d), wb, wc, lr, li, d_skip.reshape(1, d), w_v, w_g)
    return out, s_re, s_im


def _s5_step_body(x_ref, g_ref, s0r_ref, s0i_ref, wb_ref, wc_ref, lr_ref, li_ref, d_ref,
                  wv_ref, wgl_ref, o_ref, sre_ref, sim_ref, y_ref, *, ns, nk):
    x = x_ref[...]
    h = _rms(x, g_ref[...])
    hb = h.astype(BF16)
    for s in range(ns):
        xs = jnp.dot(hb[:, S5_SLAB * s:S5_SLAB * (s + 1)], wb_ref[s], preferred_element_type=F32)
        acc = None
        for k in range(nk):
            col = (s * nk + k) * LANES
            lr = lr_ref[:, col:col + LANES]
            li = li_ref[:, col:col + LANES]
            s0r = s0r_ref[:, col:col + LANES]
            s0i = s0i_ref[:, col:col + LANES]
            nr = lr * s0r - li * s0i + xs[:, 2 * LANES * k:2 * LANES * k + LANES]
            ni = lr * s0i + li * s0r + xs[:, 2 * LANES * k + LANES:2 * LANES * (k + 1)]
            sre_ref[:, col:col + LANES] = nr
            sim_ref[:, col:col + LANES] = ni
            part = (jnp.dot(nr.astype(BF16), wc_ref[s, 2 * LANES * k:2 * LANES * k + LANES, :],
                            preferred_element_type=F32)
                    + jnp.dot(ni.astype(BF16), wc_ref[s, 2 * LANES * k + LANES:2 * LANES * (k + 1), :],
                              preferred_element_type=F32))
            acc = part if acc is None else acc + part
        y_ref[:, S5_SLAB * s:S5_SLAB * (s + 1)] = acc
    y = jax.nn.gelu(y_ref[...] + d_ref[...] * h, approximate=True).astype(BF16)
    val = jnp.dot(y, wv_ref[...], preferred_element_type=F32)
    gate = jnp.dot(y, wgl_ref[...], preferred_element_type=F32)
    o_ref[...] = x + val * jax.nn.sigmoid(gate)


def _s5_step(x, gain, s0_re, s0_im, lam_re, lam_im, wb, wc, d_skip, w_v, w_g):
    m, d = x.shape
    ns = wb.shape[0]
    nk = wb.shape[2] // (2 * LANES)
    nstate = s0_re.shape[1]
    args = (x, gain.reshape(1, d), s0_re, s0_im, wb, wc, lam_re.reshape(1, nstate),
            lam_im.reshape(1, nstate), d_skip.reshape(1, d), w_v, w_g)
    return pl.pallas_call(
        functools.partial(_s5_step_body, ns=ns, nk=nk),
        out_shape=[
            jax.ShapeDtypeStruct((m, d), F32),
            jax.ShapeDtypeStruct((m, nstate), F32),
            jax.ShapeDtypeStruct((m, nstate), F32),
        ],
        scratch_shapes=[pltpu.VMEM((m, d), F32)],
        compiler_params=pltpu.CompilerParams(vmem_limit_bytes=VMEM_LIMIT),
        name="s5_step",
    )(*args)


def _rope_tables(pos, half):
    inv = ROPE_BASE ** (-jnp.arange(half, dtype=F32) / half)
    ang = pos.astype(F32)[:, None] * inv[None, :]
    return jnp.cos(ang), jnp.sin(ang)


def _kv_compute(x, in_refs, out_refs, *, rank, rope, with_heads, nh):
    if with_heads:
        g_ref, w_ref, cn_ref, cos_ref, sin_ref, wuk_ref, wuvt_ref = in_refs
        c_ref, kpe_ref, k_ref, vt_ref = out_refs
    else:
        g_ref, w_ref, cn_ref, cos_ref, sin_ref = in_refs
        c_ref, kpe_ref = out_refs
    hk = _rms(x, g_ref[...]).astype(BF16)
    ck = jnp.dot(hk, w_ref[...], preferred_element_type=F32)
    c = _rms(ck[:, :rank], cn_ref[...])
    c_ref[...] = c

    def pe_block(i, copies):
        pe = ck[:, rank + 2 * LANES * i:rank + 2 * LANES * i + LANES]
        rot = ck[:, rank + 2 * LANES * i + LANES:rank + 2 * LANES * (i + 1)]
        r = lax.rsqrt(jnp.sum(pe * pe, axis=-1, keepdims=True) * (1.0 / (rope * copies)) + EPS)
        return (pe * cos_ref[:, LANES * i:LANES * (i + 1)] + rot * sin_ref[:, LANES * i:LANES * (i + 1)]) * r

    kpe_ref[...] = pe_block(0, 1)[:, :rope]
    if with_heads:
        cb = c.astype(BF16)
        pe_mid = pe_block(1, 2)
        kn = jnp.dot(cb, wuk_ref[...], preferred_element_type=F32)
        for h in range(nh):
            k_ref[:, LANES * h:LANES * (h + 1)] = (kn[:, LANES * h:LANES * (h + 1)] + pe_mid).astype(BF16)
        vt = lax.dot_general(wuvt_ref[...], cb, (((1,), (1,)), ((), ())), preferred_element_type=F32)
        hv = vt.shape[0] // nh
        hvx = hv + VT_PAD
        ones_row = (lax.broadcasted_iota(jnp.int32, (VT_PAD, vt.shape[1]), 0) == 0).astype(BF16)
        for h in range(nh):
            vt_ref[hvx * h:hvx * h + hv, :] = vt[hv * h:hv * (h + 1)].astype(BF16)
            vt_ref[hvx * h + hv:hvx * (h + 1), :] = ones_row


def _kv_stage(b, t, tm, pos, kv_in_norm, w_dkv, c_norm, kpe_norm, w_uk, w_uv, *, with_heads):
    d = w_dkv.shape[0]
    rank = c_norm.shape[0]
    rope = kpe_norm.shape[0]
    half = rope // 2
    nh, nope = w_uk.shape[1], w_uk.shape[2]
    hv = w_uv.shape[2]
    nt = t // tm
    m = b * t
    assert nope + 2 * rope == LANES
    w_pe = w_dkv[:, rank:]
    w_rot = jnp.concatenate([-w_pe[:, half:], w_pe[:, :half]], axis=1)
    zero = lambda n: jnp.zeros((d, n), F32)
    w_ext = jnp.concatenate([
        w_dkv[:, :rank],
        w_pe, zero(LANES - rope), w_rot, zero(LANES - rope),
        zero(nope), w_pe, w_pe, zero(nope), w_rot, w_rot,
    ], axis=1).astype(BF16)
    cos, sin = _rope_tables(pos, half)
    g_rot = jnp.concatenate([kpe_norm[half:], kpe_norm[:half]])
    gcos = jnp.concatenate([cos, cos], 1) * kpe_norm[None, :]
    gsin = jnp.concatenate([sin, sin], 1) * g_rot[None, :]
    zt = lambda n: jnp.zeros((t, n), F32)
    cos_t = jnp.concatenate([gcos, zt(LANES - rope), zt(nope), gcos, gcos], axis=1)
    sin_t = jnp.concatenate([gsin, zt(LANES - rope), zt(nope), gsin, gsin], axis=1)

    const = lambda i: (0, 0)
    args = [kv_in_norm.reshape(1, d), w_ext, c_norm.reshape(1, rank), cos_t, sin_t]
    in_specs = [
        _resident((1, d), const),
        _resident(w_ext.shape, const),
        _resident((1, rank), const),
        pl.BlockSpec((tm, 2 * LANES), lambda i: (i % nt, 0)),
        pl.BlockSpec((tm, 2 * LANES), lambda i: (i % nt, 0)),
    ]
    out_specs = [pl.BlockSpec((tm, rank), lambda i: (i, 0)), pl.BlockSpec((tm, rope), lambda i: (i, 0))]
    out_shape = [jax.ShapeDtypeStruct((m, rank), F32), jax.ShapeDtypeStruct((m, rope), F32)]
    if with_heads:
        wuk = jnp.concatenate([w_uk, jnp.zeros((rank, nh, LANES - nope), F32)], axis=2)
        wuk = wuk.reshape(rank, nh * LANES).astype(BF16)
        wuvt = jnp.transpose(w_uv, (1, 2, 0)).reshape(nh * hv, rank).astype(BF16)
        in_specs += [_resident(wuk.shape, const), _resident(wuvt.shape, const)]
        args += [wuk, wuvt]
        out_specs += [pl.BlockSpec((tm, nh * LANES), lambda i: (i, 0)),
                      pl.BlockSpec((None, nh * (hv + VT_PAD), tm), lambda i: (i // nt, 0, i % nt))]
        out_shape += [jax.ShapeDtypeStruct((m, nh * LANES), BF16),
                      jax.ShapeDtypeStruct((b, nh * (hv + VT_PAD), t), BF16)]
    fn = functools.partial(_kv_compute, rank=rank, rope=rope, with_heads=with_heads, nh=nh)
    return _Stage(tuple(args), tuple(in_specs), tuple(out_shape), tuple(out_specs), fn)


def _q_compute(x, in_refs, out_refs, *, nh, nope, rope):
    g_ref, wdq_ref, qn_ref, w1t_ref, tabt_ref = in_refs
    qt_ref, = out_refs
    hb = _rms(x, g_ref[...]).astype(BF16)
    cq = jnp.dot(hb, wdq_ref[...], preferred_element_type=F32)
    cqb = _rms(cq, qn_ref[...]).astype(BF16)
    qm = lax.dot_general(w1t_ref[...], cqb, (((1,), (1,)), ((), ())), preferred_element_type=F32)
    tab_n, tab_p = tabt_ref[:nope, :], tabt_ref[nope:, :]
    for h in range(nh):
        q_n = qm[LANES * h:LANES * h + nope, :]
        q_p = qm[LANES * h + nope:LANES * (h + 1), :]
        r_n = lax.rsqrt(jnp.mean(q_n * q_n, axis=0, keepdims=True) + EPS)
        q_pe = q_p[:rope]
        r_p = lax.rsqrt(jnp.mean(q_pe * q_pe, axis=0, keepdims=True) + EPS)
        qt_ref[LANES * h:LANES * h + nope, :] = (q_n * tab_n * r_n).astype(BF16)
        qt_ref[LANES * h + nope:LANES * (h + 1), :] = (q_p * tab_p * r_p).astype(BF16)


def _q_stage(b, t, tm, pos, mix_norm, w_dq, q_norm, w_uq, qn_norm, qr_norm):
    d, qrank = w_dq.shape
    nh = w_uq.shape[1]
    nope, rope = qn_norm.shape[0], qr_norm.shape[0]
    assert nope + 2 * rope == LANES
    half = rope // 2
    scale = (nope + rope) ** -0.5 * math.log2(math.e)
    nt = t // tm
    m = b * t
    w_n, w_p = w_uq[:, :, :nope], w_uq[:, :, nope:]
    w_rot = jnp.concatenate([-w_p[:, :, half:], w_p[:, :, :half]], axis=2)
    w1t = jnp.concatenate([w_n, w_p, w_rot], axis=2).reshape(qrank, nh * LANES).T.astype(BF16)
    g_rot = jnp.concatenate([qr_norm[half:], qr_norm[:half]])
    cos, sin = _rope_tables(pos, half)
    tabt = jnp.concatenate([
        jnp.broadcast_to(qn_norm[:, None], (nope, t)),
        jnp.concatenate([cos, cos], 1).T * qr_norm[:, None],
        jnp.concatenate([sin, sin], 1).T * g_rot[:, None]], axis=0) * scale
    const = lambda i: (0, 0)
    args = (mix_norm.reshape(1, d), w_dq.astype(BF16), q_norm.reshape(1, qrank), w1t, tabt)
    in_specs = (
        _resident((1, d), const),
        _resident(w_dq.shape, const),
        _resident((1, qrank), const),
        _resident(w1t.shape, const),
        pl.BlockSpec((LANES, tm), lambda i: (0, i % nt)),
    )
    out_shape = (jax.ShapeDtypeStruct((nh * LANES, m), BF16),)
    out_specs = (pl.BlockSpec((nh * LANES, tm), lambda i: (0, i)),)
    return _Stage(args, in_specs, out_shape, out_specs, functools.partial(_q_compute, nh=nh, nope=nope, rope=rope))


def _attn_prompt_body(q_ref, k_ref, vt_ref, o_ref, *, tq, hp, hv):
    qi = pl.program_id(2)
    hvx = hv + VT_PAD
    qs = [q_ref[LANES * a:LANES * (a + 1), :] for a in range(hp)]

    def block(ki, carry, masked):
        off = pl.multiple_of(ki * tq, tq)
        ss = []
        for a in range(hp):
            k = k_ref[pl.ds(off, tq), LANES * a:LANES * (a + 1)]
            ss.append(jnp.dot(k, qs[a], preferred_element_type=F32))
        ps, stats = [], []
        for a in range(hp):
            m_i = carry[a][0]
            s = ss[a]
            if masked:
                kpos = lax.broadcasted_iota(jnp.int32, (tq, tq), 0)
                qpos = lax.broadcasted_iota(jnp.int32, (tq, tq), 1)
                s = jnp.where(kpos <= qpos, s, NEG_BIG)
            m_new = jnp.maximum(m_i, jnp.max(s, axis=0, keepdims=True))
            stats.append((m_new, jnp.exp2(m_i - m_new)))
            ps.append(jnp.exp2(s - m_new).astype(BF16))
        out = []
        for a in range(hp):
            m_new, alpha = stats[a]
            pv = jnp.dot(vt_ref[hvx * a:hvx * (a + 1), pl.ds(off, tq)], ps[a], preferred_element_type=F32)
            out.append((m_new, alpha * carry[a][1] + pv))
        return tuple(out)

    init = tuple((jnp.full((1, tq), NEG_BIG, F32), jnp.zeros((hvx, tq), F32)) for _ in range(hp))
    carry = lax.fori_loop(0, qi, lambda ki, c: block(ki, c, False), init)
    fin = block(qi, carry, True)
    for a in range(hp):
        acc = fin[a][1]
        o_ref[hv * a:hv * (a + 1), :] = (acc[:hv] / acc[hv:hv + 1]).astype(BF16)


def _attn_prompt(qt, k, vt, *, nh, hv, tq_pref=512, hp=4):
    b, t, _ = k.shape
    tq = _tile(t, tq_pref)
    nq = t // tq
    assert nh % hp == 0
    return pl.pallas_call(
        functools.partial(_attn_prompt_body, tq=tq, hp=hp, hv=hv),
        grid=(b, nh // hp, t // tq),
        in_specs=[
            pl.BlockSpec((hp * LANES, tq), lambda i, h, j: (h, i * nq + j)),
            pl.BlockSpec((None, t, hp * LANES), lambda i, h, j: (i, 0, h)),
            pl.BlockSpec((None, hp * (hv + VT_PAD), t), lambda i, h, j: (i, h, 0)),
        ],
        out_specs=pl.BlockSpec((None, hp * hv, tq), lambda i, h, j: (i, h, j)),
        out_shape=jax.ShapeDtypeStruct((b, nh * hv, t), BF16),
        compiler_params=_cparams(("parallel", "parallel", "arbitrary")),
        name="attn_prompt",
    )(qt, k, vt)


def _absorb_body(qt_ref, w_ref, o_ref):
    o_ref[...] = lax.dot_general(qt_ref[...], w_ref[...], (((0,), (0,)), ((), ())),
                                 preferred_element_type=F32).astype(BF16)


def _absorb(qt, w_uk, rope):
    m = qt.shape[1]
    rank, nh, nope = w_uk.shape
    width = rank + LANES
    w = jnp.zeros((nh, LANES, width), F32)
    w = w.at[:, :nope, :rank].set(jnp.transpose(w_uk, (1, 2, 0)))
    eye = jnp.broadcast_to(jnp.eye(rope, dtype=F32), (nh, rope, rope))
    w = w.at[:, nope:nope + rope, rank:rank + rope].set(eye)
    w = w.at[:, nope + rope:nope + 2 * rope, rank:rank + rope].set(eye)
    out = pl.pallas_call(
        _absorb_body,
        grid=(nh,),
        in_specs=[pl.BlockSpec((LANES, m), lambda h: (h, 0)),
                  pl.BlockSpec((None, LANES, width), lambda h: (h, 0, 0))],
        out_specs=pl.BlockSpec((m, width), lambda h: (0, h)),
        out_shape=jax.ShapeDtypeStruct((m, nh * width), BF16),
        compiler_params=_cparams(("parallel",)),
        name="absorb_queries",
    )(qt, w.astype(BF16))
    return out.reshape(m, nh, width)


CHUNK_AHEAD = 3


def _attn_sample_body(pt_ref, q_ref, cn_ref, pn_ref, ckv_hbm, kpe_hbm, o_ref, cbuf, pbuf, sem,
                      *, npg, nch, page, rank, rope):
    b = pl.program_id(0)
    total = pl.num_programs(0) * nch

    def page_copies(chunk, slot):
        out = []
        for i in range(npg):
            pid = pt_ref[chunk * npg + i]
            out.append(pltpu.make_async_copy(ckv_hbm.at[pid], cbuf.at[slot, pl.ds(i * page, page)],
                                             sem.at[0, slot]))
            out.append(pltpu.make_async_copy(kpe_hbm.at[pid], pbuf.at[slot, :, pl.ds(i * page, page)],
                                             sem.at[1, slot]))
        return out

    @pl.when(b == 0)
    def _():
        for c in range(CHUNK_AHEAD):
            for cp in page_copies(jnp.minimum(c, total - 1), c % nch):
                cp.start()

    q_lat = q_ref[0, :, :rank]
    q_pe = q_ref[0, :, rank:rank + rope]
    nh = q_lat.shape[0]
    m_i = jnp.full((nh, 1), NEG_BIG, F32)
    l_i = jnp.zeros((nh, 1), F32)
    half = npg * page // 2

    def add_pv(accs, alpha, p, c):
        return tuple(alpha * accs[i] + jnp.dot(p[:, half * i:half * (i + 1)], c[half * i:half * (i + 1)],
                                               preferred_element_type=F32) for i in range(2))

    accs = (jnp.zeros((nh, rank), F32), jnp.zeros((nh, rank), F32))
    pending = None
    for j in range(nch):
        for cp in page_copies(b * nch + j, j):
            cp.wait()
        c = cbuf[j].astype(BF16)
        s = (lax.dot_general(q_lat, c, (((1,), (1,)), ((), ())), preferred_element_type=F32)
             + jnp.dot(q_pe, pbuf[j].astype(BF16), preferred_element_type=F32))
        if pending is not None:
            accs = add_pv(accs, *pending)
        m_new = jnp.maximum(m_i, jnp.max(s, axis=1, keepdims=True))
        alpha = jnp.exp2(m_i - m_new)
        p = jnp.exp2(s - m_new)
        l_i = alpha * l_i + jnp.sum(p, axis=1, keepdims=True)
        m_i = m_new
        pending = (alpha, p.astype(BF16), c)
        nxt = b * nch + j + CHUNK_AHEAD
        for cp in page_copies(jnp.minimum(nxt, total - 1), (j + CHUNK_AHEAD) % nch):
            cp.start()
    accs = add_pv(accs, *pending)

    @pl.when(b == pl.num_programs(0) - 1)
    def _():
        for j in range(nch - CHUNK_AHEAD, nch):
            for cp in page_copies(total - 1, (j + CHUNK_AHEAD) % nch):
                cp.wait()

    c_new = cn_ref[0].astype(BF16).astype(F32)
    p_new = pn_ref[0].astype(BF16).astype(F32)
    s_new = (jnp.sum(q_lat.astype(F32) * c_new, axis=1, keepdims=True)
             + jnp.sum(q_pe.astype(F32) * p_new, axis=1, keepdims=True))
    m_fin = jnp.maximum(m_i, s_new)
    a_old = jnp.exp2(m_i - m_fin)
    p_n = jnp.exp2(s_new - m_fin)
    l_fin = a_old * l_i + p_n
    o_new = p_n.astype(BF16).astype(F32) * c_new
    o_ref[0] = (a_old * accs[0] + (a_old * accs[1] + o_new)) / l_fin


def _attn_sample(qabs, cache_ckv, cache_kpe, page_table, c_new, kpe_new, *, npg_pref=32):
    m, nh, width = qabs.shape
    _, page, rank = cache_ckv.shape
    rope = cache_kpe.shape[2]
    n_pages = page_table.shape[1]
    npg = npg_pref if n_pages % npg_pref == 0 else n_pages
    nch = n_pages // npg
    assert nch > CHUNK_AHEAD, "a slot must not be refilled while its chunk is in use"
    kpe_t = jnp.swapaxes(cache_kpe, 1, 2)
    grid_spec = pltpu.PrefetchScalarGridSpec(
        num_scalar_prefetch=1,
        grid=(m,),
        in_specs=[
            pl.BlockSpec((1, nh, width), lambda b, pt: (b, 0, 0)),
            pl.BlockSpec((1, 1, rank), lambda b, pt: (b, 0, 0)),
            pl.BlockSpec((1, 1, rope), lambda b, pt: (b, 0, 0)),
            pl.BlockSpec(memory_space=pl.ANY),
            pl.BlockSpec(memory_space=pl.ANY),
        ],
        out_specs=pl.BlockSpec((1, nh, rank), lambda b, pt: (b, 0, 0)),
        scratch_shapes=[
            pltpu.VMEM((nch, npg * page, rank), F32),
            pltpu.VMEM((nch, rope, npg * page), F32),
            pltpu.SemaphoreType.DMA((2, nch)),
        ],
    )
    return pl.pallas_call(
        functools.partial(_attn_sample_body, npg=npg, nch=nch, page=page, rank=rank, rope=rope),
        grid_spec=grid_spec,
        out_shape=jax.ShapeDtypeStruct((m, nh, rank), F32),
        compiler_params=_cparams(("arbitrary",)),
        name="attn_sample",
    )(page_table.reshape(-1), qabs, c_new.reshape(m, 1, rank), kpe_new.reshape(m, 1, rope),
      cache_ckv, kpe_t)


def _oproj_lat_body(ol_ref, x_ref, wuv_ref, wo_ref, o_ref, *, nh, rank, hv):
    acc = x_ref[...]
    for h in range(nh):
        oh = jnp.dot(ol_ref[:, rank * h:rank * (h + 1)].astype(BF16), wuv_ref[h], preferred_element_type=F32)
        acc = acc + jnp.dot(oh.astype(BF16), wo_ref[hv * h:hv * (h + 1), :], preferred_element_type=F32)
    o_ref[...] = acc


def _oproj_lat(o_lat, x, w_uv, w_o):
    m, nh, rank = o_lat.shape
    hv = w_uv.shape[2]
    return pl.pallas_call(
        functools.partial(_oproj_lat_body, nh=nh, rank=rank, hv=hv),
        out_shape=jax.ShapeDtypeStruct(x.shape, F32),
        compiler_params=pltpu.CompilerParams(vmem_limit_bytes=VMEM_LIMIT),
        name="attn_out_sample",
    )(o_lat.reshape(m, nh * rank), x, jnp.transpose(w_uv, (1, 0, 2)).astype(BF16), w_o.astype(BF16))


def kernel(x_prompt, x_sample, state_s5_re, state_s5_im, cache_ckv, cache_kpe, page_table, ffn_norm, ffn_w_gate, ffn_w_up, ffn_w_down, mix_norm, s5_a_re, s5_a_im, s5_log_dt, s5_b_re, s5_b_im, s5_c_re, s5_c_im, s5_d, s5_w_glu_v, s5_w_glu_g, mla_w_dq, mla_q_norm, mla_w_uq, mla_qn_norm, mla_qr_norm, mla_w_o, kv_in_norm, kv_w_dkv, kv_c_norm, kv_kpe_norm, kv_w_uk, kv_w_uv):
    bp, tp, d = x_prompt.shape
    bs, ts, _ = x_sample.shape
    assert ts == 1, "the sample group decodes one token per sequence"
    depth = ffn_norm.shape[0]
    n_a = s5_a_re.shape[0]
    assert depth == 2 and n_a == 1 and mla_w_dq.shape[0] == 1, "one S5 layer followed by one MLA layer"
    g, p = s5_a_re.shape[1], s5_a_re.shape[2]
    nh, hv = kv_w_uv.shape[1], kv_w_uv.shape[2]
    rope = kv_kpe_norm.shape[0]
    past_len = page_table.shape[1] * cache_ckv.shape[1]

    wg, wu, wd = ffn_w_gate.astype(BF16), ffn_w_up.astype(BF16), ffn_w_down.astype(BF16)
    tm_p, tm_s = _tile(tp, 512), bs

    def ffn(x, layer, idx, tm, **fused):
        return _ffn(x, ffn_norm[layer, idx], wg, wu, wd, layer, idx, tm, **fused)

    xp = x_prompt.reshape(bp * tp, d)
    xs = x_sample.reshape(bs, d)
    pos_p = jnp.arange(tp, dtype=jnp.int32)
    pos_s = jnp.full((bs,), past_len, dtype=jnp.int32)
    kv_args = (kv_in_norm, kv_w_dkv, kv_c_norm, kv_kpe_norm, kv_w_uk, kv_w_uv)
    q_args = (mix_norm[1], mla_w_dq[0], mla_q_norm[0], mla_w_uq[0], mla_qn_norm[0], mla_qr_norm[0])

    lam_re, lam_im, wb, wc = _s5_params(s5_a_re[0], s5_a_im[0], s5_log_dt[0], s5_b_re[0], s5_b_im[0],
                                        s5_c_re[0], s5_c_im[0])
    wv, wgl = s5_w_glu_v[0].astype(BF16), s5_w_glu_g[0].astype(BF16)
    xp = ffn(xp, 0, 0, tm_p)
    xs = ffn(xs, 0, 0, tm_s)
    xp, sre_p, sim_p = _s5_prompt(xp.reshape(bp, tp, d), mix_norm[0], lam_re, lam_im, wb, wc, s5_d[0], wv, wgl)
    xs, sre_s, sim_s = _s5_step(xs, mix_norm[0], state_s5_re[0].reshape(bs, g * p),
                                state_s5_im[0].reshape(bs, g * p), lam_re, lam_im, wb, wc, s5_d[0], wv, wgl)
    xp, ckv_p, kpe_p, k_p, vt_p = ffn(xp.reshape(bp * tp, d), 0, 1, tm_p,
                                      post=_kv_stage(bp, tp, tm_p, pos_p, *kv_args, with_heads=True))
    xs, ckv_s, kpe_s = ffn(xs, 0, 1, tm_s, post=_kv_stage(1, bs, tm_s, pos_s, *kv_args, with_heads=False))

    xp, q_p = ffn(xp, 1, 0, tm_p, post=_q_stage(bp, tp, tm_p, pos_p, *q_args))
    xs, q_s = ffn(xs, 1, 0, tm_s, post=_q_stage(1, bs, tm_s, pos_s, *q_args))
    ot = _attn_prompt(q_p, k_p.reshape(bp, tp, nh * LANES), vt_p, nh=nh, hv=hv)
    xp = ffn(xp, 1, 1, tm_p, pre=(ot, mla_w_o[0].astype(BF16)))

    qabs = _absorb(q_s, kv_w_uk, rope)
    o_lat = _attn_sample(qabs, cache_ckv, cache_kpe, page_table, ckv_s, kpe_s)
    xs = _oproj_lat(o_lat, xs, kv_w_uv, mla_w_o[0])
    xs = ffn(xs, 1, 1, tm_s)

    return (xp.reshape(bp, tp, d), xs.reshape(bs, 1, d),
            sre_p.reshape(n_a, bp, g, p), sim_p.reshape(n_a, bp, g, p),
            sre_s.reshape(n_a, bs, g, p), sim_s.reshape(n_a, bs, g, p),
            ckv_p.reshape(bp, tp, -1), kpe_p.reshape(bp, tp, rope),
            ckv_s.reshape(bs, 1, -1), kpe_s.reshape(bs, 1, rope))
```

```python
import functools
import math
from typing import Callable, NamedTuple

import jax
import jax.numpy as jnp
from jax import lax
from jax.experimental import pallas as pl
from jax.experimental.pallas import tpu as pltpu

F32 = jnp.float32
BF16 = jnp.bfloat16

EPS = 1e-6
FFN_RES = 0.5
NEG_BIG = -1e30
ROPE_BASE = 10000.0

GROUP_CH = 16
LANES = 128
SUBLANES = 8
S5_SLAB = 256
S5_PITCH = 12
S5_SCAN_BLOCK = 8
VT_PAD = 16
KPE_COPIES = 3
VMEM_LIMIT = 56 * 1024 * 1024


def _tile(n, pref):
    if n <= pref:
        return n
    for t in range(pref, 7, -1):
        if n % t == 0 and t % 8 == 0:
            return t
    return n


def _rms(x, gain):
    ms = jnp.mean(x * x, axis=-1, keepdims=True)
    return x * lax.rsqrt(ms + EPS) * gain


def _cparams(sem):
    return pltpu.CompilerParams(dimension_semantics=sem, vmem_limit_bytes=VMEM_LIMIT)


class _Stage(NamedTuple):
    args: tuple
    in_specs: tuple
    out_shape: tuple
    out_specs: tuple
    fn: Callable


def _ffn_body(*refs, has_pre, post):
    x_ref, g_ref, wg_ref, wu_ref, wd_ref = refs[:5]
    refs = refs[5:]
    x = x_ref[...]
    if has_pre:
        ot_ref, wo_ref = refs[:2]
        refs = refs[2:]
        x = x + lax.dot_general(ot_ref[...], wo_ref[...], (((0,), (0,)), ((), ())), preferred_element_type=F32)
    n_in = len(post.args) if post else 0
    post_in, o_ref, post_out = refs[:n_in], refs[n_in], refs[n_in + 1:]
    xn = _rms(x, g_ref[...]).astype(BF16)
    gate = jnp.dot(xn, wg_ref[...], preferred_element_type=F32)
    up = jnp.dot(xn, wu_ref[...], preferred_element_type=F32)
    hid = (gate * jax.nn.sigmoid(gate) * up).astype(BF16)
    y = x + FFN_RES * jnp.dot(hid, wd_ref[...], preferred_element_type=F32)
    o_ref[...] = y
    if post:
        post.fn(y, post_in, post_out)


def _resident(shape, index_map):
    return pl.BlockSpec(shape, index_map, pipeline_mode=pl.Buffered(1))


def _ffn(x, gain, w_gate, w_up, w_down, layer, idx, tm, *, pre=None, post=None):
    m, d = x.shape
    ff = w_gate.shape[3]
    pick = lambda i: (layer, idx, 0, 0)
    args = [x, gain.reshape(1, d), w_gate, w_up, w_down]
    in_specs = [
        pl.BlockSpec((tm, d), lambda i: (i, 0)),
        _resident((1, d), lambda i: (0, 0)),
        _resident((None, None, d, ff), pick),
        _resident((None, None, d, ff), pick),
        _resident((None, None, ff, d), pick),
    ]
    if pre is not None:
        ot, w_o = pre
        nt = ot.shape[2] // tm
        args += [ot, w_o]
        in_specs += [pl.BlockSpec((None, ot.shape[1], tm), lambda i: (i // nt, 0, i % nt)),
                     _resident(w_o.shape, lambda i: (0, 0))]
    out_shape = [jax.ShapeDtypeStruct((m, d), F32)]
    out_specs = [pl.BlockSpec((tm, d), lambda i: (i, 0))]
    if post is not None:
        args += list(post.args)
        in_specs += list(post.in_specs)
        out_shape += list(post.out_shape)
        out_specs += list(post.out_specs)
    out = pl.pallas_call(
        functools.partial(_ffn_body, has_pre=pre is not None, post=post),
        grid=(m // tm,),
        in_specs=in_specs,
        out_specs=out_specs,
        out_shape=out_shape,
        compiler_params=_cparams(("parallel",)),
        name="ffn",
    )(*args)
    return out if post is not None else out[0]


def _s5_params(a_re, a_im, log_dt, b_re, b_im, c_re, c_im):
    g, p = a_re.shape
    dt = jnp.exp(log_dt)[:, None]
    mag = jnp.exp(dt * a_re)
    lam_re, lam_im = mag * jnp.cos(dt * a_im), mag * jnp.sin(dt * a_im)
    den = a_re * a_re + a_im * a_im
    f_re = ((lam_re - 1.0) * a_re + lam_im * a_im) / den
    f_im = (lam_im * a_re - (lam_re - 1.0) * a_im) / den
    bb_re = f_re[..., None] * b_re - f_im[..., None] * b_im
    bb_im = f_re[..., None] * b_im + f_im[..., None] * b_re
    gs = S5_SLAB // GROUP_CH
    ns = g // gs
    same_group = jnp.eye(gs, dtype=bool)[None, :, None, :, None]
    nk = gs * p // LANES

    def pack_b(bb):
        bb = jnp.transpose(bb.reshape(ns, gs, p, GROUP_CH), (0, 1, 3, 2))
        w = jnp.where(same_group, bb[:, :, :, None, :], 0.0)
        return w.reshape(ns, S5_SLAB, nk, LANES)

    wb = jnp.stack([pack_b(bb_re), pack_b(bb_im)], axis=3)
    wb = wb.reshape(ns, S5_SLAB, 2 * gs * p).astype(BF16)

    def pack_c(cc):
        cc = jnp.transpose(cc.reshape(ns, gs, GROUP_CH, p), (0, 3, 1, 2))
        w = jnp.where(same_group, cc[:, None, :, :, :], 0.0)
        return w.reshape(ns, nk, LANES, S5_SLAB)

    wc = jnp.stack([pack_c(c_re), -pack_c(c_im)], axis=2)
    wc = wc.reshape(ns, 2 * gs * p, S5_SLAB).astype(BF16)
    return lam_re, lam_im, wb, wc


def _s5_prompt_body(x_ref, g_ref, wb_ref, wc_ref, lr_ref, li_ref, d_ref, wv_ref, wgl_ref,
                    o_ref, sre_ref, sim_ref, xr0_ref, xi0_ref, xr1_ref, xi1_ref, cr_ref, ci_ref, *, tc, ns, nk):
    t_idx = pl.program_id(1)

    @pl.when(t_idx == 0)
    def _():
        cr_ref[...] = jnp.zeros_like(cr_ref)
        ci_ref[...] = jnp.zeros_like(ci_ref)

    half = tc // 2
    pieces = [(s, k) for s in range(ns) for k in range(nk)]
    assert half == S5_SCAN_BLOCK * len(pieces), "one matmul piece per scan block"
    x = x_ref[0]
    h = _rms(x, g_ref[...])
    hb = h.astype(BF16)
    lam = [(lr_ref[s], li_ref[s]) for s in range(ns)]
    bufs = ((xr0_ref, xi0_ref), (xr1_ref, xi1_ref))

    def bproj(hf, s, k):
        xr_ref, xi_ref = bufs[hf]
        us = hb[half * hf:half * (hf + 1), S5_SLAB * s:S5_SLAB * (s + 1)]
        xk = jnp.dot(us, wb_ref[s, :, 2 * LANES * k:2 * LANES * (k + 1)], preferred_element_type=F32)
        rows = pl.ds(k, half, stride=S5_PITCH)
        xr_ref[s, rows, :] = xk[:, :LANES]
        xi_ref[s, rows, :] = xk[:, LANES:]

    def scan(hf, t0, carry):
        xr_ref, xi_ref = bufs[hf]
        for t in range(t0, t0 + S5_SCAN_BLOCK):
            rows = pl.ds(t * S5_PITCH, nk)
            out = []
            for s in range(ns):
                sr, si = carry[s]
                lr, li = lam[s]
                nr = lr * sr - li * si + xr_ref[s, rows, :]
                ni = lr * si + li * sr + xi_ref[s, rows, :]
                xr_ref[s, rows, :] = nr
                xi_ref[s, rows, :] = ni
                out.append((nr, ni))
            carry = tuple(out)
        return carry

    def cproj(hf, s, k, acc):
        xr_ref, xi_ref = bufs[hf]
        rows = pl.ds(k, half, stride=S5_PITCH)
        sk = jnp.concatenate([xr_ref[s, rows, :], xi_ref[s, rows, :]], axis=1).astype(BF16)
        part = jnp.dot(sk, wc_ref[s, 2 * LANES * k:2 * LANES * (k + 1), :], preferred_element_type=F32)
        return part if acc is None else acc + part

    def finish(hf, accs):
        rows = slice(half * hf, half * (hf + 1))
        y = jnp.concatenate(accs, axis=1) + d_ref[...] * h[rows]
        y = jax.nn.gelu(y, approximate=True).astype(BF16)
        val = jnp.dot(y, wv_ref[...], preferred_element_type=F32)
        gate = jnp.dot(y, wgl_ref[...], preferred_element_type=F32)
        o_ref[0, rows, :] = x[rows] + val * jax.nn.sigmoid(gate)

    for s, k in pieces:
        bproj(0, s, k)
    carry = tuple((cr_ref[s], ci_ref[s]) for s in range(ns))
    for i, (s, k) in enumerate(pieces):
        carry = scan(0, S5_SCAN_BLOCK * i, carry)
        bproj(1, s, k)
    accs = [None] * ns
    for i, (s, k) in enumerate(pieces):
        carry = scan(1, S5_SCAN_BLOCK * i, carry)
        accs[s] = cproj(0, s, k, accs[s])
    for s in range(ns):
        cr_ref[s] = carry[s][0]
        ci_ref[s] = carry[s][1]
    finish(0, accs)
    accs = [None] * ns
    for s, k in pieces:
        accs[s] = cproj(1, s, k, accs[s])
    finish(1, accs)

    @pl.when(t_idx == pl.num_programs(1) - 1)
    def _():
        sre_ref[0] = cr_ref[...].reshape(ns * nk, LANES)
        sim_ref[0] = ci_ref[...].reshape(ns * nk, LANES)


def _s5_prompt(x, gain, lam_re, lam_im, wb, wc, d_skip, w_v, w_g, *, tc_pref=512):
    b, t, d = x.shape
    ns = wb.shape[0]
    nk = wb.shape[2] // (2 * LANES)
    assert nk == SUBLANES, "one token's slab state must fill whole vregs"
    tc = _tile(t, tc_pref)
    lr = lam_re.reshape(ns, nk, LANES)
    li = lam_im.reshape(ns, nk, LANES)
    const3 = lambda i, j: (0, 0, 0)
    const2 = lambda i, j: (0, 0)
    out, s_re, s_im = pl.pallas_call(
        functools.partial(_s5_prompt_body, tc=tc, ns=ns, nk=nk),
        grid=(b, t // tc),
        in_specs=[
            pl.BlockSpec((1, tc, d), lambda i, j: (i, j, 0)),
            _resident((1, d), const2),
            _resident(wb.shape, const3),
            _resident(wc.shape, const3),
            _resident(lr.shape, const3),
            _resident(li.shape, const3),
            _resident((1, d), const2),
            _resident(w_v.shape, const2),
            _resident(w_g.shape, const2),
        ],
        out_specs=[
            pl.BlockSpec((1, tc, d), lambda i, j: (i, j, 0)),
            pl.BlockSpec((1, ns * nk, LANES), lambda i, j: (i, 0, 0)),
            pl.BlockSpec((1, ns * nk, LANES), lambda i, j: (i, 0, 0)),
        ],
        out_shape=[
            jax.ShapeDtypeStruct((b, t, d), F32),
            jax.ShapeDtypeStruct((b, ns * nk, LANES), F32),
            jax.ShapeDtypeStruct((b, ns * nk, LANES), F32),
        ],
        scratch_shapes=[
            pltpu.VMEM((ns, tc // 2 * S5_PITCH, LANES), F32),
            pltpu.VMEM((ns, tc // 2 * S5_PITCH, LANES), F32),
            pltpu.VMEM((ns, tc // 2 * S5_PITCH, LANES), F32),
            pltpu.VMEM((ns, tc // 2 * S5_PITCH, LANES), F32),
            pltpu.VMEM((ns, nk, LANES), F32),
            pltpu.VMEM((ns, nk, LANES), F32),
        ],
        compiler_params=_cparams(("parallel", "arbitrary")),
        name="s5_prompt",
    )(x, gain.reshape(1, d), wb, wc, lr, li, d_skip.reshape(1, d), w_v, w_g)
    return out, s_re, s_im


def _s5_step_body(x_ref, g_ref, s0r_ref, s0i_ref, wb_ref, wc_ref, lr_ref, li_ref, d_ref,
                  wv_ref, wgl_ref, o_ref, sre_ref, sim_ref, y_ref, *, ns, nk):
    x = x_ref[...]
    h = _rms(x, g_ref[...])
    hb = h.astype(BF16)
    for s in range(ns):
        xs = jnp.dot(hb[:, S5_SLAB * s:S5_SLAB * (s + 1)], wb_ref[s], preferred_element_type=F32)
        acc = None
        for k in range(nk):
            col = (s * nk + k) * LANES
            lr = lr_ref[:, col:col + LANES]
            li = li_ref[:, col:col + LANES]
            s0r = s0r_ref[:, col:col + LANES]
            s0i = s0i_ref[:, col:col + LANES]
            nr = lr * s0r - li * s0i + xs[:, 2 * LANES * k:2 * LANES * k + LANES]
            ni = lr * s0i + li * s0r + xs[:, 2 * LANES * k + LANES:2 * LANES * (k + 1)]
            sre_ref[:, col:col + LANES] = nr
            sim_ref[:, col:col + LANES] = ni
            part = (jnp.dot(nr.astype(BF16), wc_ref[s, 2 * LANES * k:2 * LANES * k + LANES, :],
                            preferred_element_type=F32)
                    + jnp.dot(ni.astype(BF16), wc_ref[s, 2 * LANES * k + LANES:2 * LANES * (k + 1), :],
                              preferred_element_type=F32))
            acc = part if acc is None else acc + part
        y_ref[:, S5_SLAB * s:S5_SLAB * (s + 1)] = acc
    y = jax.nn.gelu(y_ref[...] + d_ref[...] * h, approximate=True).astype(BF16)
    val = jnp.dot(y, wv_ref[...], preferred_element_type=F32)
    gate = jnp.dot(y, wgl_ref[...], preferred_element_type=F32)
    o_ref[...] = x + val * jax.nn.sigmoid(gate)


def _s5_step(x, gain, s0_re, s0_im, lam_re, lam_im, wb, wc, d_skip, w_v, w_g):
    m, d = x.shape
    ns = wb.shape[0]
    nk = wb.shape[2] // (2 * LANES)
    nstate = s0_re.shape[1]
    args = (x, gain.reshape(1, d), s0_re, s0_im, wb, wc, lam_re.reshape(1, nstate),
            lam_im.reshape(1, nstate), d_skip.reshape(1, d), w_v, w_g)
    return pl.pallas_call(
        functools.partial(_s5_step_body, ns=ns, nk=nk),
        out_shape=[
            jax.ShapeDtypeStruct((m, d), F32),
            jax.ShapeDtypeStruct((m, nstate), F32),
            jax.ShapeDtypeStruct((m, nstate), F32),
        ],
        scratch_shapes=[pltpu.VMEM((m, d), F32)],
        compiler_params=pltpu.CompilerParams(vmem_limit_bytes=VMEM_LIMIT),
        name="s5_step",
    )(*args)


def _rope_tables(pos, half):
    inv = ROPE_BASE ** (-jnp.arange(half, dtype=F32) / half)
    ang = pos.astype(F32)[:, None] * inv[None, :]
    return jnp.cos(ang), jnp.sin(ang)


def _kv_compute(x, in_refs, out_refs, *, rank, rope, with_heads, nh):
    if with_heads:
        g_ref, w_ref, cn_ref, cos_ref, sin_ref, wuk_ref, wuvt_ref = in_refs
        c_ref, kpe_ref, k_ref, vt_ref = out_refs
    else:
        g_ref, w_ref, cn_ref, cos_ref, sin_ref = in_refs
        c_ref, kpe_ref = out_refs
    hk = _rms(x, g_ref[...]).astype(BF16)
    ck = jnp.dot(hk, w_ref[...], preferred_element_type=F32)
    c = _rms(ck[:, :rank], cn_ref[...])
    c_ref[...] = c

    pe = ck[:, rank:rank + LANES]
    rot = ck[:, rank + LANES:rank + 2 * LANES]
    r = lax.rsqrt(jnp.sum(pe * pe, axis=-1, keepdims=True) * (1.0 / (rope * KPE_COPIES)) + EPS)
    kpe_blk = (pe * cos_ref[...] + rot * sin_ref[...]) * r
    kpe_ref[...] = kpe_blk[:, :rope]
    if with_heads:
        cb = c.astype(BF16)
        nope = LANES - 2 * rope
        lane = lax.broadcasted_iota(jnp.int32, (1, LANES), 1)
        pe_mid = jnp.where(lane >= nope, kpe_blk, 0.0)
        kn = jnp.dot(cb, wuk_ref[...], preferred_element_type=F32)
        for h in range(nh):
            k_ref[:, LANES * h:LANES * (h + 1)] = (kn[:, LANES * h:LANES * (h + 1)] + pe_mid).astype(BF16)
        vt = lax.dot_general(wuvt_ref[...], cb, (((1,), (1,)), ((), ())), preferred_element_type=F32)
        hv = vt.shape[0] // nh
        hvx = hv + VT_PAD
        ones_row = (lax.broadcasted_iota(jnp.int32, (VT_PAD, vt.shape[1]), 0) == 0).astype(BF16)
        for h in range(nh):
            vt_ref[hvx * h:hvx * h + hv, :] = vt[hv * h:hv * (h + 1)].astype(BF16)
            vt_ref[hvx * h + hv:hvx * (h + 1), :] = ones_row


def _kv_stage(b, t, tm, pos, kv_in_norm, w_dkv, c_norm, kpe_norm, w_uk, w_uv, *, with_heads):
    d = w_dkv.shape[0]
    rank = c_norm.shape[0]
    rope = kpe_norm.shape[0]
    half = rope // 2
    nh, nope = w_uk.shape[1], w_uk.shape[2]
    hv = w_uv.shape[2]
    nt = t // tm
    m = b * t
    assert nope + 2 * rope == LANES and nope >= 2 * rope
    w_pe = w_dkv[:, rank:]
    w_rot = jnp.concatenate([-w_pe[:, half:], w_pe[:, :half]], axis=1)
    gap = jnp.zeros((d, nope - rope), F32)
    w_ext = jnp.concatenate([w_dkv[:, :rank], w_pe, gap, w_pe, w_pe, w_rot, gap, w_rot, w_rot],
                            axis=1).astype(BF16)
    cos, sin = _rope_tables(pos, half)
    g_rot = jnp.concatenate([kpe_norm[half:], kpe_norm[:half]])
    gcos = jnp.concatenate([cos, cos], 1) * kpe_norm[None, :]
    gsin = jnp.concatenate([sin, sin], 1) * g_rot[None, :]
    tgap = jnp.zeros((t, nope - rope), F32)
    cos_t = jnp.concatenate([gcos, tgap, gcos, gcos], axis=1)
    sin_t = jnp.concatenate([gsin, tgap, gsin, gsin], axis=1)

    const = lambda i: (0, 0)
    args = [kv_in_norm.reshape(1, d), w_ext, c_norm.reshape(1, rank), cos_t, sin_t]
    in_specs = [
        _resident((1, d), const),
        _resident(w_ext.shape, const),
        _resident((1, rank), const),
        pl.BlockSpec((tm, LANES), lambda i: (i % nt, 0)),
        pl.BlockSpec((tm, LANES), lambda i: (i % nt, 0)),
    ]
    out_specs = [pl.BlockSpec((tm, rank), lambda i: (i, 0)), pl.BlockSpec((tm, rope), lambda i: (i, 0))]
    out_shape = [jax.ShapeDtypeStruct((m, rank), F32), jax.ShapeDtypeStruct((m, rope), F32)]
    if with_heads:
        wuk = jnp.concatenate([w_uk, jnp.zeros((rank, nh, LANES - nope), F32)], axis=2)
        wuk = wuk.reshape(rank, nh * LANES).astype(BF16)
        wuvt = jnp.transpose(w_uv, (1, 2, 0)).reshape(nh * hv, rank).astype(BF16)
        in_specs += [_resident(wuk.shape, const), _resident(wuvt.shape, const)]
        args += [wuk, wuvt]
        out_specs += [pl.BlockSpec((tm, nh * LANES), lambda i: (i, 0)),
                      pl.BlockSpec((None, nh * (hv + VT_PAD), tm), lambda i: (i // nt, 0, i % nt))]
        out_shape += [jax.ShapeDtypeStruct((m, nh * LANES), BF16),
                      jax.ShapeDtypeStruct((b, nh * (hv + VT_PAD), t), BF16)]
    fn = functools.partial(_kv_compute, rank=rank, rope=rope, with_heads=with_heads, nh=nh)
    return _Stage(tuple(args), tuple(in_specs), tuple(out_shape), tuple(out_specs), fn)


def _q_compute(x, in_refs, out_refs, *, nh, nope, rope):
    g_ref, wdq_ref, qn_ref, w1t_ref, tabt_ref = in_refs
    qt_ref, = out_refs
    hb = _rms(x, g_ref[...]).astype(BF16)
    cq = jnp.dot(hb, wdq_ref[...], preferred_element_type=F32)
    cqb = _rms(cq, qn_ref[...]).astype(BF16)
    qm = lax.dot_general(w1t_ref[...], cqb, (((1,), (1,)), ((), ())), preferred_element_type=F32)
    tab_n, tab_p = tabt_ref[:nope, :], tabt_ref[nope:, :]
    for h in range(nh):
        q_n = qm[LANES * h:LANES * h + nope, :]
        q_p = qm[LANES * h + nope:LANES * (h + 1), :]
        r_n = lax.rsqrt(jnp.mean(q_n * q_n, axis=0, keepdims=True) + EPS)
        q_pe = q_p[:rope]
        r_p = lax.rsqrt(jnp.mean(q_pe * q_pe, axis=0, keepdims=True) + EPS)
        qt_ref[LANES * h:LANES * h + nope, :] = (q_n * tab_n * r_n).astype(BF16)
        qt_ref[LANES * h + nope:LANES * (h + 1), :] = (q_p * tab_p * r_p).astype(BF16)


def _q_stage(b, t, tm, pos, mix_norm, w_dq, q_norm, w_uq, qn_norm, qr_norm):
    d, qrank = w_dq.shape
    nh = w_uq.shape[1]
    nope, rope = qn_norm.shape[0], qr_norm.shape[0]
    assert nope + 2 * rope == LANES
    half = rope // 2
    scale = (nope + rope) ** -0.5 * math.log2(math.e)
    nt = t // tm
    m = b * t
    w_n, w_p = w_uq[:, :, :nope], w_uq[:, :, nope:]
    w_rot = jnp.concatenate([-w_p[:, :, half:], w_p[:, :, :half]], axis=2)
    w1t = jnp.concatenate([w_n, w_p, w_rot], axis=2).reshape(qrank, nh * LANES).T.astype(BF16)
    g_rot = jnp.concatenate([qr_norm[half:], qr_norm[:half]])
    cos, sin = _rope_tables(pos, half)
    tabt = jnp.concatenate([
        jnp.broadcast_to(qn_norm[:, None], (nope, t)),
        jnp.concatenate([cos, cos], 1).T * qr_norm[:, None],
        jnp.concatenate([sin, sin], 1).T * g_rot[:, None]], axis=0) * scale
    const = lambda i: (0, 0)
    args = (mix_norm.reshape(1, d), w_dq.astype(BF16), q_norm.reshape(1, qrank), w1t, tabt)
    in_specs = (
        _resident((1, d), const),
        _resident(w_dq.shape, const),
        _resident((1, qrank), const),
        _resident(w1t.shape, const),
        pl.BlockSpec((LANES, tm), lambda i: (0, i % nt)),
    )
    out_shape = (jax.ShapeDtypeStruct((nh * LANES, m), BF16),)
    out_specs = (pl.BlockSpec((nh * LANES, tm), lambda i: (0, i)),)
    return _Stage(args, in_specs, out_shape, out_specs, functools.partial(_q_compute, nh=nh, nope=nope, rope=rope))


def _attn_prompt_body(q_ref, k_ref, vt_ref, o_ref, *, tq, hp, hv):
    qi = pl.program_id(2)
    hvx = hv + VT_PAD
    qs = [q_ref[LANES * a:LANES * (a + 1), :] for a in range(hp)]

    def block(ki, carry, masked):
        off = pl.multiple_of(ki * tq, tq)
        ss = []
        for a in range(hp):
            k = k_ref[pl.ds(off, tq), LANES * a:LANES * (a + 1)]
            ss.append(jnp.dot(k, qs[a], preferred_element_type=F32))
        ps, stats = [], []
        for a in range(hp):
            m_i = carry[a][0]
            s = ss[a]
            if masked:
                kpos = lax.broadcasted_iota(jnp.int32, (tq, tq), 0)
                qpos = lax.broadcasted_iota(jnp.int32, (tq, tq), 1)
                s = jnp.where(kpos <= qpos, s, NEG_BIG)
            m_new = jnp.maximum(m_i, jnp.max(s, axis=0, keepdims=True))
            stats.append((m_new, jnp.exp2(m_i - m_new)))
            ps.append(jnp.exp2(s - m_new).astype(BF16))
        out = []
        for a in range(hp):
            m_new, alpha = stats[a]
            pv = jnp.dot(vt_ref[hvx * a:hvx * (a + 1), pl.ds(off, tq)], ps[a], preferred_element_type=F32)
            out.append((m_new, alpha * carry[a][1] + pv))
        return tuple(out)

    init = tuple((jnp.full((1, tq), NEG_BIG, F32), jnp.zeros((hvx, tq), F32)) for _ in range(hp))
    carry = lax.fori_loop(0, qi, lambda ki, c: block(ki, c, False), init)
    fin = block(qi, carry, True)
    for a in range(hp):
        acc = fin[a][1]
        o_ref[hv * a:hv * (a + 1), :] = (acc[:hv] / acc[hv:hv + 1]).astype(BF16)


def _attn_prompt(qt, k, vt, *, nh, hv, tq_pref=512, hp=4):
    b, t, _ = k.shape
    tq = _tile(t, tq_pref)
    nq = t // tq
    assert nh % hp == 0
    return pl.pallas_call(
        functools.partial(_attn_prompt_body, tq=tq, hp=hp, hv=hv),
        grid=(b, nh // hp, t // tq),
        in_specs=[
            pl.BlockSpec((hp * LANES, tq), lambda i, h, j: (h, i * nq + j)),
            pl.BlockSpec((None, t, hp * LANES), lambda i, h, j: (i, 0, h)),
            pl.BlockSpec((None, hp * (hv + VT_PAD), t), lambda i, h, j: (i, h, 0)),
        ],
        out_specs=pl.BlockSpec((None, hp * hv, tq), lambda i, h, j: (i, h, j)),
        out_shape=jax.ShapeDtypeStruct((b, nh * hv, t), BF16),
        compiler_params=_cparams(("parallel", "parallel", "arbitrary")),
        name="attn_prompt",
    )(qt, k, vt)


def _absorb_body(qt_ref, w_ref, o_ref):
    nh, _, width = w_ref.shape
    for h in range(nh):
        o_ref[:, width * h:width * (h + 1)] = lax.dot_general(
            qt_ref[LANES * h:LANES * (h + 1), :], w_ref[h], (((0,), (0,)), ((), ())),
            preferred_element_type=F32).astype(BF16)


def _absorb(qt, w_uk, rope):
    m = qt.shape[1]
    rank, nh, nope = w_uk.shape
    width = rank + LANES
    w = jnp.zeros((nh, LANES, width), F32)
    w = w.at[:, :nope, :rank].set(jnp.transpose(w_uk, (1, 2, 0)))
    eye = jnp.broadcast_to(jnp.eye(rope, dtype=F32), (nh, rope, rope))
    w = w.at[:, nope:nope + rope, rank:rank + rope].set(eye)
    w = w.at[:, nope + rope:nope + 2 * rope, rank:rank + rope].set(eye)
    out = pl.pallas_call(
        _absorb_body,
        out_shape=jax.ShapeDtypeStruct((m, nh * width), BF16),
        compiler_params=pltpu.CompilerParams(vmem_limit_bytes=VMEM_LIMIT),
        name="absorb_queries",
    )(qt, w.astype(BF16))
    return out.reshape(m, nh, width)


CHUNK_AHEAD = 3


def _attn_sample_body(pt_ref, q_ref, cn_ref, pn_ref, ckv_hbm, kpe_hbm, o_ref, cbuf, pbuf, sem,
                      *, npg, nch, page, rank, rope):
    b = pl.program_id(0)
    total = pl.num_programs(0) * nch

    def page_copies(chunk, slot):
        out = []
        for i in range(npg):
            pid = pt_ref[chunk * npg + i]
            out.append(pltpu.make_async_copy(ckv_hbm.at[pid], cbuf.at[slot, pl.ds(i * page, page)],
                                             sem.at[0, slot]))
            out.append(pltpu.make_async_copy(kpe_hbm.at[pid], pbuf.at[slot, :, pl.ds(i * page, page)],
                                             sem.at[1, slot]))
        return out

    @pl.when(b == 0)
    def _():
        for c in range(CHUNK_AHEAD):
            for cp in page_copies(jnp.minimum(c, total - 1), c % nch):
                cp.start()

    q_lat = q_ref[0, :, :rank]
    q_pe = q_ref[0, :, rank:rank + rope]
    nh = q_lat.shape[0]
    m_i = jnp.full((nh, 1), NEG_BIG, F32)
    l_i = jnp.zeros((nh, 1), F32)
    half = npg * page // 2

    def add_pv(accs, alpha, p, c):
        return tuple(alpha * accs[i] + jnp.dot(p[:, half * i:half * (i + 1)], c[half * i:half * (i + 1)],
                                               preferred_element_type=F32) for i in range(2))

    accs = (jnp.zeros((nh, rank), F32), jnp.zeros((nh, rank), F32))
    pending = None
    for j in range(nch):
        for cp in page_copies(b * nch + j, j):
            cp.wait()
        c = cbuf[j].astype(BF16)
        s = (lax.dot_general(q_lat, c, (((1,), (1,)), ((), ())), preferred_element_type=F32)
             + jnp.dot(q_pe, pbuf[j].astype(BF16), preferred_element_type=F32))
        if pending is not None:
            accs = add_pv(accs, *pending)
        m_new = jnp.maximum(m_i, jnp.max(s, axis=1, keepdims=True))
        alpha = jnp.exp2(m_i - m_new)
        p = jnp.exp2(s - m_new)
        l_i = alpha * l_i + jnp.sum(p, axis=1, keepdims=True)
        m_i = m_new
        pending = (alpha, p.astype(BF16), c)
        nxt = b * nch + j + CHUNK_AHEAD
        for cp in page_copies(jnp.minimum(nxt, total - 1), (j + CHUNK_AHEAD) % nch):
            cp.start()
    accs = add_pv(accs, *pending)

    @pl.when(b == pl.num_programs(0) - 1)
    def _():
        for j in range(nch - CHUNK_AHEAD, nch):
            for cp in page_copies(total - 1, (j + CHUNK_AHEAD) % nch):
                cp.wait()

    c_new = cn_ref[0].astype(BF16).astype(F32)
    p_new = pn_ref[0].astype(BF16).astype(F32)
    s_new = (jnp.sum(q_lat.astype(F32) * c_new, axis=1, keepdims=True)
             + jnp.sum(q_pe.astype(F32) * p_new, axis=1, keepdims=True))
    m_fin = jnp.maximum(m_i, s_new)
    a_old = jnp.exp2(m_i - m_fin)
    p_n = jnp.exp2(s_new - m_fin)
    l_fin = a_old * l_i + p_n
    o_new = p_n.astype(BF16).astype(F32) * c_new
    o_ref[0] = (a_old * accs[0] + (a_old * accs[1] + o_new)) / l_fin


def _attn_sample(qabs, cache_ckv, cache_kpe, page_table, c_new, kpe_new, *, npg_pref=32):
    m, nh, width = qabs.shape
    _, page, rank = cache_ckv.shape
    rope = cache_kpe.shape[2]
    n_pages = page_table.shape[1]
    npg = npg_pref if n_pages % npg_pref == 0 else n_pages
    nch = n_pages // npg
    assert nch > CHUNK_AHEAD, "a slot must not be refilled while its chunk is in use"
    kpe_t = jnp.swapaxes(cache_kpe, 1, 2)
    grid_spec = pltpu.PrefetchScalarGridSpec(
        num_scalar_prefetch=1,
        grid=(m,),
        in_specs=[
            pl.BlockSpec((1, nh, width), lambda b, pt: (b, 0, 0)),
            pl.BlockSpec((1, 1, rank), lambda b, pt: (b, 0, 0)),
            pl.BlockSpec((1, 1, rope), lambda b, pt: (b, 0, 0)),
            pl.BlockSpec(memory_space=pl.ANY),
            pl.BlockSpec(memory_space=pl.ANY),
        ],
        out_specs=pl.BlockSpec((1, nh, rank), lambda b, pt: (b, 0, 0)),
        scratch_shapes=[
            pltpu.VMEM((nch, npg * page, rank), F32),
            pltpu.VMEM((nch, rope, npg * page), F32),
            pltpu.SemaphoreType.DMA((2, nch)),
        ],
    )
    return pl.pallas_call(
        functools.partial(_attn_sample_body, npg=npg, nch=nch, page=page, rank=rank, rope=rope),
        grid_spec=grid_spec,
        out_shape=jax.ShapeDtypeStruct((m, nh, rank), F32),
        compiler_params=_cparams(("arbitrary",)),
        name="attn_sample",
    )(page_table.reshape(-1), qabs, c_new.reshape(m, 1, rank), kpe_new.reshape(m, 1, rope),
      cache_ckv, kpe_t)


def _oproj_lat_body(ol_ref, x_ref, wuv_ref, wo_ref, o_ref, *, nh, rank, hv):
    acc = x_ref[...]
    for h in range(nh):
        oh = jnp.dot(ol_ref[:, rank * h:rank * (h + 1)].astype(BF16), wuv_ref[h], preferred_element_type=F32)
        acc = acc + jnp.dot(oh.astype(BF16), wo_ref[hv * h:hv * (h + 1), :], preferred_element_type=F32)
    o_ref[...] = acc


def _oproj_lat(o_lat, x, w_uv, w_o):
    m, nh, rank = o_lat.shape
    hv = w_uv.shape[2]
    return pl.pallas_call(
        functools.partial(_oproj_lat_body, nh=nh, rank=rank, hv=hv),
        out_shape=jax.ShapeDtypeStruct(x.shape, F32),
        compiler_params=pltpu.CompilerParams(vmem_limit_bytes=VMEM_LIMIT),
        name="attn_out_sample",
    )(o_lat.reshape(m, nh * rank), x, jnp.transpose(w_uv, (1, 0, 2)).astype(BF16), w_o.astype(BF16))


def kernel(x_prompt, x_sample, state_s5_re, state_s5_im, cache_ckv, cache_kpe, page_table, ffn_norm, ffn_w_gate, ffn_w_up, ffn_w_down, mix_norm, s5_a_re, s5_a_im, s5_log_dt, s5_b_re, s5_b_im, s5_c_re, s5_c_im, s5_d, s5_w_glu_v, s5_w_glu_g, mla_w_dq, mla_q_norm, mla_w_uq, mla_qn_norm, mla_qr_norm, mla_w_o, kv_in_norm, kv_w_dkv, kv_c_norm, kv_kpe_norm, kv_w_uk, kv_w_uv):
    bp, tp, d = x_prompt.shape
    bs, ts, _ = x_sample.shape
    assert ts == 1, "the sample group decodes one token per sequence"
    depth = ffn_norm.shape[0]
    n_a = s5_a_re.shape[0]
    assert depth == 2 and n_a == 1 and mla_w_dq.shape[0] == 1, "one S5 layer followed by one MLA layer"
    g, p = s5_a_re.shape[1], s5_a_re.shape[2]
    nh, hv = kv_w_uv.shape[1], kv_w_uv.shape[2]
    rope = kv_kpe_norm.shape[0]
    past_len = page_table.shape[1] * cache_ckv.shape[1]

    wg, wu, wd = ffn_w_gate.astype(BF16), ffn_w_up.astype(BF16), ffn_w_down.astype(BF16)
    tm_p, tm_s = _tile(tp, 512), bs

    def ffn(x, layer, idx, tm, **fused):
        return _ffn(x, ffn_norm[layer, idx], wg, wu, wd, layer, idx, tm, **fused)

    xp = x_prompt.reshape(bp * tp, d)
    xs = x_sample.reshape(bs, d)
    pos_p = jnp.arange(tp, dtype=jnp.int32)
    pos_s = jnp.full((bs,), past_len, dtype=jnp.int32)
    kv_args = (kv_in_norm, kv_w_dkv, kv_c_norm, kv_kpe_norm, kv_w_uk, kv_w_uv)
    q_args = (mix_norm[1], mla_w_dq[0], mla_q_norm[0], mla_w_uq[0], mla_qn_norm[0], mla_qr_norm[0])

    lam_re, lam_im, wb, wc = _s5_params(s5_a_re[0], s5_a_im[0], s5_log_dt[0], s5_b_re[0], s5_b_im[0],
                                        s5_c_re[0], s5_c_im[0])
    wv, wgl = s5_w_glu_v[0].astype(BF16), s5_w_glu_g[0].astype(BF16)
    xp = ffn(xp, 0, 0, tm_p)
    xs = ffn(xs, 0, 0, tm_s)
    xp, sre_p, sim_p = _s5_prompt(xp.reshape(bp, tp, d), mix_norm[0], lam_re, lam_im, wb, wc, s5_d[0], wv, wgl)
    xs, sre_s, sim_s = _s5_step(xs, mix_norm[0], state_s5_re[0].reshape(bs, g * p),
                                state_s5_im[0].reshape(bs, g * p), lam_re, lam_im, wb, wc, s5_d[0], wv, wgl)
    xp, ckv_p, kpe_p, k_p, vt_p = ffn(xp.reshape(bp * tp, d), 0, 1, tm_p,
                                      post=_kv_stage(bp, tp, tm_p, pos_p, *kv_args, with_heads=True))
    xs, ckv_s, kpe_s = ffn(xs, 0, 1, tm_s, post=_kv_stage(1, bs, tm_s, pos_s, *kv_args, with_heads=False))

    xp, q_p = ffn(xp, 1, 0, tm_p, post=_q_stage(bp, tp, tm_p, pos_p, *q_args))
    xs, q_s = ffn(xs, 1, 0, tm_s, post=_q_stage(1, bs, tm_s, pos_s, *q_args))
    ot = _attn_prompt(q_p, k_p.reshape(bp, tp, nh * LANES), vt_p, nh=nh, hv=hv)
    xp = ffn(xp, 1, 1, tm_p, pre=(ot, mla_w_o[0].astype(BF16)))

    qabs = _absorb(q_s, kv_w_uk, rope)
    o_lat = _attn_sample(qabs, cache_ckv, cache_kpe, page_table, ckv_s, kpe_s)
    xs = _oproj_lat(o_lat, xs, kv_w_uv, mla_w_o[0])
    xs = ffn(xs, 1, 1, tm_s)

    return (xp.reshape(bp, tp, d), xs.reshape(bs, 1, d),
            sre_p.reshape(n_a, bp, g, p), sim_p.reshape(n_a, bp, g, p),
            sre_s.reshape(n_a, bs, g, p), sim_s.reshape(n_a, bs, g, p),
            ckv_p.reshape(bp, tp, -1), kpe_p.reshape(bp, tp, rope),
            ckv_s.reshape(bs, 1, -1), kpe_s.reshape(bs, 1, rope))
```

```python
import functools
import math
from typing import Callable, NamedTuple

import jax
import jax.numpy as jnp
import numpy as np
from jax import lax
from jax.experimental import pallas as pl
from jax.experimental.pallas import tpu as pltpu

F32 = jnp.float32
BF16 = jnp.bfloat16

EPS = 1e-6
FFN_RES = 0.5
NEG_BIG = -1e30
ROPE_BASE = 10000.0

GROUP_CH = 16
LANES = 128
SUBLANES = 8
S5_SLAB = 256
S5_PITCH = 12
VT_PAD = 16
KPE_COPIES = 3
VMEM_LIMIT = 56 * 1024 * 1024


def _tile(n, pref):
    if n <= pref:
        return n
    for t in range(pref, 7, -1):
        if n % t == 0 and t % 8 == 0:
            return t
    return n


def _rms(x, gain):
    ms = jnp.mean(x * x, axis=-1, keepdims=True)
    return x * lax.rsqrt(ms + EPS) * gain


def _cparams(sem):
    return pltpu.CompilerParams(dimension_semantics=sem, vmem_limit_bytes=VMEM_LIMIT)


class _Stage(NamedTuple):
    args: tuple
    in_specs: tuple
    out_shape: tuple
    out_specs: tuple
    fn: Callable


def _ffn_body(*refs, has_pre, post):
    x_ref, g_ref, wg_ref, wu_ref, wd_ref = refs[:5]
    refs = refs[5:]
    x = x_ref[...]
    if has_pre:
        ot_ref, wo_ref = refs[:2]
        refs = refs[2:]
        x = x + lax.dot_general(ot_ref[...], wo_ref[...], (((0,), (0,)), ((), ())), preferred_element_type=F32)
    n_in = len(post.args) if post else 0
    post_in, o_ref, post_out = refs[:n_in], refs[n_in], refs[n_in + 1:]
    xn = _rms(x, g_ref[...]).astype(BF16)
    gate = jnp.dot(xn, wg_ref[...], preferred_element_type=F32)
    up = jnp.dot(xn, wu_ref[...], preferred_element_type=F32)
    hid = (gate * jax.nn.sigmoid(gate) * up).astype(BF16)
    y = x + FFN_RES * jnp.dot(hid, wd_ref[...], preferred_element_type=F32)
    o_ref[...] = y
    if post:
        post.fn(y, post_in, post_out)


def _resident(shape, index_map):
    return pl.BlockSpec(shape, index_map, pipeline_mode=pl.Buffered(1))


def _ffn(x, gain, w_gate, w_up, w_down, layer, idx, tm, *, pre=None, post=None):
    m, d = x.shape
    ff = w_gate.shape[3]
    pick = lambda i: (layer, idx, 0, 0)
    args = [x, gain.reshape(1, d), w_gate, w_up, w_down]
    in_specs = [
        pl.BlockSpec((tm, d), lambda i: (i, 0)),
        _resident((1, d), lambda i: (0, 0)),
        _resident((None, None, d, ff), pick),
        _resident((None, None, d, ff), pick),
        _resident((None, None, ff, d), pick),
    ]
    if pre is not None:
        ot, w_o = pre
        nt = ot.shape[2] // tm
        args += [ot, w_o]
        in_specs += [pl.BlockSpec((None, ot.shape[1], tm), lambda i: (i // nt, 0, i % nt)),
                     _resident(w_o.shape, lambda i: (0, 0))]
    out_shape = [jax.ShapeDtypeStruct((m, d), F32)]
    out_specs = [pl.BlockSpec((tm, d), lambda i: (i, 0))]
    if post is not None:
        args += list(post.args)
        in_specs += list(post.in_specs)
        out_shape += list(post.out_shape)
        out_specs += list(post.out_specs)
    out = pl.pallas_call(
        functools.partial(_ffn_body, has_pre=pre is not None, post=post),
        grid=(m // tm,),
        in_specs=in_specs,
        out_specs=out_specs,
        out_shape=out_shape,
        compiler_params=_cparams(("parallel",)),
        name="ffn",
    )(*args)
    return out if post is not None else out[0]


def _s5_params(a_re, a_im, log_dt, b_re, b_im, c_re, c_im):
    g, p = a_re.shape
    dt = jnp.exp(log_dt)[:, None]
    mag = jnp.exp(dt * a_re)
    lam_re, lam_im = mag * jnp.cos(dt * a_im), mag * jnp.sin(dt * a_im)
    den = a_re * a_re + a_im * a_im
    f_re = ((lam_re - 1.0) * a_re + lam_im * a_im) / den
    f_im = (lam_im * a_re - (lam_re - 1.0) * a_im) / den
    bb_re = f_re[..., None] * b_re - f_im[..., None] * b_im
    bb_im = f_re[..., None] * b_im + f_im[..., None] * b_re
    gs = S5_SLAB // GROUP_CH
    ns = g // gs
    width = 2 * gs * p
    j = np.arange(width)
    state = (j // (2 * LANES)) * LANES + j % LANES
    part = (j // LANES) % 2
    place = (np.arange(p)[None, :, None] == (state % p)[None, None, :]) & (np.arange(2)[:, None, None] == part[None, None, :])
    place = jnp.asarray(place, F32)
    own = jnp.asarray(np.arange(gs)[:, None] == (state // p)[None, :], F32)

    def pack(w_re, w_im):
        w = jnp.einsum("asgcp,apj->sgcj", jnp.stack([w_re, w_im]), place) * own[None, :, None, :]
        return w.reshape(ns, S5_SLAB, width).astype(BF16)

    to_gcp = lambda bb: jnp.transpose(bb.reshape(ns, gs, p, GROUP_CH), (0, 1, 3, 2))
    wb = pack(to_gcp(bb_re), to_gcp(bb_im))
    wc = jnp.transpose(pack(c_re.reshape(ns, gs, GROUP_CH, p), -c_im.reshape(ns, gs, GROUP_CH, p)), (0, 2, 1))
    return lam_re, lam_im, wb, wc


def _s5_prompt_body(x_ref, g_ref, wb_ref, wc_ref, lr_ref, li_ref, d_ref, wv_ref, wgl_ref,
                    o_ref, sre_ref, sim_ref, xr_ref, xi_ref, cr_ref, ci_ref, y_ref, *, tc, ns, nk):
    t_idx = pl.program_id(1)

    @pl.when(t_idx == 0)
    def _():
        cr_ref[...] = jnp.zeros_like(cr_ref)
        ci_ref[...] = jnp.zeros_like(ci_ref)

    x = x_ref[0]
    h = _rms(x, g_ref[...])
    hb = h.astype(BF16)
    for s in range(ns):
        us = hb[:, S5_SLAB * s:S5_SLAB * (s + 1)]
        for k in range(nk):
            xk = jnp.dot(us, wb_ref[s, :, 2 * LANES * k:2 * LANES * (k + 1)],
                         preferred_element_type=F32)
            xr_ref[s, pl.ds(k, tc, stride=S5_PITCH), :] = xk[:, :LANES]
            xi_ref[s, pl.ds(k, tc, stride=S5_PITCH), :] = xk[:, LANES:]

    lam = [(lr_ref[s], li_ref[s]) for s in range(ns)]

    def step(t, carry):
        row = pl.multiple_of(t * S5_PITCH, math.gcd(S5_PITCH, SUBLANES))
        out = []
        for s in range(ns):
            sr, si = carry[s]
            lr, li = lam[s]
            nr = lr * sr - li * si + xr_ref[s, pl.ds(row, nk), :]
            ni = lr * si + li * sr + xi_ref[s, pl.ds(row, nk), :]
            xr_ref[s, pl.ds(row, nk), :] = nr
            xi_ref[s, pl.ds(row, nk), :] = ni
            out.append((nr, ni))
        return tuple(out)

    init = tuple((cr_ref[s], ci_ref[s]) for s in range(ns))
    fin = lax.fori_loop(0, tc, step, init, unroll=8)
    for s in range(ns):
        cr_ref[s] = fin[s][0]
        ci_ref[s] = fin[s][1]

    for s in range(ns):
        acc = None
        for k in range(nk):
            sk = jnp.concatenate([xr_ref[s, pl.ds(k, tc, stride=S5_PITCH), :],
                                  xi_ref[s, pl.ds(k, tc, stride=S5_PITCH), :]], axis=1).astype(BF16)
            part = jnp.dot(sk, wc_ref[s, 2 * LANES * k:2 * LANES * (k + 1), :],
                           preferred_element_type=F32)
            acc = part if acc is None else acc + part
        y_ref[:, S5_SLAB * s:S5_SLAB * (s + 1)] = acc

    y = jax.nn.gelu(y_ref[...] + d_ref[...] * h, approximate=True).astype(BF16)
    val = jnp.dot(y, wv_ref[...], preferred_element_type=F32)
    gate = jnp.dot(y, wgl_ref[...], preferred_element_type=F32)
    o_ref[0] = x + val * jax.nn.sigmoid(gate)

    @pl.when(t_idx == pl.num_programs(1) - 1)
    def _():
        sre_ref[0] = cr_ref[...].reshape(ns * nk, LANES)
        sim_ref[0] = ci_ref[...].reshape(ns * nk, LANES)


def _s5_prompt(x, gain, lam_re, lam_im, wb, wc, d_skip, w_v, w_g, *, tc_pref=512):
    b, t, d = x.shape
    ns = wb.shape[0]
    nk = wb.shape[2] // (2 * LANES)
    assert nk == SUBLANES, "one token's slab state must fill whole vregs"
    tc = _tile(t, tc_pref)
    lr = lam_re.reshape(ns, nk, LANES)
    li = lam_im.reshape(ns, nk, LANES)
    const3 = lambda i, j: (0, 0, 0)
    const2 = lambda i, j: (0, 0)
    out, s_re, s_im = pl.pallas_call(
        functools.partial(_s5_prompt_body, tc=tc, ns=ns, nk=nk),
        grid=(b, t // tc),
        in_specs=[
            pl.BlockSpec((1, tc, d), lambda i, j: (i, j, 0)),
            _resident((1, d), const2),
            _resident(wb.shape, const3),
            _resident(wc.shape, const3),
            _resident(lr.shape, const3),
            _resident(li.shape, const3),
            _resident((1, d), const2),
            _resident(w_v.shape, const2),
            _resident(w_g.shape, const2),
        ],
        out_specs=[
            pl.BlockSpec((1, tc, d), lambda i, j: (i, j, 0)),
            pl.BlockSpec((1, ns * nk, LANES), lambda i, j: (i, 0, 0)),
            pl.BlockSpec((1, ns * nk, LANES), lambda i, j: (i, 0, 0)),
        ],
        out_shape=[
            jax.ShapeDtypeStruct((b, t, d), F32),
            jax.ShapeDtypeStruct((b, ns * nk, LANES), F32),
            jax.ShapeDtypeStruct((b, ns * nk, LANES), F32),
        ],
        scratch_shapes=[
            pltpu.VMEM((ns, tc * S5_PITCH, LANES), F32),
            pltpu.VMEM((ns, tc * S5_PITCH, LANES), F32),
            pltpu.VMEM((ns, nk, LANES), F32),
            pltpu.VMEM((ns, nk, LANES), F32),
            pltpu.VMEM((tc, d), F32),
        ],
        compiler_params=_cparams(("parallel", "arbitrary")),
        name="s5_prompt",
    )(x, gain.reshape(1, d), wb, wc, lr, li, d_skip.reshape(1, d), w_v, w_g)
    return out, s_re, s_im


def _s5_step_body(x_ref, g_ref, s0r_ref, s0i_ref, wb_ref, wc_ref, lr_ref, li_ref, d_ref,
                  wv_ref, wgl_ref, o_ref, sre_ref, sim_ref, y_ref, *, ns, nk):
    x = x_ref[...]
    h = _rms(x, g_ref[...])
    hb = h.astype(BF16)
    for s in range(ns):
        xs = jnp.dot(hb[:, S5_SLAB * s:S5_SLAB * (s + 1)], wb_ref[s], preferred_element_type=F32)
        acc = None
        for k in range(nk):
            col = (s * nk + k) * LANES
            lr = lr_ref[:, col:col + LANES]
            li = li_ref[:, col:col + LANES]
            s0r = s0r_ref[:, col:col + LANES]
            s0i = s0i_ref[:, col:col + LANES]
            nr = lr * s0r - li * s0i + xs[:, 2 * LANES * k:2 * LANES * k + LANES]
            ni = lr * s0i + li * s0r + xs[:, 2 * LANES * k + LANES:2 * LANES * (k + 1)]
            sre_ref[:, col:col + LANES] = nr
            sim_ref[:, col:col + LANES] = ni
            part = (jnp.dot(nr.astype(BF16), wc_ref[s, 2 * LANES * k:2 * LANES * k + LANES, :],
                            preferred_element_type=F32)
                    + jnp.dot(ni.astype(BF16), wc_ref[s, 2 * LANES * k + LANES:2 * LANES * (k + 1), :],
                              preferred_element_type=F32))
            acc = part if acc is None else acc + part
        y_ref[:, S5_SLAB * s:S5_SLAB * (s + 1)] = acc
    y = jax.nn.gelu(y_ref[...] + d_ref[...] * h, approximate=True).astype(BF16)
    val = jnp.dot(y, wv_ref[...], preferred_element_type=F32)
    gate = jnp.dot(y, wgl_ref[...], preferred_element_type=F32)
    o_ref[...] = x + val * jax.nn.sigmoid(gate)


def _s5_step(x, gain, s0_re, s0_im, lam_re, lam_im, wb, wc, d_skip, w_v, w_g):
    m, d = x.shape
    ns = wb.shape[0]
    nk = wb.shape[2] // (2 * LANES)
    nstate = s0_re.shape[1]
    args = (x, gain.reshape(1, d), s0_re, s0_im, wb, wc, lam_re.reshape(1, nstate),
            lam_im.reshape(1, nstate), d_skip.reshape(1, d), w_v, w_g)
    return pl.pallas_call(
        functools.partial(_s5_step_body, ns=ns, nk=nk),
        out_shape=[
            jax.ShapeDtypeStruct((m, d), F32),
            jax.ShapeDtypeStruct((m, nstate), F32),
            jax.ShapeDtypeStruct((m, nstate), F32),
        ],
        scratch_shapes=[pltpu.VMEM((m, d), F32)],
        compiler_params=pltpu.CompilerParams(vmem_limit_bytes=VMEM_LIMIT),
        name="s5_step",
    )(*args)


def _rope_tables(pos, half):
    inv = ROPE_BASE ** (-jnp.arange(half, dtype=F32) / half)
    ang = pos.astype(F32)[:, None] * inv[None, :]
    return jnp.cos(ang), jnp.sin(ang)


def _kv_compute(x, in_refs, out_refs, *, rank, rope, with_heads, nh):
    if with_heads:
        g_ref, w_ref, cn_ref, cos_ref, sin_ref, wuk_ref, wuvt_ref = in_refs
        c_ref, kpe_ref, k_ref, vt_ref = out_refs
    else:
        g_ref, w_ref, cn_ref, cos_ref, sin_ref = in_refs
        c_ref, kpe_ref = out_refs
    hk = _rms(x, g_ref[...]).astype(BF16)
    ck = jnp.dot(hk, w_ref[...], preferred_element_type=F32)
    c = _rms(ck[:, :rank], cn_ref[...])
    c_ref[...] = c

    pe = ck[:, rank:rank + LANES]
    rot = ck[:, rank + LANES:rank + 2 * LANES]
    r = lax.rsqrt(jnp.sum(pe * pe, axis=-1, keepdims=True) * (1.0 / (rope * KPE_COPIES)) + EPS)
    kpe_blk = (pe * cos_ref[...] + rot * sin_ref[...]) * r
    kpe_ref[...] = kpe_blk[:, :rope]
    if with_heads:
        cb = c.astype(BF16)
        nope = LANES - 2 * rope
        lane = lax.broadcasted_iota(jnp.int32, (1, LANES), 1)
        pe_mid = jnp.where(lane >= nope, kpe_blk, 0.0)
        kn = jnp.dot(cb, wuk_ref[...], preferred_element_type=F32)
        for h in range(nh):
            k_ref[:, LANES * h:LANES * (h + 1)] = (kn[:, LANES * h:LANES * (h + 1)] + pe_mid).astype(BF16)
        vt = lax.dot_general(wuvt_ref[...], cb, (((1,), (1,)), ((), ())), preferred_element_type=F32)
        hv = vt.shape[0] // nh
        hvx = hv + VT_PAD
        ones_row = (lax.broadcasted_iota(jnp.int32, (VT_PAD, vt.shape[1]), 0) == 0).astype(BF16)
        for h in range(nh):
            vt_ref[hvx * h:hvx * h + hv, :] = vt[hv * h:hv * (h + 1)].astype(BF16)
            vt_ref[hvx * h + hv:hvx * (h + 1), :] = ones_row


def _kv_stage(b, t, tm, pos, kv_in_norm, w_dkv, c_norm, kpe_norm, w_uk, w_uv, *, with_heads):
    d = w_dkv.shape[0]
    rank = c_norm.shape[0]
    rope = kpe_norm.shape[0]
    half = rope // 2
    nh, nope = w_uk.shape[1], w_uk.shape[2]
    hv = w_uv.shape[2]
    nt = t // tm
    m = b * t
    assert nope + 2 * rope == LANES and nope >= 2 * rope
    w_pe = w_dkv[:, rank:]
    w_rot = jnp.concatenate([-w_pe[:, half:], w_pe[:, :half]], axis=1)
    gap = jnp.zeros((d, nope - rope), F32)
    w_ext = jnp.concatenate([w_dkv[:, :rank], w_pe, gap, w_pe, w_pe, w_rot, gap, w_rot, w_rot],
                            axis=1).astype(BF16)
    cos, sin = _rope_tables(pos, half)
    g_rot = jnp.concatenate([kpe_norm[half:], kpe_norm[:half]])
    gcos = jnp.concatenate([cos, cos], 1) * kpe_norm[None, :]
    gsin = jnp.concatenate([sin, sin], 1) * g_rot[None, :]
    tgap = jnp.zeros((t, nope - rope), F32)
    cos_t = jnp.concatenate([gcos, tgap, gcos, gcos], axis=1)
    sin_t = jnp.concatenate([gsin, tgap, gsin, gsin], axis=1)

    const = lambda i: (0, 0)
    args = [kv_in_norm.reshape(1, d), w_ext, c_norm.reshape(1, rank), cos_t, sin_t]
    in_specs = [
        _resident((1, d), const),
        _resident(w_ext.shape, const),
        _resident((1, rank), const),
        pl.BlockSpec((tm, LANES), lambda i: (i % nt, 0)),
        pl.BlockSpec((tm, LANES), lambda i: (i % nt, 0)),
    ]
    out_specs = [pl.BlockSpec((tm, rank), lambda i: (i, 0)), pl.BlockSpec((tm, rope), lambda i: (i, 0))]
    out_shape = [jax.ShapeDtypeStruct((m, rank), F32), jax.ShapeDtypeStruct((m, rope), F32)]
    if with_heads:
        wuk = jnp.concatenate([w_uk, jnp.zeros((rank, nh, LANES - nope), F32)], axis=2)
        wuk = wuk.reshape(rank, nh * LANES).astype(BF16)
        wuvt = jnp.transpose(w_uv, (1, 2, 0)).reshape(nh * hv, rank).astype(BF16)
        in_specs += [_resident(wuk.shape, const), _resident(wuvt.shape, const)]
        args += [wuk, wuvt]
        out_specs += [pl.BlockSpec((tm, nh * LANES), lambda i: (i, 0)),
                      pl.BlockSpec((None, nh * (hv + VT_PAD), tm), lambda i: (i // nt, 0, i % nt))]
        out_shape += [jax.ShapeDtypeStruct((m, nh * LANES), BF16),
                      jax.ShapeDtypeStruct((b, nh * (hv + VT_PAD), t), BF16)]
    fn = functools.partial(_kv_compute, rank=rank, rope=rope, with_heads=with_heads, nh=nh)
    return _Stage(tuple(args), tuple(in_specs), tuple(out_shape), tuple(out_specs), fn)


def _q_compute(x, in_refs, out_refs, *, nh, nope, rope):
    g_ref, wdq_ref, qn_ref, w1t_ref, tabt_ref = in_refs
    qt_ref, = out_refs
    hb = _rms(x, g_ref[...]).astype(BF16)
    cq = jnp.dot(hb, wdq_ref[...], preferred_element_type=F32)
    cqb = _rms(cq, qn_ref[...]).astype(BF16)
    qm = lax.dot_general(w1t_ref[...], cqb, (((1,), (1,)), ((), ())), preferred_element_type=F32)
    tab_n, tab_p = tabt_ref[:nope, :], tabt_ref[nope:, :]
    for h in range(nh):
        q_n = qm[LANES * h:LANES * h + nope, :]
        q_p = qm[LANES * h + nope:LANES * (h + 1), :]
        r_n = lax.rsqrt(jnp.mean(q_n * q_n, axis=0, keepdims=True) + EPS)
        q_pe = q_p[:rope]
        r_p = lax.rsqrt(jnp.mean(q_pe * q_pe, axis=0, keepdims=True) + EPS)
        qt_ref[LANES * h:LANES * h + nope, :] = (q_n * tab_n * r_n).astype(BF16)
        qt_ref[LANES * h + nope:LANES * (h + 1), :] = (q_p * tab_p * r_p).astype(BF16)


def _q_stage(b, t, tm, pos, mix_norm, w_dq, q_norm, w_uq, qn_norm, qr_norm):
    d, qrank = w_dq.shape
    nh = w_uq.shape[1]
    nope, rope = qn_norm.shape[0], qr_norm.shape[0]
    assert nope + 2 * rope == LANES
    half = rope // 2
    scale = (nope + rope) ** -0.5 * math.log2(math.e)
    nt = t // tm
    m = b * t
    w_n, w_p = w_uq[:, :, :nope], w_uq[:, :, nope:]
    w_rot = jnp.concatenate([-w_p[:, :, half:], w_p[:, :, :half]], axis=2)
    w1t = jnp.concatenate([w_n, w_p, w_rot], axis=2).reshape(qrank, nh * LANES).T.astype(BF16)
    g_rot = jnp.concatenate([qr_norm[half:], qr_norm[:half]])
    cos, sin = _rope_tables(pos, half)
    tabt = jnp.concatenate([
        jnp.broadcast_to(qn_norm[:, None], (nope, t)),
        jnp.concatenate([cos, cos], 1).T * qr_norm[:, None],
        jnp.concatenate([sin, sin], 1).T * g_rot[:, None]], axis=0) * scale
    const = lambda i: (0, 0)
    args = (mix_norm.reshape(1, d), w_dq.astype(BF16), q_norm.reshape(1, qrank), w1t, tabt)
    in_specs = (
        _resident((1, d), const),
        _resident(w_dq.shape, const),
        _resident((1, qrank), const),
        _resident(w1t.shape, const),
        pl.BlockSpec((LANES, tm), lambda i: (0, i % nt)),
    )
    out_shape = (jax.ShapeDtypeStruct((nh * LANES, m), BF16),)
    out_specs = (pl.BlockSpec((nh * LANES, tm), lambda i: (0, i)),)
    return _Stage(args, in_specs, out_shape, out_specs, functools.partial(_q_compute, nh=nh, nope=nope, rope=rope))


def _attn_prompt_body(q_ref, k_ref, vt_ref, o_ref, *, tq, hp, hv):
    qi = pl.program_id(2)
    hvx = hv + VT_PAD
    qs = [q_ref[LANES * a:LANES * (a + 1), :] for a in range(hp)]

    def block(ki, carry, masked):
        off = pl.multiple_of(ki * tq, tq)
        ss = []
        for a in range(hp):
            k = k_ref[pl.ds(off, tq), LANES * a:LANES * (a + 1)]
            ss.append(jnp.dot(k, qs[a], preferred_element_type=F32))
        ps, stats = [], []
        for a in range(hp):
            m_i = carry[a][0]
            s = ss[a]
            if masked:
                kpos = lax.broadcasted_iota(jnp.int32, (tq, tq), 0)
                qpos = lax.broadcasted_iota(jnp.int32, (tq, tq), 1)
                s = jnp.where(kpos <= qpos, s, NEG_BIG)
            m_new = jnp.maximum(m_i, jnp.max(s, axis=0, keepdims=True))
            stats.append((m_new, jnp.exp2(m_i - m_new)))
            ps.append(jnp.exp2(s - m_new).astype(BF16))
        out = []
        for a in range(hp):
            m_new, alpha = stats[a]
            pv = jnp.dot(vt_ref[hvx * a:hvx * (a + 1), pl.ds(off, tq)], ps[a], preferred_element_type=F32)
            out.append((m_new, alpha * carry[a][1] + pv))
        return tuple(out)

    init = tuple((jnp.full((1, tq), NEG_BIG, F32), jnp.zeros((hvx, tq), F32)) for _ in range(hp))
    carry = lax.fori_loop(0, qi, lambda ki, c: block(ki, c, False), init)
    fin = block(qi, carry, True)
    for a in range(hp):
        acc = fin[a][1]
        o_ref[hv * a:hv * (a + 1), :] = (acc[:hv] / acc[hv:hv + 1]).astype(BF16)


def _attn_prompt(qt, k, vt, *, nh, hv, tq_pref=512, hp=4):
    b, t, _ = k.shape
    tq = _tile(t, tq_pref)
    nq = t // tq
    assert nh % hp == 0
    return pl.pallas_call(
        functools.partial(_attn_prompt_body, tq=tq, hp=hp, hv=hv),
        grid=(b, nh // hp, t // tq),
        in_specs=[
            pl.BlockSpec((hp * LANES, tq), lambda i, h, j: (h, i * nq + j)),
            pl.BlockSpec((None, t, hp * LANES), lambda i, h, j: (i, 0, h)),
            pl.BlockSpec((None, hp * (hv + VT_PAD), t), lambda i, h, j: (i, h, 0)),
        ],
        out_specs=pl.BlockSpec((None, hp * hv, tq), lambda i, h, j: (i, h, j)),
        out_shape=jax.ShapeDtypeStruct((b, nh * hv, t), BF16),
        compiler_params=_cparams(("parallel", "parallel", "arbitrary")),
        name="attn_prompt",
    )(qt, k, vt)


def _absorb_body(qt_ref, w_ref, o_ref):
    nh, _, width = w_ref.shape
    for h in range(nh):
        o_ref[:, width * h:width * (h + 1)] = lax.dot_general(
            qt_ref[LANES * h:LANES * (h + 1), :], w_ref[h], (((0,), (0,)), ((), ())),
            preferred_element_type=F32).astype(BF16)


def _absorb(qt, w_uk, rope):
    m = qt.shape[1]
    rank, nh, nope = w_uk.shape
    width = rank + LANES
    w = jnp.zeros((nh, LANES, width), F32)
    w = w.at[:, :nope, :rank].set(jnp.transpose(w_uk, (1, 2, 0)))
    eye = jnp.broadcast_to(jnp.eye(rope, dtype=F32), (nh, rope, rope))
    w = w.at[:, nope:nope + rope, rank:rank + rope].set(eye)
    w = w.at[:, nope + rope:nope + 2 * rope, rank:rank + rope].set(eye)
    out = pl.pallas_call(
        _absorb_body,
        out_shape=jax.ShapeDtypeStruct((m, nh * width), BF16),
        compiler_params=pltpu.CompilerParams(vmem_limit_bytes=VMEM_LIMIT),
        name="absorb_queries",
    )(qt, w.astype(BF16))
    return out.reshape(m, nh, width)


CHUNK_AHEAD = 3


def _attn_sample_body(pt_ref, q_ref, cn_ref, pn_ref, ckv_hbm, kpe_hbm, o_ref, cbuf, pbuf, sem,
                      *, npg, nch, page, rank, rope):
    b = pl.program_id(0)
    total = pl.num_programs(0) * nch

    def page_copies(chunk, slot):
        out = []
        for i in range(npg):
            pid = pt_ref[chunk * npg + i]
            out.append(pltpu.make_async_copy(ckv_hbm.at[pid], cbuf.at[slot, pl.ds(i * page, page)],
                                             sem.at[0, slot]))
            out.append(pltpu.make_async_copy(kpe_hbm.at[pid], pbuf.at[slot, :, pl.ds(i * page, page)],
                                             sem.at[1, slot]))
        return out

    @pl.when(b == 0)
    def _():
        for c in range(CHUNK_AHEAD):
            for cp in page_copies(jnp.minimum(c, total - 1), c % nch):
                cp.start()

    q_lat = q_ref[0, :, :rank]
    q_pe = q_ref[0, :, rank:rank + rope]
    nh = q_lat.shape[0]
    m_i = jnp.full((nh, 1), NEG_BIG, F32)
    l_i = jnp.zeros((nh, 1), F32)
    half = npg * page // 2

    def add_pv(accs, alpha, p, c):
        return tuple(alpha * accs[i] + jnp.dot(p[:, half * i:half * (i + 1)], c[half * i:half * (i + 1)],
                                               preferred_element_type=F32) for i in range(2))

    accs = (jnp.zeros((nh, rank), F32), jnp.zeros((nh, rank), F32))
    pending = None
    for j in range(nch):
        for cp in page_copies(b * nch + j, j):
            cp.wait()
        c = cbuf[j].astype(BF16)
        s = (lax.dot_general(q_lat, c, (((1,), (1,)), ((), ())), preferred_element_type=F32)
             + jnp.dot(q_pe, pbuf[j].astype(BF16), preferred_element_type=F32))
        if pending is not None:
            accs = add_pv(accs, *pending)
        m_new = jnp.maximum(m_i, jnp.max(s, axis=1, keepdims=True))
        alpha = jnp.exp2(m_i - m_new)
        p = jnp.exp2(s - m_new)
        l_i = alpha * l_i + jnp.sum(p, axis=1, keepdims=True)
        m_i = m_new
        pending = (alpha, p.astype(BF16), c)
        nxt = b * nch + j + CHUNK_AHEAD
        for cp in page_copies(jnp.minimum(nxt, total - 1), (j + CHUNK_AHEAD) % nch):
            cp.start()
    accs = add_pv(accs, *pending)

    @pl.when(b == pl.num_programs(0) - 1)
    def _():
        for j in range(nch - CHUNK_AHEAD, nch):
            for cp in page_copies(total - 1, (j + CHUNK_AHEAD) % nch):
                cp.wait()

    c_new = cn_ref[0].astype(BF16).astype(F32)
    p_new = pn_ref[0].astype(BF16).astype(F32)
    s_new = (jnp.sum(q_lat.astype(F32) * c_new, axis=1, keepdims=True)
             + jnp.sum(q_pe.astype(F32) * p_new, axis=1, keepdims=True))
    m_fin = jnp.maximum(m_i, s_new)
    a_old = jnp.exp2(m_i - m_fin)
    p_n = jnp.exp2(s_new - m_fin)
    l_fin = a_old * l_i + p_n
    o_new = p_n.astype(BF16).astype(F32) * c_new
    o_ref[0] = (a_old * accs[0] + (a_old * accs[1] + o_new)) / l_fin


def _attn_sample(qabs, cache_ckv, cache_kpe, page_table, c_new, kpe_new, *, npg_pref=32):
    m, nh, width = qabs.shape
    _, page, rank = cache_ckv.shape
    rope = cache_kpe.shape[2]
    n_pages = page_table.shape[1]
    npg = npg_pref if n_pages % npg_pref == 0 else n_pages
    nch = n_pages // npg
    assert nch > CHUNK_AHEAD, "a slot must not be refilled while its chunk is in use"
    kpe_t = jnp.swapaxes(cache_kpe, 1, 2)
    grid_spec = pltpu.PrefetchScalarGridSpec(
        num_scalar_prefetch=1,
        grid=(m,),
        in_specs=[
            pl.BlockSpec((1, nh, width), lambda b, pt: (b, 0, 0)),
            pl.BlockSpec((1, 1, rank), lambda b, pt: (b, 0, 0)),
            pl.BlockSpec((1, 1, rope), lambda b, pt: (b, 0, 0)),
            pl.BlockSpec(memory_space=pl.ANY),
            pl.BlockSpec(memory_space=pl.ANY),
        ],
        out_specs=pl.BlockSpec((1, nh, rank), lambda b, pt: (b, 0, 0)),
        scratch_shapes=[
            pltpu.VMEM((nch, npg * page, rank), F32),
            pltpu.VMEM((nch, rope, npg * page), F32),
            pltpu.SemaphoreType.DMA((2, nch)),
        ],
    )
    return pl.pallas_call(
        functools.partial(_attn_sample_body, npg=npg, nch=nch, page=page, rank=rank, rope=rope),
        grid_spec=grid_spec,
        out_shape=jax.ShapeDtypeStruct((m, nh, rank), F32),
        compiler_params=_cparams(("arbitrary",)),
        name="attn_sample",
    )(page_table.reshape(-1), qabs, c_new.reshape(m, 1, rank), kpe_new.reshape(m, 1, rope),
      cache_ckv, kpe_t)


def _oproj_lat_body(ol_ref, x_ref, wuv_ref, wo_ref, o_ref, *, nh, rank, hv):
    acc = x_ref[...]
    for h in range(nh):
        oh = jnp.dot(ol_ref[:, rank * h:rank * (h + 1)].astype(BF16), wuv_ref[h], preferred_element_type=F32)
        acc = acc + jnp.dot(oh.astype(BF16), wo_ref[hv * h:hv * (h + 1), :], preferred_element_type=F32)
    o_ref[...] = acc


def _oproj_lat(o_lat, x, w_uv, w_o):
    m, nh, rank = o_lat.shape
    hv = w_uv.shape[2]
    return pl.pallas_call(
        functools.partial(_oproj_lat_body, nh=nh, rank=rank, hv=hv),
        out_shape=jax.ShapeDtypeStruct(x.shape, F32),
        compiler_params=pltpu.CompilerParams(vmem_limit_bytes=VMEM_LIMIT),
        name="attn_out_sample",
    )(o_lat.reshape(m, nh * rank), x, jnp.transpose(w_uv, (1, 0, 2)).astype(BF16), w_o.astype(BF16))


def kernel(x_prompt, x_sample, state_s5_re, state_s5_im, cache_ckv, cache_kpe, page_table, ffn_norm, ffn_w_gate, ffn_w_up, ffn_w_down, mix_norm, s5_a_re, s5_a_im, s5_log_dt, s5_b_re, s5_b_im, s5_c_re, s5_c_im, s5_d, s5_w_glu_v, s5_w_glu_g, mla_w_dq, mla_q_norm, mla_w_uq, mla_qn_norm, mla_qr_norm, mla_w_o, kv_in_norm, kv_w_dkv, kv_c_norm, kv_kpe_norm, kv_w_uk, kv_w_uv):
    bp, tp, d = x_prompt.shape
    bs, ts, _ = x_sample.shape
    assert ts == 1, "the sample group decodes one token per sequence"
    depth = ffn_norm.shape[0]
    n_a = s5_a_re.shape[0]
    assert depth == 2 and n_a == 1 and mla_w_dq.shape[0] == 1, "one S5 layer followed by one MLA layer"
    g, p = s5_a_re.shape[1], s5_a_re.shape[2]
    nh, hv = kv_w_uv.shape[1], kv_w_uv.shape[2]
    rope = kv_kpe_norm.shape[0]
    past_len = page_table.shape[1] * cache_ckv.shape[1]

    wg, wu, wd = ffn_w_gate.astype(BF16), ffn_w_up.astype(BF16), ffn_w_down.astype(BF16)
    tm_p, tm_s = _tile(tp, 512), bs

    def ffn(x, layer, idx, tm, **fused):
        return _ffn(x, ffn_norm[layer, idx], wg, wu, wd, layer, idx, tm, **fused)

    xp = x_prompt.reshape(bp * tp, d)
    xs = x_sample.reshape(bs, d)
    pos_p = jnp.arange(tp, dtype=jnp.int32)
    pos_s = jnp.full((bs,), past_len, dtype=jnp.int32)
    kv_args = (kv_in_norm, kv_w_dkv, kv_c_norm, kv_kpe_norm, kv_w_uk, kv_w_uv)
    q_args = (mix_norm[1], mla_w_dq[0], mla_q_norm[0], mla_w_uq[0], mla_qn_norm[0], mla_qr_norm[0])

    lam_re, lam_im, wb, wc = _s5_params(s5_a_re[0], s5_a_im[0], s5_log_dt[0], s5_b_re[0], s5_b_im[0],
                                        s5_c_re[0], s5_c_im[0])
    wv, wgl = s5_w_glu_v[0].astype(BF16), s5_w_glu_g[0].astype(BF16)
    xp = ffn(xp, 0, 0, tm_p)
    xs = ffn(xs, 0, 0, tm_s)
    xp, sre_p, sim_p = _s5_prompt(xp.reshape(bp, tp, d), mix_norm[0], lam_re, lam_im, wb, wc, s5_d[0], wv, wgl)
    xs, sre_s, sim_s = _s5_step(xs, mix_norm[0], state_s5_re[0].reshape(bs, g * p),
                                state_s5_im[0].reshape(bs, g * p), lam_re, lam_im, wb, wc, s5_d[0], wv, wgl)
    xp, ckv_p, kpe_p, k_p, vt_p = ffn(xp.reshape(bp * tp, d), 0, 1, tm_p,
                                      post=_kv_stage(bp, tp, tm_p, pos_p, *kv_args, with_heads=True))
    xs, ckv_s, kpe_s = ffn(xs, 0, 1, tm_s, post=_kv_stage(1, bs, tm_s, pos_s, *kv_args, with_heads=False))

    xp, q_p = ffn(xp, 1, 0, tm_p, post=_q_stage(bp, tp, tm_p, pos_p, *q_args))
    xs, q_s = ffn(xs, 1, 0, tm_s, post=_q_stage(1, bs, tm_s, pos_s, *q_args))
    ot = _attn_prompt(q_p, k_p.reshape(bp, tp, nh * LANES), vt_p, nh=nh, hv=hv)
    xp = ffn(xp, 1, 1, tm_p, pre=(ot, mla_w_o[0].astype(BF16)))

    qabs = _absorb(q_s, kv_w_uk, rope)
    o_lat = _attn_sample(qabs, cache_ckv, cache_kpe, page_table, ckv_s, kpe_s)
    xs = _oproj_lat(o_lat, xs, kv_w_uv, mla_w_o[0])
    xs = ffn(xs, 1, 1, tm_s)

    return (xp.reshape(bp, tp, d), xs.reshape(bs, 1, d),
            sre_p.reshape(n_a, bp, g, p), sim_p.reshape(n_a, bp, g, p),
            sre_s.reshape(n_a, bs, g, p), sim_s.reshape(n_a, bs, g, p),
            ckv_p.reshape(bp, tp, -1), kpe_p.reshape(bp, tp, rope),
            ckv_s.reshape(bs, 1, -1), kpe_s.reshape(bs, 1, rope))
```

```python
import functools
import math
from typing import Callable, NamedTuple

import jax
import jax.numpy as jnp
import numpy as np
from jax import lax
from jax.experimental import pallas as pl
from jax.experimental.pallas import tpu as pltpu

F32 = jnp.float32
BF16 = jnp.bfloat16

EPS = 1e-6
FFN_RES = 0.5
NEG_BIG = -1e30
ROPE_BASE = 10000.0

GROUP_CH = 16
LANES = 128
SUBLANES = 8
S5_SLAB = 256
S5_PITCH = 12
VT_PAD = 16
KPE_COPIES = 3
VMEM_LIMIT = 56 * 1024 * 1024


def _tile(n, pref):
    if n <= pref:
        return n
    for t in range(pref, 7, -1):
        if n % t == 0 and t % 8 == 0:
            return t
    return n


def _rms(x, gain):
    ms = jnp.mean(x * x, axis=-1, keepdims=True)
    return x * lax.rsqrt(ms + EPS) * gain


def _cparams(sem):
    return pltpu.CompilerParams(dimension_semantics=sem, vmem_limit_bytes=VMEM_LIMIT)


class _Stage(NamedTuple):
    args: tuple
    in_specs: tuple
    out_shape: tuple
    out_specs: tuple
    fn: Callable


class _Rows(NamedTuple):
    x: jax.Array
    tm: int
    pre: tuple = None
    post: _Stage = None


def _ffn_rows(x_ref, g_ref, wg_ref, wu_ref, wd_ref, pre_refs, post, post_in, o_ref, post_out):
    x = x_ref[...]
    if pre_refs:
        ot_ref, wo_ref = pre_refs
        x = x + lax.dot_general(ot_ref[...], wo_ref[...], (((0,), (0,)), ((), ())), preferred_element_type=F32)
    xn = _rms(x, g_ref[...]).astype(BF16)
    gate = jnp.dot(xn, wg_ref[...], preferred_element_type=F32)
    up = jnp.dot(xn, wu_ref[...], preferred_element_type=F32)
    hid = (gate * jax.nn.sigmoid(gate) * up).astype(BF16)
    y = x + FFN_RES * jnp.dot(hid, wd_ref[...], preferred_element_type=F32)
    o_ref[...] = y
    if post:
        post.fn(y, post_in, post_out)


def _ffn_body(*refs, plan):
    g_ref, wg_ref, wu_ref, wd_ref = refs[:4]
    n_inputs = 4 + sum(1 + 2 * has_pre + (len(post.args) if post else 0) for _, _, has_pre, post in plan)
    ins, outs = list(refs[4:n_inputs]), list(refs[n_inputs:])
    i = pl.program_id(0)
    for first, tiles, has_pre, post in plan:
        x_ref = ins.pop(0)
        pre_refs = [ins.pop(0) for _ in range(2 * has_pre)]
        post_in = [ins.pop(0) for _ in range(len(post.args) if post else 0)]
        o_ref = outs.pop(0)
        post_out = [outs.pop(0) for _ in range(len(post.out_shape) if post else 0)]
        run = functools.partial(_ffn_rows, x_ref, g_ref, wg_ref, wu_ref, wd_ref, pre_refs, post, post_in,
                                o_ref, post_out)
        if len(plan) == 1:
            run()
        else:
            pl.when(jnp.logical_and(i >= first, i < first + tiles))(run)


def _resident(shape, index_map):
    return pl.BlockSpec(shape, index_map, pipeline_mode=pl.Buffered(1))


def _ffn(groups, gain, w_gate, w_up, w_down, layer, idx):
    d = groups[0].x.shape[1]
    ff = w_gate.shape[3]
    pick = lambda i: (layer, idx, 0, 0)
    args = [gain.reshape(1, d), w_gate, w_up, w_down]
    in_specs = [
        _resident((1, d), lambda i: (0, 0)),
        _resident((None, None, d, ff), pick),
        _resident((None, None, d, ff), pick),
        _resident((None, None, ff, d), pick),
    ]
    out_shape, out_specs, plan, first = [], [], [], 0

    def local(spec, first, tiles):
        return pl.BlockSpec(spec.block_shape, lambda i: spec.index_map(jnp.clip(i - first, 0, tiles - 1)),
                            pipeline_mode=spec.pipeline_mode)

    for rows in groups:
        m, tm = rows.x.shape[0], rows.tm
        tiles = m // tm
        g_in = [pl.BlockSpec((tm, d), lambda i: (i, 0))]
        g_args = [rows.x]
        if rows.pre is not None:
            ot, w_o = rows.pre
            nt = ot.shape[2] // tm
            g_args += [ot, w_o]
            g_in += [pl.BlockSpec((None, ot.shape[1], tm), lambda i, nt=nt: (i // nt, 0, i % nt)),
                     _resident(w_o.shape, lambda i: (0, 0))]
        g_out_shape = [jax.ShapeDtypeStruct((m, d), F32)]
        g_out = [pl.BlockSpec((tm, d), lambda i: (i, 0))]
        if rows.post is not None:
            g_args += list(rows.post.args)
            g_in += list(rows.post.in_specs)
            g_out_shape += list(rows.post.out_shape)
            g_out += list(rows.post.out_specs)
        args += g_args
        in_specs += [local(sp, first, tiles) for sp in g_in]
        out_shape += g_out_shape
        out_specs += [local(sp, first, tiles) for sp in g_out]
        plan.append((first, tiles, rows.pre is not None, rows.post))
        first += tiles
    out = pl.pallas_call(
        functools.partial(_ffn_body, plan=tuple(plan)),
        grid=(first,),
        in_specs=in_specs,
        out_specs=out_specs,
        out_shape=out_shape,
        compiler_params=_cparams(("arbitrary",)),
        name="ffn",
    )(*args)
    res, k = [], 0
    for _, _, _, post in plan:
        n = 1 + (len(post.out_shape) if post else 0)
        res.append(tuple(out[k:k + n]))
        k += n
    return res


def _s5_params(a_re, a_im, log_dt, b_re, b_im, c_re, c_im):
    g, p = a_re.shape
    dt = jnp.exp(log_dt)[:, None]
    mag = jnp.exp(dt * a_re)
    lam_re, lam_im = mag * jnp.cos(dt * a_im), mag * jnp.sin(dt * a_im)
    den = a_re * a_re + a_im * a_im
    f_re = ((lam_re - 1.0) * a_re + lam_im * a_im) / den
    f_im = (lam_im * a_re - (lam_re - 1.0) * a_im) / den
    bb_re = f_re[..., None] * b_re - f_im[..., None] * b_im
    bb_im = f_re[..., None] * b_im + f_im[..., None] * b_re
    gs = S5_SLAB // GROUP_CH
    ns = g // gs
    width = 2 * gs * p
    j = np.arange(width)
    state = (j // (2 * LANES)) * LANES + j % LANES
    part = (j // LANES) % 2
    place = (np.arange(p)[None, :, None] == (state % p)[None, None, :]) & (np.arange(2)[:, None, None] == part[None, None, :])
    place = jnp.asarray(place, F32)
    own = jnp.asarray(np.arange(gs)[:, None] == (state // p)[None, :], F32)

    def pack(w_re, w_im):
        w = jnp.einsum("asgcp,apj->sgcj", jnp.stack([w_re, w_im]), place) * own[None, :, None, :]
        return w.reshape(ns, S5_SLAB, width).astype(BF16)

    to_gcp = lambda bb: jnp.transpose(bb.reshape(ns, gs, p, GROUP_CH), (0, 1, 3, 2))
    wb = pack(to_gcp(bb_re), to_gcp(bb_im))
    wc = jnp.transpose(pack(c_re.reshape(ns, gs, GROUP_CH, p), -c_im.reshape(ns, gs, GROUP_CH, p)), (0, 2, 1))
    return lam_re, lam_im, wb, wc


def _s5_prompt_body(x_ref, g_ref, wb_ref, wc_ref, lr_ref, li_ref, d_ref, wv_ref, wgl_ref,
                    o_ref, sre_ref, sim_ref, xr_ref, xi_ref, cr_ref, ci_ref, y_ref, *, tc, ns, nk):
    t_idx = pl.program_id(1)

    @pl.when(t_idx == 0)
    def _():
        cr_ref[...] = jnp.zeros_like(cr_ref)
        ci_ref[...] = jnp.zeros_like(ci_ref)

    x = x_ref[0]
    h = _rms(x, g_ref[...])
    hb = h.astype(BF16)
    for s in range(ns):
        us = hb[:, S5_SLAB * s:S5_SLAB * (s + 1)]
        for k in range(nk):
            xk = jnp.dot(us, wb_ref[s, :, 2 * LANES * k:2 * LANES * (k + 1)],
                         preferred_element_type=F32)
            xr_ref[s, pl.ds(k, tc, stride=S5_PITCH), :] = xk[:, :LANES]
            xi_ref[s, pl.ds(k, tc, stride=S5_PITCH), :] = xk[:, LANES:]

    lam = [(lr_ref[s], li_ref[s]) for s in range(ns)]

    def step(t, carry):
        row = pl.multiple_of(t * S5_PITCH, math.gcd(S5_PITCH, SUBLANES))
        out = []
        for s in range(ns):
            sr, si = carry[s]
            lr, li = lam[s]
            nr = lr * sr - li * si + xr_ref[s, pl.ds(row, nk), :]
            ni = lr * si + li * sr + xi_ref[s, pl.ds(row, nk), :]
            xr_ref[s, pl.ds(row, nk), :] = nr
            xi_ref[s, pl.ds(row, nk), :] = ni
            out.append((nr, ni))
        return tuple(out)

    init = tuple((cr_ref[s], ci_ref[s]) for s in range(ns))
    fin = lax.fori_loop(0, tc, step, init, unroll=8)
    for s in range(ns):
        cr_ref[s] = fin[s][0]
        ci_ref[s] = fin[s][1]

    for s in range(ns):
        acc = None
        for k in range(nk):
            sk = jnp.concatenate([xr_ref[s, pl.ds(k, tc, stride=S5_PITCH), :],
                                  xi_ref[s, pl.ds(k, tc, stride=S5_PITCH), :]], axis=1).astype(BF16)
            part = jnp.dot(sk, wc_ref[s, 2 * LANES * k:2 * LANES * (k + 1), :],
                           preferred_element_type=F32)
            acc = part if acc is None else acc + part
        y_ref[:, S5_SLAB * s:S5_SLAB * (s + 1)] = acc

    y = jax.nn.gelu(y_ref[...] + d_ref[...] * h, approximate=True).astype(BF16)
    val = jnp.dot(y, wv_ref[...], preferred_element_type=F32)
    gate = jnp.dot(y, wgl_ref[...], preferred_element_type=F32)
    o_ref[0] = x + val * jax.nn.sigmoid(gate)

    @pl.when(t_idx == pl.num_programs(1) - 1)
    def _():
        sre_ref[0] = cr_ref[...].reshape(ns * nk, LANES)
        sim_ref[0] = ci_ref[...].reshape(ns * nk, LANES)


def _s5_prompt(x, gain, lam_re, lam_im, wb, wc, d_skip, w_v, w_g, *, tc_pref=512):
    b, t, d = x.shape
    ns = wb.shape[0]
    nk = wb.shape[2] // (2 * LANES)
    assert nk == SUBLANES, "one token's slab state must fill whole vregs"
    tc = _tile(t, tc_pref)
    lr = lam_re.reshape(ns, nk, LANES)
    li = lam_im.reshape(ns, nk, LANES)
    const3 = lambda i, j: (0, 0, 0)
    const2 = lambda i, j: (0, 0)
    out, s_re, s_im = pl.pallas_call(
        functools.partial(_s5_prompt_body, tc=tc, ns=ns, nk=nk),
        grid=(b, t // tc),
        in_specs=[
            pl.BlockSpec((1, tc, d), lambda i, j: (i, j, 0)),
            _resident((1, d), const2),
            _resident(wb.shape, const3),
            _resident(wc.shape, const3),
            _resident(lr.shape, const3),
            _resident(li.shape, const3),
            _resident((1, d), const2),
            _resident(w_v.shape, const2),
            _resident(w_g.shape, const2),
        ],
        out_specs=[
            pl.BlockSpec((1, tc, d), lambda i, j: (i, j, 0)),
            pl.BlockSpec((1, ns * nk, LANES), lambda i, j: (i, 0, 0)),
            pl.BlockSpec((1, ns * nk, LANES), lambda i, j: (i, 0, 0)),
        ],
        out_shape=[
            jax.ShapeDtypeStruct((b, t, d), F32),
            jax.ShapeDtypeStruct((b, ns * nk, LANES), F32),
            jax.ShapeDtypeStruct((b, ns * nk, LANES), F32),
        ],
        scratch_shapes=[
            pltpu.VMEM((ns, tc * S5_PITCH, LANES), F32),
            pltpu.VMEM((ns, tc * S5_PITCH, LANES), F32),
            pltpu.VMEM((ns, nk, LANES), F32),
            pltpu.VMEM((ns, nk, LANES), F32),
            pltpu.VMEM((tc, d), F32),
        ],
        compiler_params=_cparams(("parallel", "arbitrary")),
        name="s5_prompt",
    )(x, gain.reshape(1, d), wb, wc, lr, li, d_skip.reshape(1, d), w_v, w_g)
    return out, s_re, s_im


def _s5_step_body(x_ref, g_ref, s0r_ref, s0i_ref, wb_ref, wc_ref, lr_ref, li_ref, d_ref,
                  wv_ref, wgl_ref, o_ref, sre_ref, sim_ref, y_ref, *, ns, nk):
    x = x_ref[...]
    h = _rms(x, g_ref[...])
    hb = h.astype(BF16)
    for s in range(ns):
        xs = jnp.dot(hb[:, S5_SLAB * s:S5_SLAB * (s + 1)], wb_ref[s], preferred_element_type=F32)
        acc = None
        for k in range(nk):
            col = (s * nk + k) * LANES
            lr = lr_ref[:, col:col + LANES]
            li = li_ref[:, col:col + LANES]
            s0r = s0r_ref[:, col:col + LANES]
            s0i = s0i_ref[:, col:col + LANES]
            nr = lr * s0r - li * s0i + xs[:, 2 * LANES * k:2 * LANES * k + LANES]
            ni = lr * s0i + li * s0r + xs[:, 2 * LANES * k + LANES:2 * LANES * (k + 1)]
            sre_ref[:, col:col + LANES] = nr
            sim_ref[:, col:col + LANES] = ni
            part = (jnp.dot(nr.astype(BF16), wc_ref[s, 2 * LANES * k:2 * LANES * k + LANES, :],
                            preferred_element_type=F32)
                    + jnp.dot(ni.astype(BF16), wc_ref[s, 2 * LANES * k + LANES:2 * LANES * (k + 1), :],
                              preferred_element_type=F32))
            acc = part if acc is None else acc + part
        y_ref[:, S5_SLAB * s:S5_SLAB * (s + 1)] = acc
    y = jax.nn.gelu(y_ref[...] + d_ref[...] * h, approximate=True).astype(BF16)
    val = jnp.dot(y, wv_ref[...], preferred_element_type=F32)
    gate = jnp.dot(y, wgl_ref[...], preferred_element_type=F32)
    o_ref[...] = x + val * jax.nn.sigmoid(gate)


def _s5_step(x, gain, s0_re, s0_im, lam_re, lam_im, wb, wc, d_skip, w_v, w_g):
    m, d = x.shape
    ns = wb.shape[0]
    nk = wb.shape[2] // (2 * LANES)
    nstate = s0_re.shape[1]
    args = (x, gain.reshape(1, d), s0_re, s0_im, wb, wc, lam_re.reshape(1, nstate),
            lam_im.reshape(1, nstate), d_skip.reshape(1, d), w_v, w_g)
    return pl.pallas_call(
        functools.partial(_s5_step_body, ns=ns, nk=nk),
        out_shape=[
            jax.ShapeDtypeStruct((m, d), F32),
            jax.ShapeDtypeStruct((m, nstate), F32),
            jax.ShapeDtypeStruct((m, nstate), F32),
        ],
        scratch_shapes=[pltpu.VMEM((m, d), F32)],
        compiler_params=pltpu.CompilerParams(vmem_limit_bytes=VMEM_LIMIT),
        name="s5_step",
    )(*args)


def _rope_tables(pos, half):
    inv = ROPE_BASE ** (-jnp.arange(half, dtype=F32) / half)
    ang = pos.astype(F32)[:, None] * inv[None, :]
    return jnp.cos(ang), jnp.sin(ang)


def _kv_compute(x, in_refs, out_refs, *, rank, rope, with_heads, nh):
    if with_heads:
        g_ref, w_ref, cn_ref, cos_ref, sin_ref, wuk_ref, wuvt_ref = in_refs
        c_ref, kpe_ref, k_ref, vt_ref = out_refs
    else:
        g_ref, w_ref, cn_ref, cos_ref, sin_ref = in_refs
        c_ref, kpe_ref = out_refs
    hk = _rms(x, g_ref[...]).astype(BF16)
    ck = jnp.dot(hk, w_ref[...], preferred_element_type=F32)
    c = _rms(ck[:, :rank], cn_ref[...])
    c_ref[...] = c

    pe = ck[:, rank:rank + LANES]
    rot = ck[:, rank + LANES:rank + 2 * LANES]
    r = lax.rsqrt(jnp.sum(pe * pe, axis=-1, keepdims=True) * (1.0 / (rope * KPE_COPIES)) + EPS)
    kpe_blk = (pe * cos_ref[...] + rot * sin_ref[...]) * r
    kpe_ref[...] = kpe_blk[:, :rope]
    if with_heads:
        cb = c.astype(BF16)
        nope = LANES - 2 * rope
        lane = lax.broadcasted_iota(jnp.int32, (1, LANES), 1)
        pe_mid = jnp.where(lane >= nope, kpe_blk, 0.0)
        kn = jnp.dot(cb, wuk_ref[...], preferred_element_type=F32)
        for h in range(nh):
            k_ref[:, LANES * h:LANES * (h + 1)] = (kn[:, LANES * h:LANES * (h + 1)] + pe_mid).astype(BF16)
        vt = lax.dot_general(wuvt_ref[...], cb, (((1,), (1,)), ((), ())), preferred_element_type=F32)
        hv = vt.shape[0] // nh
        hvx = hv + VT_PAD
        ones_row = (lax.broadcasted_iota(jnp.int32, (VT_PAD, vt.shape[1]), 0) == 0).astype(BF16)
        for h in range(nh):
            vt_ref[hvx * h:hvx * h + hv, :] = vt[hv * h:hv * (h + 1)].astype(BF16)
            vt_ref[hvx * h + hv:hvx * (h + 1), :] = ones_row


def _kv_stage(b, t, tm, pos, kv_in_norm, w_dkv, c_norm, kpe_norm, w_uk, w_uv, *, with_heads):
    d = w_dkv.shape[0]
    rank = c_norm.shape[0]
    rope = kpe_norm.shape[0]
    half = rope // 2
    nh, nope = w_uk.shape[1], w_uk.shape[2]
    hv = w_uv.shape[2]
    nt = t // tm
    m = b * t
    assert nope + 2 * rope == LANES and nope >= 2 * rope
    w_pe = w_dkv[:, rank:]
    w_rot = jnp.concatenate([-w_pe[:, half:], w_pe[:, :half]], axis=1)
    gap = jnp.zeros((d, nope - rope), F32)
    w_ext = jnp.concatenate([w_dkv[:, :rank], w_pe, gap, w_pe, w_pe, w_rot, gap, w_rot, w_rot],
                            axis=1).astype(BF16)
    cos, sin = _rope_tables(pos, half)
    g_rot = jnp.concatenate([kpe_norm[half:], kpe_norm[:half]])
    gcos = jnp.concatenate([cos, cos], 1) * kpe_norm[None, :]
    gsin = jnp.concatenate([sin, sin], 1) * g_rot[None, :]
    tgap = jnp.zeros((t, nope - rope), F32)
    cos_t = jnp.concatenate([gcos, tgap, gcos, gcos], axis=1)
    sin_t = jnp.concatenate([gsin, tgap, gsin, gsin], axis=1)

    const = lambda i: (0, 0)
    args = [kv_in_norm.reshape(1, d), w_ext, c_norm.reshape(1, rank), cos_t, sin_t]
    in_specs = [
        _resident((1, d), const),
        _resident(w_ext.shape, const),
        _resident((1, rank), const),
        pl.BlockSpec((tm, LANES), lambda i: (i % nt, 0)),
        pl.BlockSpec((tm, LANES), lambda i: (i % nt, 0)),
    ]
    out_specs = [pl.BlockSpec((tm, rank), lambda i: (i, 0)), pl.BlockSpec((tm, rope), lambda i: (i, 0))]
    out_shape = [jax.ShapeDtypeStruct((m, rank), F32), jax.ShapeDtypeStruct((m, rope), F32)]
    if with_heads:
        wuk = jnp.concatenate([w_uk, jnp.zeros((rank, nh, LANES - nope), F32)], axis=2)
        wuk = wuk.reshape(rank, nh * LANES).astype(BF16)
        wuvt = jnp.transpose(w_uv, (1, 2, 0)).reshape(nh * hv, rank).astype(BF16)
        in_specs += [_resident(wuk.shape, const), _resident(wuvt.shape, const)]
        args += [wuk, wuvt]
        out_specs += [pl.BlockSpec((tm, nh * LANES), lambda i: (i, 0)),
                      pl.BlockSpec((None, nh * (hv + VT_PAD), tm), lambda i: (i // nt, 0, i % nt))]
        out_shape += [jax.ShapeDtypeStruct((m, nh * LANES), BF16),
                      jax.ShapeDtypeStruct((b, nh * (hv + VT_PAD), t), BF16)]
    fn = functools.partial(_kv_compute, rank=rank, rope=rope, with_heads=with_heads, nh=nh)
    return _Stage(tuple(args), tuple(in_specs), tuple(out_shape), tuple(out_specs), fn)


def _q_compute(x, in_refs, out_refs, *, nh, nope, rope):
    g_ref, wdq_ref, qn_ref, w1t_ref, tabt_ref = in_refs
    qt_ref, = out_refs
    hb = _rms(x, g_ref[...]).astype(BF16)
    cq = jnp.dot(hb, wdq_ref[...], preferred_element_type=F32)
    cqb = _rms(cq, qn_ref[...]).astype(BF16)
    qm = lax.dot_general(w1t_ref[...], cqb, (((1,), (1,)), ((), ())), preferred_element_type=F32)
    tab_n, tab_p = tabt_ref[:nope, :], tabt_ref[nope:, :]
    for h in range(nh):
        q_n = qm[LANES * h:LANES * h + nope, :]
        q_p = qm[LANES * h + nope:LANES * (h + 1), :]
        r_n = lax.rsqrt(jnp.mean(q_n * q_n, axis=0, keepdims=True) + EPS)
        q_pe = q_p[:rope]
        r_p = lax.rsqrt(jnp.mean(q_pe * q_pe, axis=0, keepdims=True) + EPS)
        qt_ref[LANES * h:LANES * h + nope, :] = (q_n * tab_n * r_n).astype(BF16)
        qt_ref[LANES * h + nope:LANES * (h + 1), :] = (q_p * tab_p * r_p).astype(BF16)


def _q_stage(b, t, tm, pos, mix_norm, w_dq, q_norm, w_uq, qn_norm, qr_norm):
    d, qrank = w_dq.shape
    nh = w_uq.shape[1]
    nope, rope = qn_norm.shape[0], qr_norm.shape[0]
    assert nope + 2 * rope == LANES
    half = rope // 2
    scale = (nope + rope) ** -0.5 * math.log2(math.e)
    nt = t // tm
    m = b * t
    w_n, w_p = w_uq[:, :, :nope], w_uq[:, :, nope:]
    w_rot = jnp.concatenate([-w_p[:, :, half:], w_p[:, :, :half]], axis=2)
    w1t = jnp.concatenate([w_n, w_p, w_rot], axis=2).reshape(qrank, nh * LANES).T.astype(BF16)
    g_rot = jnp.concatenate([qr_norm[half:], qr_norm[:half]])
    cos, sin = _rope_tables(pos, half)
    tabt = jnp.concatenate([
        jnp.broadcast_to(qn_norm[:, None], (nope, t)),
        jnp.concatenate([cos, cos], 1).T * qr_norm[:, None],
        jnp.concatenate([sin, sin], 1).T * g_rot[:, None]], axis=0) * scale
    const = lambda i: (0, 0)
    args = (mix_norm.reshape(1, d), w_dq.astype(BF16), q_norm.reshape(1, qrank), w1t, tabt)
    in_specs = (
        _resident((1, d), const),
        _resident(w_dq.shape, const),
        _resident((1, qrank), const),
        _resident(w1t.shape, const),
        pl.BlockSpec((LANES, tm), lambda i: (0, i % nt)),
    )
    out_shape = (jax.ShapeDtypeStruct((nh * LANES, m), BF16),)
    out_specs = (pl.BlockSpec((nh * LANES, tm), lambda i: (0, i)),)
    return _Stage(args, in_specs, out_shape, out_specs, functools.partial(_q_compute, nh=nh, nope=nope, rope=rope))


def _attn_prompt_body(q_ref, k_ref, vt_ref, o_ref, *, tq, hp, hv):
    qi = pl.program_id(2)
    hvx = hv + VT_PAD
    qs = [q_ref[LANES * a:LANES * (a + 1), :] for a in range(hp)]

    def block(ki, carry, masked):
        off = pl.multiple_of(ki * tq, tq)
        ss = []
        for a in range(hp):
            k = k_ref[pl.ds(off, tq), LANES * a:LANES * (a + 1)]
            ss.append(jnp.dot(k, qs[a], preferred_element_type=F32))
        ps, stats = [], []
        for a in range(hp):
            m_i = carry[a][0]
            s = ss[a]
            if masked:
                kpos = lax.broadcasted_iota(jnp.int32, (tq, tq), 0)
                qpos = lax.broadcasted_iota(jnp.int32, (tq, tq), 1)
                s = jnp.where(kpos <= qpos, s, NEG_BIG)
            m_new = jnp.maximum(m_i, jnp.max(s, axis=0, keepdims=True))
            stats.append((m_new, jnp.exp2(m_i - m_new)))
            ps.append(jnp.exp2(s - m_new).astype(BF16))
        out = []
        for a in range(hp):
            m_new, alpha = stats[a]
            pv = jnp.dot(vt_ref[hvx * a:hvx * (a + 1), pl.ds(off, tq)], ps[a], preferred_element_type=F32)
            out.append((m_new, alpha * carry[a][1] + pv))
        return tuple(out)

    init = tuple((jnp.full((1, tq), NEG_BIG, F32), jnp.zeros((hvx, tq), F32)) for _ in range(hp))
    carry = lax.fori_loop(0, qi, lambda ki, c: block(ki, c, False), init)
    fin = block(qi, carry, True)
    for a in range(hp):
        acc = fin[a][1]
        o_ref[hv * a:hv * (a + 1), :] = (acc[:hv] / acc[hv:hv + 1]).astype(BF16)


def _attn_prompt(qt, k, vt, *, nh, hv, tq_pref=512, hp=4):
    b, t, _ = k.shape
    tq = _tile(t, tq_pref)
    nq = t // tq
    assert nh % hp == 0
    return pl.pallas_call(
        functools.partial(_attn_prompt_body, tq=tq, hp=hp, hv=hv),
        grid=(b, nh // hp, t // tq),
        in_specs=[
            pl.BlockSpec((hp * LANES, tq), lambda i, h, j: (h, i * nq + j)),
            pl.BlockSpec((None, t, hp * LANES), lambda i, h, j: (i, 0, h)),
            pl.BlockSpec((None, hp * (hv + VT_PAD), t), lambda i, h, j: (i, h, 0)),
        ],
        out_specs=pl.BlockSpec((None, hp * hv, tq), lambda i, h, j: (i, h, j)),
        out_shape=jax.ShapeDtypeStruct((b, nh * hv, t), BF16),
        compiler_params=_cparams(("parallel", "parallel", "arbitrary")),
        name="attn_prompt",
    )(qt, k, vt)


def _absorb_body(qt_ref, w_ref, o_ref):
    nh, _, width = w_ref.shape
    for h in range(nh):
        o_ref[:, width * h:width * (h + 1)] = lax.dot_general(
            qt_ref[LANES * h:LANES * (h + 1), :], w_ref[h], (((0,), (0,)), ((), ())),
            preferred_element_type=F32).astype(BF16)


def _absorb(qt, w_uk, rope):
    m = qt.shape[1]
    rank, nh, nope = w_uk.shape
    width = rank + LANES
    w = jnp.zeros((nh, LANES, width), F32)
    w = w.at[:, :nope, :rank].set(jnp.transpose(w_uk, (1, 2, 0)))
    eye = jnp.broadcast_to(jnp.eye(rope, dtype=F32), (nh, rope, rope))
    w = w.at[:, nope:nope + rope, rank:rank + rope].set(eye)
    w = w.at[:, nope + rope:nope + 2 * rope, rank:rank + rope].set(eye)
    out = pl.pallas_call(
        _absorb_body,
        out_shape=jax.ShapeDtypeStruct((m, nh * width), BF16),
        compiler_params=pltpu.CompilerParams(vmem_limit_bytes=VMEM_LIMIT),
        name="absorb_queries",
    )(qt, w.astype(BF16))
    return out.reshape(m, nh, width)


CHUNK_AHEAD = 3


def _attn_sample_body(pt_ref, q_ref, cn_ref, pn_ref, ckv_hbm, kpe_hbm, o_ref, cbuf, pbuf, sem,
                      *, npg, nch, page, rank, rope):
    b = pl.program_id(0)
    total = pl.num_programs(0) * nch

    def page_copies(chunk, slot):
        out = []
        for i in range(npg):
            pid = pt_ref[chunk * npg + i]
            out.append(pltpu.make_async_copy(ckv_hbm.at[pid], cbuf.at[slot, pl.ds(i * page, page)],
                                             sem.at[0, slot]))
            out.append(pltpu.make_async_copy(kpe_hbm.at[pid], pbuf.at[slot, :, pl.ds(i * page, page)],
                                             sem.at[1, slot]))
        return out

    @pl.when(b == 0)
    def _():
        for c in range(CHUNK_AHEAD):
            for cp in page_copies(jnp.minimum(c, total - 1), c % nch):
                cp.start()

    q_lat = q_ref[0, :, :rank]
    q_pe = q_ref[0, :, rank:rank + rope]
    nh = q_lat.shape[0]
    m_i = jnp.full((nh, 1), NEG_BIG, F32)
    l_i = jnp.zeros((nh, 1), F32)
    half = npg * page // 2

    def add_pv(accs, alpha, p, c):
        return tuple(alpha * accs[i] + jnp.dot(p[:, half * i:half * (i + 1)], c[half * i:half * (i + 1)],
                                               preferred_element_type=F32) for i in range(2))

    accs = (jnp.zeros((nh, rank), F32), jnp.zeros((nh, rank), F32))
    pending = None
    for j in range(nch):
        for cp in page_copies(b * nch + j, j):
            cp.wait()
        c = cbuf[j].astype(BF16)
        s = (lax.dot_general(q_lat, c, (((1,), (1,)), ((), ())), preferred_element_type=F32)
             + jnp.dot(q_pe, pbuf[j].astype(BF16), preferred_element_type=F32))
        if pending is not None:
            accs = add_pv(accs, *pending)
        m_new = jnp.maximum(m_i, jnp.max(s, axis=1, keepdims=True))
        alpha = jnp.exp2(m_i - m_new)
        p = jnp.exp2(s - m_new)
        l_i = alpha * l_i + jnp.sum(p, axis=1, keepdims=True)
        m_i = m_new
        pending = (alpha, p.astype(BF16), c)
        nxt = b * nch + j + CHUNK_AHEAD
        for cp in page_copies(jnp.minimum(nxt, total - 1), (j + CHUNK_AHEAD) % nch):
            cp.start()
    accs = add_pv(accs, *pending)

    @pl.when(b == pl.num_programs(0) - 1)
    def _():
        for j in range(nch - CHUNK_AHEAD, nch):
            for cp in page_copies(total - 1, (j + CHUNK_AHEAD) % nch):
                cp.wait()

    c_new = cn_ref[0].astype(BF16).astype(F32)
    p_new = pn_ref[0].astype(BF16).astype(F32)
    s_new = (jnp.sum(q_lat.astype(F32) * c_new, axis=1, keepdims=True)
             + jnp.sum(q_pe.astype(F32) * p_new, axis=1, keepdims=True))
    m_fin = jnp.maximum(m_i, s_new)
    a_old = jnp.exp2(m_i - m_fin)
    p_n = jnp.exp2(s_new - m_fin)
    l_fin = a_old * l_i + p_n
    o_new = p_n.astype(BF16).astype(F32) * c_new
    o_ref[0] = (a_old * accs[0] + (a_old * accs[1] + o_new)) / l_fin


def _attn_sample(qabs, cache_ckv, cache_kpe, page_table, c_new, kpe_new, *, npg_pref=32):
    m, nh, width = qabs.shape
    _, page, rank = cache_ckv.shape
    rope = cache_kpe.shape[2]
    n_pages = page_table.shape[1]
    npg = npg_pref if n_pages % npg_pref == 0 else n_pages
    nch = n_pages // npg
    assert nch > CHUNK_AHEAD, "a slot must not be refilled while its chunk is in use"
    kpe_t = jnp.swapaxes(cache_kpe, 1, 2)
    grid_spec = pltpu.PrefetchScalarGridSpec(
        num_scalar_prefetch=1,
        grid=(m,),
        in_specs=[
            pl.BlockSpec((1, nh, width), lambda b, pt: (b, 0, 0)),
            pl.BlockSpec((1, 1, rank), lambda b, pt: (b, 0, 0)),
            pl.BlockSpec((1, 1, rope), lambda b, pt: (b, 0, 0)),
            pl.BlockSpec(memory_space=pl.ANY),
            pl.BlockSpec(memory_space=pl.ANY),
        ],
        out_specs=pl.BlockSpec((1, nh, rank), lambda b, pt: (b, 0, 0)),
        scratch_shapes=[
            pltpu.VMEM((nch, npg * page, rank), F32),
            pltpu.VMEM((nch, rope, npg * page), F32),
            pltpu.SemaphoreType.DMA((2, nch)),
        ],
    )
    return pl.pallas_call(
        functools.partial(_attn_sample_body, npg=npg, nch=nch, page=page, rank=rank, rope=rope),
        grid_spec=grid_spec,
        out_shape=jax.ShapeDtypeStruct((m, nh, rank), F32),
        compiler_params=_cparams(("arbitrary",)),
        name="attn_sample",
    )(page_table.reshape(-1), qabs, c_new.reshape(m, 1, rank), kpe_new.reshape(m, 1, rope),
      cache_ckv, kpe_t)


def _oproj_lat_body(ol_ref, x_ref, wuv_ref, wo_ref, o_ref, *, nh, rank, hv):
    acc = x_ref[...]
    for h in range(nh):
        oh = jnp.dot(ol_ref[:, rank * h:rank * (h + 1)].astype(BF16), wuv_ref[h], preferred_element_type=F32)
        acc = acc + jnp.dot(oh.astype(BF16), wo_ref[hv * h:hv * (h + 1), :], preferred_element_type=F32)
    o_ref[...] = acc


def _oproj_lat(o_lat, x, w_uv, w_o):
    m, nh, rank = o_lat.shape
    hv = w_uv.shape[2]
    return pl.pallas_call(
        functools.partial(_oproj_lat_body, nh=nh, rank=rank, hv=hv),
        out_shape=jax.ShapeDtypeStruct(x.shape, F32),
        compiler_params=pltpu.CompilerParams(vmem_limit_bytes=VMEM_LIMIT),
        name="attn_out_sample",
    )(o_lat.reshape(m, nh * rank), x, jnp.transpose(w_uv, (1, 0, 2)).astype(BF16), w_o.astype(BF16))


def kernel(x_prompt, x_sample, state_s5_re, state_s5_im, cache_ckv, cache_kpe, page_table, ffn_norm, ffn_w_gate, ffn_w_up, ffn_w_down, mix_norm, s5_a_re, s5_a_im, s5_log_dt, s5_b_re, s5_b_im, s5_c_re, s5_c_im, s5_d, s5_w_glu_v, s5_w_glu_g, mla_w_dq, mla_q_norm, mla_w_uq, mla_qn_norm, mla_qr_norm, mla_w_o, kv_in_norm, kv_w_dkv, kv_c_norm, kv_kpe_norm, kv_w_uk, kv_w_uv):
    bp, tp, d = x_prompt.shape
    bs, ts, _ = x_sample.shape
    assert ts == 1, "the sample group decodes one token per sequence"
    depth = ffn_norm.shape[0]
    n_a = s5_a_re.shape[0]
    assert depth == 2 and n_a == 1 and mla_w_dq.shape[0] == 1, "one S5 layer followed by one MLA layer"
    g, p = s5_a_re.shape[1], s5_a_re.shape[2]
    nh, hv = kv_w_uv.shape[1], kv_w_uv.shape[2]
    rope = kv_kpe_norm.shape[0]
    past_len = page_table.shape[1] * cache_ckv.shape[1]

    wg, wu, wd = ffn_w_gate.astype(BF16), ffn_w_up.astype(BF16), ffn_w_down.astype(BF16)
    tm_p, tm_s = _tile(tp, 512), bs

    def ffn(layer, idx, *groups):
        return _ffn(groups, ffn_norm[layer, idx], wg, wu, wd, layer, idx)

    xp = x_prompt.reshape(bp * tp, d)
    xs = x_sample.reshape(bs, d)
    pos_p = jnp.arange(tp, dtype=jnp.int32)
    pos_s = jnp.full((bs,), past_len, dtype=jnp.int32)
    kv_args = (kv_in_norm, kv_w_dkv, kv_c_norm, kv_kpe_norm, kv_w_uk, kv_w_uv)
    q_args = (mix_norm[1], mla_w_dq[0], mla_q_norm[0], mla_w_uq[0], mla_qn_norm[0], mla_qr_norm[0])

    lam_re, lam_im, wb, wc = _s5_params(s5_a_re[0], s5_a_im[0], s5_log_dt[0], s5_b_re[0], s5_b_im[0],
                                        s5_c_re[0], s5_c_im[0])
    wv, wgl = s5_w_glu_v[0].astype(BF16), s5_w_glu_g[0].astype(BF16)
    (xp,), (xs,) = ffn(0, 0, _Rows(xp, tm_p), _Rows(xs, tm_s))
    xp, sre_p, sim_p = _s5_prompt(xp.reshape(bp, tp, d), mix_norm[0], lam_re, lam_im, wb, wc, s5_d[0], wv, wgl)
    xs, sre_s, sim_s = _s5_step(xs, mix_norm[0], state_s5_re[0].reshape(bs, g * p),
                                state_s5_im[0].reshape(bs, g * p), lam_re, lam_im, wb, wc, s5_d[0], wv, wgl)
    (xp, ckv_p, kpe_p, k_p, vt_p), (xs, ckv_s, kpe_s) = ffn(
        0, 1,
        _Rows(xp.reshape(bp * tp, d), tm_p, post=_kv_stage(bp, tp, tm_p, pos_p, *kv_args, with_heads=True)),
        _Rows(xs, tm_s, post=_kv_stage(1, bs, tm_s, pos_s, *kv_args, with_heads=False)))

    (xp, q_p), (xs, q_s) = ffn(1, 0,
                               _Rows(xp, tm_p, post=_q_stage(bp, tp, tm_p, pos_p, *q_args)),
                               _Rows(xs, tm_s, post=_q_stage(1, bs, tm_s, pos_s, *q_args)))
    ot = _attn_prompt(q_p, k_p.reshape(bp, tp, nh * LANES), vt_p, nh=nh, hv=hv)
    qabs = _absorb(q_s, kv_w_uk, rope)
    o_lat = _attn_sample(qabs, cache_ckv, cache_kpe, page_table, ckv_s, kpe_s)
    xs = _oproj_lat(o_lat, xs, kv_w_uv, mla_w_o[0])
    (xp,), (xs,) = ffn(1, 1, _Rows(xp, tm_p, pre=(ot, mla_w_o[0].astype(BF16))), _Rows(xs, tm_s))

    return (xp.reshape(bp, tp, d), xs.reshape(bs, 1, d),
            sre_p.reshape(n_a, bp, g, p), sim_p.reshape(n_a, bp, g, p),
            sre_s.reshape(n_a, bs, g, p), sim_s.reshape(n_a, bs, g, p),
            ckv_p.reshape(bp, tp, -1), kpe_p.reshape(bp, tp, rope),
            ckv_s.reshape(bs, 1, -1), kpe_s.reshape(bs, 1, rope))
```

```python
import functools
import math
from typing import Callable, NamedTuple

import jax
import jax.numpy as jnp
import numpy as np
from jax import lax
from jax.experimental import pallas as pl
from jax.experimental.pallas import tpu as pltpu

F32 = jnp.float32
BF16 = jnp.bfloat16

EPS = 1e-6
FFN_RES = 0.5
NEG_BIG = -1e30
ROPE_BASE = 10000.0

GROUP_CH = 16
LANES = 128
SUBLANES = 8
S5_SLAB = 256
S5_PITCH = 12
VT_PAD = 16
KPE_COPIES = 3
VMEM_LIMIT = 56 * 1024 * 1024


def _tile(n, pref):
    if n <= pref:
        return n
    for t in range(pref, 7, -1):
        if n % t == 0 and t % 8 == 0:
            return t
    return n


def _rms(x, gain):
    ms = jnp.mean(x * x, axis=-1, keepdims=True)
    return x * lax.rsqrt(ms + EPS) * gain


def _cparams(sem):
    return pltpu.CompilerParams(dimension_semantics=sem, vmem_limit_bytes=VMEM_LIMIT)


class _Stage(NamedTuple):
    args: tuple
    in_specs: tuple
    out_shape: tuple
    out_specs: tuple
    fn: Callable


class _Rows(NamedTuple):
    x: jax.Array
    tm: int
    pre: tuple = None
    post: _Stage = None


def _ffn_rows(x_ref, g_ref, wg_ref, wu_ref, wd_ref, pre_refs, post, post_in, o_ref, post_out):
    x = x_ref[...]
    if pre_refs:
        ot_ref, wo_ref = pre_refs
        x = x + lax.dot_general(ot_ref[...], wo_ref[...], (((0,), (0,)), ((), ())), preferred_element_type=F32)
    xn = _rms(x, g_ref[...]).astype(BF16)
    gate = jnp.dot(xn, wg_ref[...], preferred_element_type=F32)
    up = jnp.dot(xn, wu_ref[...], preferred_element_type=F32)
    hid = (gate * jax.nn.sigmoid(gate) * up).astype(BF16)
    y = x + FFN_RES * jnp.dot(hid, wd_ref[...], preferred_element_type=F32)
    o_ref[...] = y
    if post:
        post.fn(y, post_in, post_out)


def _ffn_body(*refs, plan):
    g_ref, wg_ref, wu_ref, wd_ref = refs[:4]
    n_inputs = 4 + sum(1 + 2 * has_pre + (len(post.args) if post else 0) for _, _, has_pre, post in plan)
    ins, outs = list(refs[4:n_inputs]), list(refs[n_inputs:])
    i = pl.program_id(0)
    for first, tiles, has_pre, post in plan:
        x_ref = ins.pop(0)
        pre_refs = [ins.pop(0) for _ in range(2 * has_pre)]
        post_in = [ins.pop(0) for _ in range(len(post.args) if post else 0)]
        o_ref = outs.pop(0)
        post_out = [outs.pop(0) for _ in range(len(post.out_shape) if post else 0)]
        run = functools.partial(_ffn_rows, x_ref, g_ref, wg_ref, wu_ref, wd_ref, pre_refs, post, post_in,
                                o_ref, post_out)
        if len(plan) == 1:
            run()
        else:
            pl.when(jnp.logical_and(i >= first, i < first + tiles))(run)


def _resident(shape, index_map):
    return pl.BlockSpec(shape, index_map, pipeline_mode=pl.Buffered(1))


def _ffn(groups, gain, w_gate, w_up, w_down, layer, idx):
    d = groups[0].x.shape[1]
    ff = w_gate.shape[3]
    pick = lambda i: (layer, idx, 0, 0)
    args = [gain.reshape(1, d), w_gate, w_up, w_down]
    in_specs = [
        _resident((1, d), lambda i: (0, 0)),
        _resident((None, None, d, ff), pick),
        _resident((None, None, d, ff), pick),
        _resident((None, None, ff, d), pick),
    ]
    out_shape, out_specs, plan, first = [], [], [], 0

    def local(spec, first, tiles):
        return pl.BlockSpec(spec.block_shape, lambda i: spec.index_map(jnp.clip(i - first, 0, tiles - 1)),
                            pipeline_mode=spec.pipeline_mode)

    for rows in groups:
        m, tm = rows.x.shape[0], rows.tm
        tiles = m // tm
        g_in = [pl.BlockSpec((tm, d), lambda i: (i, 0))]
        g_args = [rows.x]
        if rows.pre is not None:
            ot, w_o = rows.pre
            nt = ot.shape[2] // tm
            g_args += [ot, w_o]
            g_in += [pl.BlockSpec((None, ot.shape[1], tm), lambda i, nt=nt: (i // nt, 0, i % nt)),
                     _resident(w_o.shape, lambda i: (0, 0))]
        g_out_shape = [jax.ShapeDtypeStruct((m, d), F32)]
        g_out = [pl.BlockSpec((tm, d), lambda i: (i, 0))]
        if rows.post is not None:
            g_args += list(rows.post.args)
            g_in += list(rows.post.in_specs)
            g_out_shape += list(rows.post.out_shape)
            g_out += list(rows.post.out_specs)
        args += g_args
        in_specs += [local(sp, first, tiles) for sp in g_in]
        out_shape += g_out_shape
        out_specs += [local(sp, first, tiles) for sp in g_out]
        plan.append((first, tiles, rows.pre is not None, rows.post))
        first += tiles
    out = pl.pallas_call(
        functools.partial(_ffn_body, plan=tuple(plan)),
        grid=(first,),
        in_specs=in_specs,
        out_specs=out_specs,
        out_shape=out_shape,
        compiler_params=_cparams(("arbitrary",)),
        name="ffn",
    )(*args)
    res, k = [], 0
    for _, _, _, post in plan:
        n = 1 + (len(post.out_shape) if post else 0)
        res.append(tuple(out[k:k + n]))
        k += n
    return res


def _s5_params(a_re, a_im, log_dt, b_re, b_im, c_re, c_im):
    g, p = a_re.shape
    dt = jnp.exp(log_dt)[:, None]
    mag = jnp.exp(dt * a_re)
    lam_re, lam_im = mag * jnp.cos(dt * a_im), mag * jnp.sin(dt * a_im)
    den = a_re * a_re + a_im * a_im
    f_re = ((lam_re - 1.0) * a_re + lam_im * a_im) / den
    f_im = (lam_im * a_re - (lam_re - 1.0) * a_im) / den
    bb_re = f_re[..., None] * b_re - f_im[..., None] * b_im
    bb_im = f_re[..., None] * b_im + f_im[..., None] * b_re
    gs = S5_SLAB // GROUP_CH
    ns = g // gs
    width = 2 * gs * p
    j = np.arange(width)
    state = (j // (2 * LANES)) * LANES + j % LANES
    part = (j // LANES) % 2
    place = (np.arange(p)[None, :, None] == (state % p)[None, None, :]) & (np.arange(2)[:, None, None] == part[None, None, :])
    place = jnp.asarray(place, F32)
    own = jnp.asarray(np.arange(gs)[:, None] == (state // p)[None, :], F32)

    def pack(w_re, w_im):
        w = jnp.einsum("asgcp,apj->sgcj", jnp.stack([w_re, w_im]), place) * own[None, :, None, :]
        return w.reshape(ns, S5_SLAB, width).astype(BF16)

    to_gcp = lambda bb: jnp.transpose(bb.reshape(ns, gs, p, GROUP_CH), (0, 1, 3, 2))
    wb = pack(to_gcp(bb_re), to_gcp(bb_im))
    wc = jnp.transpose(pack(c_re.reshape(ns, gs, GROUP_CH, p), -c_im.reshape(ns, gs, GROUP_CH, p)), (0, 2, 1))
    return lam_re, lam_im, wb, wc


def _s5_prompt_body(x_ref, g_ref, wb_ref, wc_ref, lr_ref, li_ref, d_ref, wv_ref, wgl_ref,
                    o_ref, sre_ref, sim_ref, xr_ref, xi_ref, cr_ref, ci_ref, y_ref, *, tc, ns, nk):
    t_idx = pl.program_id(1)

    @pl.when(t_idx == 0)
    def _():
        cr_ref[...] = jnp.zeros_like(cr_ref)
        ci_ref[...] = jnp.zeros_like(ci_ref)

    x = x_ref[0]
    h = _rms(x, g_ref[...])
    hb = h.astype(BF16)
    for s in range(ns):
        us = hb[:, S5_SLAB * s:S5_SLAB * (s + 1)]
        for k in range(nk):
            xk = jnp.dot(us, wb_ref[s, :, 2 * LANES * k:2 * LANES * (k + 1)],
                         preferred_element_type=F32)
            xr_ref[s, pl.ds(k, tc, stride=S5_PITCH), :] = xk[:, :LANES]
            xi_ref[s, pl.ds(k, tc, stride=S5_PITCH), :] = xk[:, LANES:]

    lam = [(lr_ref[s], li_ref[s]) for s in range(ns)]

    def step(t, carry):
        row = pl.multiple_of(t * S5_PITCH, math.gcd(S5_PITCH, SUBLANES))
        out = []
        for s in range(ns):
            sr, si = carry[s]
            lr, li = lam[s]
            nr = lr * sr - li * si + xr_ref[s, pl.ds(row, nk), :]
            ni = lr * si + li * sr + xi_ref[s, pl.ds(row, nk), :]
            xr_ref[s, pl.ds(row, nk), :] = nr
            xi_ref[s, pl.ds(row, nk), :] = ni
            out.append((nr, ni))
        return tuple(out)

    init = tuple((cr_ref[s], ci_ref[s]) for s in range(ns))
    fin = lax.fori_loop(0, tc, step, init, unroll=8)
    for s in range(ns):
        cr_ref[s] = fin[s][0]
        ci_ref[s] = fin[s][1]

    for s in range(ns):
        acc = None
        for k in range(nk):
            sk = jnp.concatenate([xr_ref[s, pl.ds(k, tc, stride=S5_PITCH), :],
                                  xi_ref[s, pl.ds(k, tc, stride=S5_PITCH), :]], axis=1).astype(BF16)
            part = jnp.dot(sk, wc_ref[s, 2 * LANES * k:2 * LANES * (k + 1), :],
                           preferred_element_type=F32)
            acc = part if acc is None else acc + part
        y_ref[:, S5_SLAB * s:S5_SLAB * (s + 1)] = acc

    y = jax.nn.gelu(y_ref[...] + d_ref[...] * h, approximate=True).astype(BF16)
    val = jnp.dot(y, wv_ref[...], preferred_element_type=F32)
    gate = jnp.dot(y, wgl_ref[...], preferred_element_type=F32)
    o_ref[0] = x + val * jax.nn.sigmoid(gate)

    @pl.when(t_idx == pl.num_programs(1) - 1)
    def _():
        sre_ref[0] = cr_ref[...].reshape(ns * nk, LANES)
        sim_ref[0] = ci_ref[...].reshape(ns * nk, LANES)


def _s5_prompt(x, gain, lam_re, lam_im, wb, wc, d_skip, w_v, w_g, *, tc_pref=512):
    b, t, d = x.shape
    ns = wb.shape[0]
    nk = wb.shape[2] // (2 * LANES)
    assert nk == SUBLANES, "one token's slab state must fill whole vregs"
    tc = _tile(t, tc_pref)
    lr = lam_re.reshape(ns, nk, LANES)
    li = lam_im.reshape(ns, nk, LANES)
    const3 = lambda i, j: (0, 0, 0)
    const2 = lambda i, j: (0, 0)
    out, s_re, s_im = pl.pallas_call(
        functools.partial(_s5_prompt_body, tc=tc, ns=ns, nk=nk),
        grid=(b, t // tc),
        in_specs=[
            pl.BlockSpec((1, tc, d), lambda i, j: (i, j, 0)),
            _resident((1, d), const2),
            _resident(wb.shape, const3),
            _resident(wc.shape, const3),
            _resident(lr.shape, const3),
            _resident(li.shape, const3),
            _resident((1, d), const2),
            _resident(w_v.shape, const2),
            _resident(w_g.shape, const2),
        ],
        out_specs=[
            pl.BlockSpec((1, tc, d), lambda i, j: (i, j, 0)),
            pl.BlockSpec((1, ns * nk, LANES), lambda i, j: (i, 0, 0)),
            pl.BlockSpec((1, ns * nk, LANES), lambda i, j: (i, 0, 0)),
        ],
        out_shape=[
            jax.ShapeDtypeStruct((b, t, d), F32),
            jax.ShapeDtypeStruct((b, ns * nk, LANES), F32),
            jax.ShapeDtypeStruct((b, ns * nk, LANES), F32),
        ],
        scratch_shapes=[
            pltpu.VMEM((ns, tc * S5_PITCH, LANES), F32),
            pltpu.VMEM((ns, tc * S5_PITCH, LANES), F32),
            pltpu.VMEM((ns, nk, LANES), F32),
            pltpu.VMEM((ns, nk, LANES), F32),
            pltpu.VMEM((tc, d), F32),
        ],
        compiler_params=_cparams(("parallel", "arbitrary")),
        name="s5_prompt",
    )(x, gain.reshape(1, d), wb, wc, lr, li, d_skip.reshape(1, d), w_v, w_g)
    return out, s_re, s_im


def _s5_step_body(x_ref, g_ref, s0r_ref, s0i_ref, wb_ref, wc_ref, lr_ref, li_ref, d_ref,
                  wv_ref, wgl_ref, o_ref, sre_ref, sim_ref, y_ref, *, ns, nk):
    x = x_ref[...]
    h = _rms(x, g_ref[...])
    hb = h.astype(BF16)
    for s in range(ns):
        xs = jnp.dot(hb[:, S5_SLAB * s:S5_SLAB * (s + 1)], wb_ref[s], preferred_element_type=F32)
        acc = None
        for k in range(nk):
            col = (s * nk + k) * LANES
            lr = lr_ref[:, col:col + LANES]
            li = li_ref[:, col:col + LANES]
            s0r = s0r_ref[:, col:col + LANES]
            s0i = s0i_ref[:, col:col + LANES]
            nr = lr * s0r - li * s0i + xs[:, 2 * LANES * k:2 * LANES * k + LANES]
            ni = lr * s0i + li * s0r + xs[:, 2 * LANES * k + LANES:2 * LANES * (k + 1)]
            sre_ref[:, col:col + LANES] = nr
            sim_ref[:, col:col + LANES] = ni
            part = (jnp.dot(nr.astype(BF16), wc_ref[s, 2 * LANES * k:2 * LANES * k + LANES, :],
                            preferred_element_type=F32)
                    + jnp.dot(ni.astype(BF16), wc_ref[s, 2 * LANES * k + LANES:2 * LANES * (k + 1), :],
                              preferred_element_type=F32))
            acc = part if acc is None else acc + part
        y_ref[:, S5_SLAB * s:S5_SLAB * (s + 1)] = acc
    y = jax.nn.gelu(y_ref[...] + d_ref[...] * h, approximate=True).astype(BF16)
    val = jnp.dot(y, wv_ref[...], preferred_element_type=F32)
    gate = jnp.dot(y, wgl_ref[...], preferred_element_type=F32)
    o_ref[...] = x + val * jax.nn.sigmoid(gate)


def _s5_step(x, gain, s0_re, s0_im, lam_re, lam_im, wb, wc, d_skip, w_v, w_g):
    m, d = x.shape
    ns = wb.shape[0]
    nk = wb.shape[2] // (2 * LANES)
    nstate = s0_re.shape[1]
    args = (x, gain.reshape(1, d), s0_re, s0_im, wb, wc, lam_re.reshape(1, nstate),
            lam_im.reshape(1, nstate), d_skip.reshape(1, d), w_v, w_g)
    return pl.pallas_call(
        functools.partial(_s5_step_body, ns=ns, nk=nk),
        out_shape=[
            jax.ShapeDtypeStruct((m, d), F32),
            jax.ShapeDtypeStruct((m, nstate), F32),
            jax.ShapeDtypeStruct((m, nstate), F32),
        ],
        scratch_shapes=[pltpu.VMEM((m, d), F32)],
        compiler_params=pltpu.CompilerParams(vmem_limit_bytes=VMEM_LIMIT),
        name="s5_step",
    )(*args)


def _rope_tables(pos, half):
    inv = ROPE_BASE ** (-jnp.arange(half, dtype=F32) / half)
    ang = pos.astype(F32)[:, None] * inv[None, :]
    return jnp.cos(ang), jnp.sin(ang)


def _kv_compute(x, in_refs, out_refs, *, rank, rope, with_heads, nh):
    if with_heads:
        g_ref, w_ref, cn_ref, cos_ref, sin_ref, wuk_ref, wuvt_ref = in_refs
        c_ref, kpe_ref, k_ref, vt_ref = out_refs
    else:
        g_ref, w_ref, cn_ref, cos_ref, sin_ref = in_refs
        c_ref, kpe_ref = out_refs
    hk = _rms(x, g_ref[...]).astype(BF16)
    ck = jnp.dot(hk, w_ref[...], preferred_element_type=F32)
    c = _rms(ck[:, :rank], cn_ref[...])
    c_ref[...] = c

    pe = ck[:, rank:rank + LANES]
    rot = ck[:, rank + LANES:rank + 2 * LANES]
    r = lax.rsqrt(jnp.sum(pe * pe, axis=-1, keepdims=True) * (1.0 / (rope * KPE_COPIES)) + EPS)
    kpe_blk = (pe * cos_ref[...] + rot * sin_ref[...]) * r
    kpe_ref[...] = kpe_blk[:, :rope]
    if with_heads:
        cb = c.astype(BF16)
        nope = LANES - 2 * rope
        lane = lax.broadcasted_iota(jnp.int32, (1, LANES), 1)
        pe_mid = jnp.where(lane >= nope, kpe_blk, 0.0)
        kn = jnp.dot(cb, wuk_ref[...], preferred_element_type=F32)
        for h in range(nh):
            k_ref[:, LANES * h:LANES * (h + 1)] = (kn[:, LANES * h:LANES * (h + 1)] + pe_mid).astype(BF16)
        vt = lax.dot_general(wuvt_ref[...], cb, (((1,), (1,)), ((), ())), preferred_element_type=F32)
        hv = vt.shape[0] // nh
        hvx = hv + VT_PAD
        ones_row = (lax.broadcasted_iota(jnp.int32, (VT_PAD, vt.shape[1]), 0) == 0).astype(BF16)
        for h in range(nh):
            vt_ref[hvx * h:hvx * h + hv, :] = vt[hv * h:hv * (h + 1)].astype(BF16)
            vt_ref[hvx * h + hv:hvx * (h + 1), :] = ones_row


def _kv_stage(b, t, tm, pos, kv_in_norm, w_dkv, c_norm, kpe_norm, w_uk, w_uv, *, with_heads):
    d = w_dkv.shape[0]
    rank = c_norm.shape[0]
    rope = kpe_norm.shape[0]
    half = rope // 2
    nh, nope = w_uk.shape[1], w_uk.shape[2]
    hv = w_uv.shape[2]
    nt = t // tm
    m = b * t
    assert nope + 2 * rope == LANES and nope >= 2 * rope
    w_dkv = w_dkv.astype(BF16)
    w_pe = w_dkv[:, rank:]
    w_rot = jnp.concatenate([-w_pe[:, half:], w_pe[:, :half]], axis=1)
    gap = jnp.zeros((d, nope - rope), BF16)
    w_ext = jnp.concatenate([w_dkv[:, :rank], w_pe, gap, w_pe, w_pe, w_rot, gap, w_rot, w_rot], axis=1)
    cos, sin = _rope_tables(pos, half)
    g_rot = jnp.concatenate([kpe_norm[half:], kpe_norm[:half]])
    gcos = jnp.concatenate([cos, cos], 1) * kpe_norm[None, :]
    gsin = jnp.concatenate([sin, sin], 1) * g_rot[None, :]
    tgap = jnp.zeros((t, nope - rope), F32)
    cos_t = jnp.concatenate([gcos, tgap, gcos, gcos], axis=1)
    sin_t = jnp.concatenate([gsin, tgap, gsin, gsin], axis=1)

    const = lambda i: (0, 0)
    args = [kv_in_norm.reshape(1, d), w_ext, c_norm.reshape(1, rank), cos_t, sin_t]
    in_specs = [
        _resident((1, d), const),
        _resident(w_ext.shape, const),
        _resident((1, rank), const),
        pl.BlockSpec((tm, LANES), lambda i: (i % nt, 0)),
        pl.BlockSpec((tm, LANES), lambda i: (i % nt, 0)),
    ]
    out_specs = [pl.BlockSpec((tm, rank), lambda i: (i, 0)), pl.BlockSpec((tm, rope), lambda i: (i, 0))]
    out_shape = [jax.ShapeDtypeStruct((m, rank), F32), jax.ShapeDtypeStruct((m, rope), F32)]
    if with_heads:
        wuk = jnp.concatenate([w_uk.astype(BF16), jnp.zeros((rank, nh, LANES - nope), BF16)], axis=2)
        wuk = wuk.reshape(rank, nh * LANES)
        wuvt = jnp.transpose(w_uv.astype(BF16), (1, 2, 0)).reshape(nh * hv, rank)
        in_specs += [_resident(wuk.shape, const), _resident(wuvt.shape, const)]
        args += [wuk, wuvt]
        out_specs += [pl.BlockSpec((tm, nh * LANES), lambda i: (i, 0)),
                      pl.BlockSpec((None, nh * (hv + VT_PAD), tm), lambda i: (i // nt, 0, i % nt))]
        out_shape += [jax.ShapeDtypeStruct((m, nh * LANES), BF16),
                      jax.ShapeDtypeStruct((b, nh * (hv + VT_PAD), t), BF16)]
    fn = functools.partial(_kv_compute, rank=rank, rope=rope, with_heads=with_heads, nh=nh)
    return _Stage(tuple(args), tuple(in_specs), tuple(out_shape), tuple(out_specs), fn)


def _q_compute(x, in_refs, out_refs, *, nh, nope, rope):
    g_ref, wdq_ref, qn_ref, w1t_ref, tabt_ref = in_refs
    qt_ref, = out_refs
    hb = _rms(x, g_ref[...]).astype(BF16)
    cq = jnp.dot(hb, wdq_ref[...], preferred_element_type=F32)
    cqb = _rms(cq, qn_ref[...]).astype(BF16)
    qm = lax.dot_general(w1t_ref[...], cqb, (((1,), (1,)), ((), ())), preferred_element_type=F32)
    tab_n, tab_p = tabt_ref[:nope, :], tabt_ref[nope:, :]
    for h in range(nh):
        q_n = qm[LANES * h:LANES * h + nope, :]
        q_p = qm[LANES * h + nope:LANES * (h + 1), :]
        r_n = lax.rsqrt(jnp.mean(q_n * q_n, axis=0, keepdims=True) + EPS)
        q_pe = q_p[:rope]
        r_p = lax.rsqrt(jnp.mean(q_pe * q_pe, axis=0, keepdims=True) + EPS)
        qt_ref[LANES * h:LANES * h + nope, :] = (q_n * tab_n * r_n).astype(BF16)
        qt_ref[LANES * h + nope:LANES * (h + 1), :] = (q_p * tab_p * r_p).astype(BF16)


def _q_stage(b, t, tm, pos, mix_norm, w_dq, q_norm, w_uq, qn_norm, qr_norm):
    d, qrank = w_dq.shape
    nh = w_uq.shape[1]
    nope, rope = qn_norm.shape[0], qr_norm.shape[0]
    assert nope + 2 * rope == LANES
    half = rope // 2
    scale = (nope + rope) ** -0.5 * math.log2(math.e)
    nt = t // tm
    m = b * t
    w_uq = w_uq.astype(BF16)
    w_n, w_p = w_uq[:, :, :nope], w_uq[:, :, nope:]
    w_rot = jnp.concatenate([-w_p[:, :, half:], w_p[:, :, :half]], axis=2)
    w1t = jnp.concatenate([w_n, w_p, w_rot], axis=2).reshape(qrank, nh * LANES).T
    g_rot = jnp.concatenate([qr_norm[half:], qr_norm[:half]])
    cos, sin = _rope_tables(pos, half)
    tabt = jnp.concatenate([
        jnp.broadcast_to(qn_norm[:, None], (nope, t)),
        jnp.concatenate([cos, cos], 1).T * qr_norm[:, None],
        jnp.concatenate([sin, sin], 1).T * g_rot[:, None]], axis=0) * scale
    const = lambda i: (0, 0)
    args = (mix_norm.reshape(1, d), w_dq.astype(BF16), q_norm.reshape(1, qrank), w1t, tabt)
    in_specs = (
        _resident((1, d), const),
        _resident(w_dq.shape, const),
        _resident((1, qrank), const),
        _resident(w1t.shape, const),
        pl.BlockSpec((LANES, tm), lambda i: (0, i % nt)),
    )
    out_shape = (jax.ShapeDtypeStruct((nh * LANES, m), BF16),)
    out_specs = (pl.BlockSpec((nh * LANES, tm), lambda i: (0, i)),)
    return _Stage(args, in_specs, out_shape, out_specs, functools.partial(_q_compute, nh=nh, nope=nope, rope=rope))


def _attn_prompt_body(q_ref, k_ref, vt_ref, o_ref, *, tq, hp, hv):
    qi = pl.program_id(2)
    hvx = hv + VT_PAD
    qs = [q_ref[LANES * a:LANES * (a + 1), :] for a in range(hp)]

    def block(ki, carry, masked):
        off = pl.multiple_of(ki * tq, tq)
        ss = []
        for a in range(hp):
            k = k_ref[pl.ds(off, tq), LANES * a:LANES * (a + 1)]
            ss.append(jnp.dot(k, qs[a], preferred_element_type=F32))
        ps, stats = [], []
        for a in range(hp):
            m_i = carry[a][0]
            s = ss[a]
            if masked:
                kpos = lax.broadcasted_iota(jnp.int32, (tq, tq), 0)
                qpos = lax.broadcasted_iota(jnp.int32, (tq, tq), 1)
                s = jnp.where(kpos <= qpos, s, NEG_BIG)
            m_new = jnp.maximum(m_i, jnp.max(s, axis=0, keepdims=True))
            stats.append((m_new, jnp.exp2(m_i - m_new)))
            ps.append(jnp.exp2(s - m_new).astype(BF16))
        out = []
        for a in range(hp):
            m_new, alpha = stats[a]
            pv = jnp.dot(vt_ref[hvx * a:hvx * (a + 1), pl.ds(off, tq)], ps[a], preferred_element_type=F32)
            out.append((m_new, alpha * carry[a][1] + pv))
        return tuple(out)

    init = tuple((jnp.full((1, tq), NEG_BIG, F32), jnp.zeros((hvx, tq), F32)) for _ in range(hp))
    carry = lax.fori_loop(0, qi, lambda ki, c: block(ki, c, False), init)
    fin = block(qi, carry, True)
    for a in range(hp):
        acc = fin[a][1]
        o_ref[hv * a:hv * (a + 1), :] = (acc[:hv] / acc[hv:hv + 1]).astype(BF16)


def _attn_prompt(qt, k, vt, *, nh, hv, tq_pref=512, hp=4):
    b, t, _ = k.shape
    tq = _tile(t, tq_pref)
    nq = t // tq
    assert nh % hp == 0
    return pl.pallas_call(
        functools.partial(_attn_prompt_body, tq=tq, hp=hp, hv=hv),
        grid=(b, nh // hp, t // tq),
        in_specs=[
            pl.BlockSpec((hp * LANES, tq), lambda i, h, j: (h, i * nq + j)),
            pl.BlockSpec((None, t, hp * LANES), lambda i, h, j: (i, 0, h)),
            pl.BlockSpec((None, hp * (hv + VT_PAD), t), lambda i, h, j: (i, h, 0)),
        ],
        out_specs=pl.BlockSpec((None, hp * hv, tq), lambda i, h, j: (i, h, j)),
        out_shape=jax.ShapeDtypeStruct((b, nh * hv, t), BF16),
        compiler_params=_cparams(("parallel", "parallel", "arbitrary")),
        name="attn_prompt",
    )(qt, k, vt)


def _absorb_body(qt_ref, w_ref, o_ref):
    nh, _, width = w_ref.shape
    for h in range(nh):
        o_ref[:, width * h:width * (h + 1)] = lax.dot_general(
            qt_ref[LANES * h:LANES * (h + 1), :], w_ref[h], (((0,), (0,)), ((), ())),
            preferred_element_type=F32).astype(BF16)


def _absorb(qt, w_uk, rope):
    m = qt.shape[1]
    rank, nh, nope = w_uk.shape
    width = rank + LANES
    eye = jnp.broadcast_to(jnp.eye(rope, LANES, dtype=BF16), (nh, rope, LANES))
    w = jnp.concatenate([
        jnp.concatenate([jnp.transpose(w_uk.astype(BF16), (1, 2, 0)), jnp.zeros((nh, nope, LANES), BF16)], axis=2),
        jnp.concatenate([jnp.zeros((nh, 2 * rope, rank), BF16), jnp.concatenate([eye, eye], axis=1)], axis=2),
    ], axis=1)
    out = pl.pallas_call(
        _absorb_body,
        out_shape=jax.ShapeDtypeStruct((m, nh * width), BF16),
        compiler_params=pltpu.CompilerParams(vmem_limit_bytes=VMEM_LIMIT),
        name="absorb_queries",
    )(qt, w)
    return out.reshape(m, nh, width)


CHUNK_AHEAD = 3


def _attn_sample_body(pt_ref, q_ref, cn_ref, pn_ref, ckv_hbm, kpe_hbm, o_ref, cbuf, pbuf, sem,
                      *, npg, nch, page, rank, rope):
    b = pl.program_id(0)
    total = pl.num_programs(0) * nch

    def page_copies(chunk, slot):
        out = []
        for i in range(npg):
            pid = pt_ref[chunk * npg + i]
            out.append(pltpu.make_async_copy(ckv_hbm.at[pid], cbuf.at[slot, pl.ds(i * page, page)],
                                             sem.at[0, slot]))
            out.append(pltpu.make_async_copy(kpe_hbm.at[pid], pbuf.at[slot, :, pl.ds(i * page, page)],
                                             sem.at[1, slot]))
        return out

    @pl.when(b == 0)
    def _():
        for c in range(CHUNK_AHEAD):
            for cp in page_copies(jnp.minimum(c, total - 1), c % nch):
                cp.start()

    q_lat = q_ref[0, :, :rank]
    q_pe = q_ref[0, :, rank:rank + rope]
    nh = q_lat.shape[0]
    m_i = jnp.full((nh, 1), NEG_BIG, F32)
    l_i = jnp.zeros((nh, 1), F32)
    half = npg * page // 2

    def add_pv(accs, alpha, p, c):
        return tuple(alpha * accs[i] + jnp.dot(p[:, half * i:half * (i + 1)], c[half * i:half * (i + 1)],
                                               preferred_element_type=F32) for i in range(2))

    accs = (jnp.zeros((nh, rank), F32), jnp.zeros((nh, rank), F32))
    pending = None
    for j in range(nch):
        for cp in page_copies(b * nch + j, j):
            cp.wait()
        c = cbuf[j].astype(BF16)
        s = (lax.dot_general(q_lat, c, (((1,), (1,)), ((), ())), preferred_element_type=F32)
             + jnp.dot(q_pe, pbuf[j].astype(BF16), preferred_element_type=F32))
        if pending is not None:
            accs = add_pv(accs, *pending)
        m_new = jnp.maximum(m_i, jnp.max(s, axis=1, keepdims=True))
        alpha = jnp.exp2(m_i - m_new)
        p = jnp.exp2(s - m_new)
        l_i = alpha * l_i + jnp.sum(p, axis=1, keepdims=True)
        m_i = m_new
        pending = (alpha, p.astype(BF16), c)
        nxt = b * nch + j + CHUNK_AHEAD
        for cp in page_copies(jnp.minimum(nxt, total - 1), (j + CHUNK_AHEAD) % nch):
            cp.start()
    accs = add_pv(accs, *pending)

    @pl.when(b == pl.num_programs(0) - 1)
    def _():
        for j in range(nch - CHUNK_AHEAD, nch):
            for cp in page_copies(total - 1, (j + CHUNK_AHEAD) % nch):
                cp.wait()

    c_new = cn_ref[0].astype(BF16).astype(F32)
    p_new = pn_ref[0].astype(BF16).astype(F32)
    s_new = (jnp.sum(q_lat.astype(F32) * c_new, axis=1, keepdims=True)
             + jnp.sum(q_pe.astype(F32) * p_new, axis=1, keepdims=True))
    m_fin = jnp.maximum(m_i, s_new)
    a_old = jnp.exp2(m_i - m_fin)
    p_n = jnp.exp2(s_new - m_fin)
    l_fin = a_old * l_i + p_n
    o_new = p_n.astype(BF16).astype(F32) * c_new
    o_ref[0] = (a_old * accs[0] + (a_old * accs[1] + o_new)) / l_fin


def _attn_sample(qabs, cache_ckv, cache_kpe, page_table, c_new, kpe_new, *, npg_pref=32):
    m, nh, width = qabs.shape
    _, page, rank = cache_ckv.shape
    rope = cache_kpe.shape[2]
    n_pages = page_table.shape[1]
    npg = npg_pref if n_pages % npg_pref == 0 else n_pages
    nch = n_pages // npg
    assert nch > CHUNK_AHEAD, "a slot must not be refilled while its chunk is in use"
    kpe_t = jnp.swapaxes(cache_kpe, 1, 2)
    grid_spec = pltpu.PrefetchScalarGridSpec(
        num_scalar_prefetch=1,
        grid=(m,),
        in_specs=[
            pl.BlockSpec((1, nh, width), lambda b, pt: (b, 0, 0)),
            pl.BlockSpec((1, 1, rank), lambda b, pt: (b, 0, 0)),
            pl.BlockSpec((1, 1, rope), lambda b, pt: (b, 0, 0)),
            pl.BlockSpec(memory_space=pl.ANY),
            pl.BlockSpec(memory_space=pl.ANY),
        ],
        out_specs=pl.BlockSpec((1, nh, rank), lambda b, pt: (b, 0, 0)),
        scratch_shapes=[
            pltpu.VMEM((nch, npg * page, rank), F32),
            pltpu.VMEM((nch, rope, npg * page), F32),
            pltpu.SemaphoreType.DMA((2, nch)),
        ],
    )
    return pl.pallas_call(
        functools.partial(_attn_sample_body, npg=npg, nch=nch, page=page, rank=rank, rope=rope),
        grid_spec=grid_spec,
        out_shape=jax.ShapeDtypeStruct((m, nh, rank), F32),
        compiler_params=_cparams(("arbitrary",)),
        name="attn_sample",
    )(page_table.reshape(-1), qabs, c_new.reshape(m, 1, rank), kpe_new.reshape(m, 1, rope),
      cache_ckv, kpe_t)


def _oproj_lat_body(ol_ref, x_ref, wuv_ref, wo_ref, o_ref, *, nh, rank, hv):
    acc = x_ref[...]
    for h in range(nh):
        oh = jnp.dot(ol_ref[:, rank * h:rank * (h + 1)].astype(BF16), wuv_ref[h], preferred_element_type=F32)
        acc = acc + jnp.dot(oh.astype(BF16), wo_ref[hv * h:hv * (h + 1), :], preferred_element_type=F32)
    o_ref[...] = acc


def _oproj_lat(o_lat, x, w_uv, w_o):
    m, nh, rank = o_lat.shape
    hv = w_uv.shape[2]
    return pl.pallas_call(
        functools.partial(_oproj_lat_body, nh=nh, rank=rank, hv=hv),
        out_shape=jax.ShapeDtypeStruct(x.shape, F32),
        compiler_params=pltpu.CompilerParams(vmem_limit_bytes=VMEM_LIMIT),
        name="attn_out_sample",
    )(o_lat.reshape(m, nh * rank), x, jnp.transpose(w_uv, (1, 0, 2)).astype(BF16), w_o.astype(BF16))


def kernel(x_prompt, x_sample, state_s5_re, state_s5_im, cache_ckv, cache_kpe, page_table, ffn_norm, ffn_w_gate, ffn_w_up, ffn_w_down, mix_norm, s5_a_re, s5_a_im, s5_log_dt, s5_b_re, s5_b_im, s5_c_re, s5_c_im, s5_d, s5_w_glu_v, s5_w_glu_g, mla_w_dq, mla_q_norm, mla_w_uq, mla_qn_norm, mla_qr_norm, mla_w_o, kv_in_norm, kv_w_dkv, kv_c_norm, kv_kpe_norm, kv_w_uk, kv_w_uv):
    bp, tp, d = x_prompt.shape
    bs, ts, _ = x_sample.shape
    assert ts == 1, "the sample group decodes one token per sequence"
    depth = ffn_norm.shape[0]
    n_a = s5_a_re.shape[0]
    assert depth == 2 and n_a == 1 and mla_w_dq.shape[0] == 1, "one S5 layer followed by one MLA layer"
    g, p = s5_a_re.shape[1], s5_a_re.shape[2]
    nh, hv = kv_w_uv.shape[1], kv_w_uv.shape[2]
    rope = kv_kpe_norm.shape[0]
    past_len = page_table.shape[1] * cache_ckv.shape[1]

    wg, wu, wd = ffn_w_gate.astype(BF16), ffn_w_up.astype(BF16), ffn_w_down.astype(BF16)
    tm_p, tm_s = _tile(tp, 512), bs

    def ffn(layer, idx, *groups):
        return _ffn(groups, ffn_norm[layer, idx], wg, wu, wd, layer, idx)

    xp = x_prompt.reshape(bp * tp, d)
    xs = x_sample.reshape(bs, d)
    pos_p = jnp.arange(tp, dtype=jnp.int32)
    pos_s = jnp.full((bs,), past_len, dtype=jnp.int32)
    kv_args = (kv_in_norm, kv_w_dkv, kv_c_norm, kv_kpe_norm, kv_w_uk, kv_w_uv)
    q_args = (mix_norm[1], mla_w_dq[0], mla_q_norm[0], mla_w_uq[0], mla_qn_norm[0], mla_qr_norm[0])

    lam_re, lam_im, wb, wc = _s5_params(s5_a_re[0], s5_a_im[0], s5_log_dt[0], s5_b_re[0], s5_b_im[0],
                                        s5_c_re[0], s5_c_im[0])
    wv, wgl = s5_w_glu_v[0].astype(BF16), s5_w_glu_g[0].astype(BF16)
    (xp,), (xs,) = ffn(0, 0, _Rows(xp, tm_p), _Rows(xs, tm_s))
    xp, sre_p, sim_p = _s5_prompt(xp.reshape(bp, tp, d), mix_norm[0], lam_re, lam_im, wb, wc, s5_d[0], wv, wgl)
    xs, sre_s, sim_s = _s5_step(xs, mix_norm[0], state_s5_re[0].reshape(bs, g * p),
                                state_s5_im[0].reshape(bs, g * p), lam_re, lam_im, wb, wc, s5_d[0], wv, wgl)
    (xp, ckv_p, kpe_p, k_p, vt_p), (xs, ckv_s, kpe_s) = ffn(
        0, 1,
        _Rows(xp.reshape(bp * tp, d), tm_p, post=_kv_stage(bp, tp, tm_p, pos_p, *kv_args, with_heads=True)),
        _Rows(xs, tm_s, post=_kv_stage(1, bs, tm_s, pos_s, *kv_args, with_heads=False)))

    (xp, q_p), (xs, q_s) = ffn(1, 0,
                               _Rows(xp, tm_p, post=_q_stage(bp, tp, tm_p, pos_p, *q_args)),
                               _Rows(xs, tm_s, post=_q_stage(1, bs, tm_s, pos_s, *q_args)))
    ot = _attn_prompt(q_p, k_p.reshape(bp, tp, nh * LANES), vt_p, nh=nh, hv=hv)
    qabs = _absorb(q_s, kv_w_uk, rope)
    o_lat = _attn_sample(qabs, cache_ckv, cache_kpe, page_table, ckv_s, kpe_s)
    xs = _oproj_lat(o_lat, xs, kv_w_uv, mla_w_o[0])
    (xp,), (xs,) = ffn(1, 1, _Rows(xp, tm_p, pre=(ot, mla_w_o[0].astype(BF16))), _Rows(xs, tm_s))

    return (xp.reshape(bp, tp, d), xs.reshape(bs, 1, d),
            sre_p.reshape(n_a, bp, g, p), sim_p.reshape(n_a, bp, g, p),
            sre_s.reshape(n_a, bs, g, p), sim_s.reshape(n_a, bs, g, p),
            ckv_p.reshape(bp, tp, -1), kpe_p.reshape(bp, tp, rope),
            ckv_s.reshape(bs, 1, -1), kpe_s.reshape(bs, 1, rope))
```

```python
import functools
import math
from typing import Callable, NamedTuple

import jax
import jax.numpy as jnp
import numpy as np
from jax import lax
from jax.experimental import pallas as pl
from jax.experimental.pallas import tpu as pltpu

F32 = jnp.float32
BF16 = jnp.bfloat16

EPS = 1e-6
FFN_RES = 0.5
NEG_BIG = -1e30
ROPE_BASE = 10000.0

GROUP_CH = 16
LANES = 128
SUBLANES = 8
S5_SLAB = 256
S5_PITCH = 12
VT_PAD = 16
KPE_COPIES = 3
VMEM_LIMIT = 56 * 1024 * 1024


def _tile(n, pref):
    if n <= pref:
        return n
    for t in range(pref, 7, -1):
        if n % t == 0 and t % 8 == 0:
            return t
    return n


def _rms(x, gain):
    ms = jnp.mean(x * x, axis=-1, keepdims=True)
    return x * lax.rsqrt(ms + EPS) * gain


def _cparams(sem):
    return pltpu.CompilerParams(dimension_semantics=sem, vmem_limit_bytes=VMEM_LIMIT)


class _Stage(NamedTuple):
    args: tuple
    in_specs: tuple
    out_shape: tuple
    out_specs: tuple
    fn: Callable


class _Rows(NamedTuple):
    x: jax.Array
    tm: int
    pre: tuple = None
    post: _Stage = None


def _ffn_rows(x_ref, g_ref, wg_ref, wu_ref, wd_ref, pre_refs, post, post_in, o_ref, post_out):
    x = x_ref[...]
    if pre_refs:
        ot_ref, wo_ref = pre_refs
        x = x + lax.dot_general(ot_ref[...], wo_ref[...], (((0,), (0,)), ((), ())), preferred_element_type=F32)
    xn = _rms(x, g_ref[...]).astype(BF16)
    gate = jnp.dot(xn, wg_ref[...], preferred_element_type=F32)
    up = jnp.dot(xn, wu_ref[...], preferred_element_type=F32)
    hid = (gate * jax.nn.sigmoid(gate) * up).astype(BF16)
    y = x + FFN_RES * jnp.dot(hid, wd_ref[...], preferred_element_type=F32)
    o_ref[...] = y
    if post:
        post.fn(y, post_in, post_out)


def _ffn_body(*refs, plan):
    g_ref, wg_ref, wu_ref, wd_ref = refs[:4]
    n_inputs = 4 + sum(1 + 2 * has_pre + (len(post.args) if post else 0) for _, _, has_pre, post in plan)
    ins, outs = list(refs[4:n_inputs]), list(refs[n_inputs:])
    i = pl.program_id(0)
    for first, tiles, has_pre, post in plan:
        x_ref = ins.pop(0)
        pre_refs = [ins.pop(0) for _ in range(2 * has_pre)]
        post_in = [ins.pop(0) for _ in range(len(post.args) if post else 0)]
        o_ref = outs.pop(0)
        post_out = [outs.pop(0) for _ in range(len(post.out_shape) if post else 0)]
        run = functools.partial(_ffn_rows, x_ref, g_ref, wg_ref, wu_ref, wd_ref, pre_refs, post, post_in,
                                o_ref, post_out)
        if len(plan) == 1:
            run()
        else:
            pl.when(jnp.logical_and(i >= first, i < first + tiles))(run)


def _resident(shape, index_map):
    return pl.BlockSpec(shape, index_map, pipeline_mode=pl.Buffered(1))


def _ffn(groups, gain, w_gate, w_up, w_down, layer, idx):
    d = groups[0].x.shape[1]
    ff = w_gate.shape[3]
    pick = lambda i: (layer, idx, 0, 0)
    args = [gain.reshape(1, d), w_gate, w_up, w_down]
    in_specs = [
        _resident((1, d), lambda i: (0, 0)),
        _resident((None, None, d, ff), pick),
        _resident((None, None, d, ff), pick),
        _resident((None, None, ff, d), pick),
    ]
    out_shape, out_specs, plan, first = [], [], [], 0

    def local(spec, first, tiles):
        return pl.BlockSpec(spec.block_shape, lambda i: spec.index_map(jnp.clip(i - first, 0, tiles - 1)),
                            pipeline_mode=spec.pipeline_mode)

    for rows in groups:
        m, tm = rows.x.shape[0], rows.tm
        tiles = m // tm
        g_in = [pl.BlockSpec((tm, d), lambda i: (i, 0))]
        g_args = [rows.x]
        if rows.pre is not None:
            ot, w_o = rows.pre
            nt = ot.shape[2] // tm
            g_args += [ot, w_o]
            g_in += [pl.BlockSpec((None, ot.shape[1], tm), lambda i, nt=nt: (i // nt, 0, i % nt)),
                     _resident(w_o.shape, lambda i: (0, 0))]
        g_out_shape = [jax.ShapeDtypeStruct((m, d), F32)]
        g_out = [pl.BlockSpec((tm, d), lambda i: (i, 0))]
        if rows.post is not None:
            g_args += list(rows.post.args)
            g_in += list(rows.post.in_specs)
            g_out_shape += list(rows.post.out_shape)
            g_out += list(rows.post.out_specs)
        args += g_args
        in_specs += [local(sp, first, tiles) for sp in g_in]
        out_shape += g_out_shape
        out_specs += [local(sp, first, tiles) for sp in g_out]
        plan.append((first, tiles, rows.pre is not None, rows.post))
        first += tiles
    out = pl.pallas_call(
        functools.partial(_ffn_body, plan=tuple(plan)),
        grid=(first,),
        in_specs=in_specs,
        out_specs=out_specs,
        out_shape=out_shape,
        compiler_params=_cparams(("arbitrary",)),
        name="ffn",
    )(*args)
    res, k = [], 0
    for _, _, _, post in plan:
        n = 1 + (len(post.out_shape) if post else 0)
        res.append(tuple(out[k:k + n]))
        k += n
    return res


def _s5_params(a_re, a_im, log_dt, b_re, b_im, c_re, c_im):
    g, p = a_re.shape
    dt = jnp.exp(log_dt)[:, None]
    mag = jnp.exp(dt * a_re)
    lam_re, lam_im = mag * jnp.cos(dt * a_im), mag * jnp.sin(dt * a_im)
    den = a_re * a_re + a_im * a_im
    f_re = ((lam_re - 1.0) * a_re + lam_im * a_im) / den
    f_im = (lam_im * a_re - (lam_re - 1.0) * a_im) / den
    bb_re = f_re[..., None] * b_re - f_im[..., None] * b_im
    bb_im = f_re[..., None] * b_im + f_im[..., None] * b_re
    gs = S5_SLAB // GROUP_CH
    ns = g // gs
    width = 2 * gs * p
    j = np.arange(width)
    state = (j // (2 * LANES)) * LANES + j % LANES
    part = (j // LANES) % 2
    place = (np.arange(p)[None, :, None] == (state % p)[None, None, :]) & (np.arange(2)[:, None, None] == part[None, None, :])
    place = jnp.asarray(place, F32)
    own = jnp.asarray(np.arange(gs)[:, None] == (state // p)[None, :], F32)

    def pack(w_re, w_im):
        w = jnp.einsum("asgcp,apj->sgcj", jnp.stack([w_re, w_im]), place) * own[None, :, None, :]
        return w.reshape(ns, S5_SLAB, width).astype(BF16)

    to_gcp = lambda bb: jnp.transpose(bb.reshape(ns, gs, p, GROUP_CH), (0, 1, 3, 2))
    wb = pack(to_gcp(bb_re), to_gcp(bb_im))
    wc = jnp.transpose(pack(c_re.reshape(ns, gs, GROUP_CH, p), -c_im.reshape(ns, gs, GROUP_CH, p)), (0, 2, 1))
    return lam_re, lam_im, wb, wc


def _s5_prompt_body(x_ref, g_ref, wb_ref, wc_ref, lr_ref, li_ref, d_ref, wv_ref, wgl_ref,
                    o_ref, sre_ref, sim_ref, xr_ref, xi_ref, cr_ref, ci_ref, y_ref, *, tc, ns, nk):
    t_idx = pl.program_id(1)

    @pl.when(t_idx == 0)
    def _():
        cr_ref[...] = jnp.zeros_like(cr_ref)
        ci_ref[...] = jnp.zeros_like(ci_ref)

    x = x_ref[0]
    h = _rms(x, g_ref[...])
    hb = h.astype(BF16)
    for s in range(ns):
        us = hb[:, S5_SLAB * s:S5_SLAB * (s + 1)]
        for k in range(nk):
            xk = jnp.dot(us, wb_ref[s, :, 2 * LANES * k:2 * LANES * (k + 1)],
                         preferred_element_type=F32)
            xr_ref[s, pl.ds(k, tc, stride=S5_PITCH), :] = xk[:, :LANES]
            xi_ref[s, pl.ds(k, tc, stride=S5_PITCH), :] = xk[:, LANES:]

    lam = [(lr_ref[s], li_ref[s]) for s in range(ns)]

    def step(t, carry):
        row = pl.multiple_of(t * S5_PITCH, math.gcd(S5_PITCH, SUBLANES))
        out = []
        for s in range(ns):
            sr, si = carry[s]
            lr, li = lam[s]
            nr = lr * sr - li * si + xr_ref[s, pl.ds(row, nk), :]
            ni = lr * si + li * sr + xi_ref[s, pl.ds(row, nk), :]
            xr_ref[s, pl.ds(row, nk), :] = nr
            xi_ref[s, pl.ds(row, nk), :] = ni
            out.append((nr, ni))
        return tuple(out)

    init = tuple((cr_ref[s], ci_ref[s]) for s in range(ns))
    fin = lax.fori_loop(0, tc, step, init, unroll=8)
    for s in range(ns):
        cr_ref[s] = fin[s][0]
        ci_ref[s] = fin[s][1]

    for s in range(ns):
        acc = None
        for k in range(nk):
            sk = jnp.concatenate([xr_ref[s, pl.ds(k, tc, stride=S5_PITCH), :],
                                  xi_ref[s, pl.ds(k, tc, stride=S5_PITCH), :]], axis=1).astype(BF16)
            part = jnp.dot(sk, wc_ref[s, 2 * LANES * k:2 * LANES * (k + 1), :],
                           preferred_element_type=F32)
            acc = part if acc is None else acc + part
        y_ref[:, S5_SLAB * s:S5_SLAB * (s + 1)] = acc

    y = jax.nn.gelu(y_ref[...] + d_ref[...] * h, approximate=True).astype(BF16)
    val = jnp.dot(y, wv_ref[...], preferred_element_type=F32)
    gate = jnp.dot(y, wgl_ref[...], preferred_element_type=F32)
    o_ref[0] = x + val * jax.nn.sigmoid(gate)

    @pl.when(t_idx == pl.num_programs(1) - 1)
    def _():
        sre_ref[0] = cr_ref[...].reshape(ns * nk, LANES)
        sim_ref[0] = ci_ref[...].reshape(ns * nk, LANES)


def _s5_prompt(x, gain, lam_re, lam_im, wb, wc, d_skip, w_v, w_g, *, tc_pref=512):
    b, t, d = x.shape
    ns = wb.shape[0]
    nk = wb.shape[2] // (2 * LANES)
    assert nk == SUBLANES, "one token's slab state must fill whole vregs"
    tc = _tile(t, tc_pref)
    lr = lam_re.reshape(ns, nk, LANES)
    li = lam_im.reshape(ns, nk, LANES)
    const3 = lambda i, j: (0, 0, 0)
    const2 = lambda i, j: (0, 0)
    out, s_re, s_im = pl.pallas_call(
        functools.partial(_s5_prompt_body, tc=tc, ns=ns, nk=nk),
        grid=(b, t // tc),
        in_specs=[
            pl.BlockSpec((1, tc, d), lambda i, j: (i, j, 0)),
            _resident((1, d), const2),
            _resident(wb.shape, const3),
            _resident(wc.shape, const3),
            _resident(lr.shape, const3),
            _resident(li.shape, const3),
            _resident((1, d), const2),
            _resident(w_v.shape, const2),
            _resident(w_g.shape, const2),
        ],
        out_specs=[
            pl.BlockSpec((1, tc, d), lambda i, j: (i, j, 0)),
            pl.BlockSpec((1, ns * nk, LANES), lambda i, j: (i, 0, 0)),
            pl.BlockSpec((1, ns * nk, LANES), lambda i, j: (i, 0, 0)),
        ],
        out_shape=[
            jax.ShapeDtypeStruct((b, t, d), F32),
            jax.ShapeDtypeStruct((b, ns * nk, LANES), F32),
            jax.ShapeDtypeStruct((b, ns * nk, LANES), F32),
        ],
        scratch_shapes=[
            pltpu.VMEM((ns, tc * S5_PITCH, LANES), F32),
            pltpu.VMEM((ns, tc * S5_PITCH, LANES), F32),
            pltpu.VMEM((ns, nk, LANES), F32),
            pltpu.VMEM((ns, nk, LANES), F32),
            pltpu.VMEM((tc, d), F32),
        ],
        compiler_params=_cparams(("parallel", "arbitrary")),
        name="s5_prompt",
    )(x, gain.reshape(1, d), wb, wc, lr, li, d_skip.reshape(1, d), w_v, w_g)
    return out, s_re, s_im


def _s5_step_body(x_ref, g_ref, s0r_ref, s0i_ref, wb_ref, wc_ref, lr_ref, li_ref, d_ref,
                  wv_ref, wgl_ref, o_ref, sre_ref, sim_ref, y_ref, *, ns, nk):
    x = x_ref[...]
    h = _rms(x, g_ref[...])
    hb = h.astype(BF16)
    for s in range(ns):
        xs = jnp.dot(hb[:, S5_SLAB * s:S5_SLAB * (s + 1)], wb_ref[s], preferred_element_type=F32)
        acc = None
        for k in range(nk):
            col = (s * nk + k) * LANES
            lr = lr_ref[:, col:col + LANES]
            li = li_ref[:, col:col + LANES]
            s0r = s0r_ref[:, col:col + LANES]
            s0i = s0i_ref[:, col:col + LANES]
            nr = lr * s0r - li * s0i + xs[:, 2 * LANES * k:2 * LANES * k + LANES]
            ni = lr * s0i + li * s0r + xs[:, 2 * LANES * k + LANES:2 * LANES * (k + 1)]
            sre_ref[:, col:col + LANES] = nr
            sim_ref[:, col:col + LANES] = ni
            part = (jnp.dot(nr.astype(BF16), wc_ref[s, 2 * LANES * k:2 * LANES * k + LANES, :],
                            preferred_element_type=F32)
                    + jnp.dot(ni.astype(BF16), wc_ref[s, 2 * LANES * k + LANES:2 * LANES * (k + 1), :],
                              preferred_element_type=F32))
            acc = part if acc is None else acc + part
        y_ref[:, S5_SLAB * s:S5_SLAB * (s + 1)] = acc
    y = jax.nn.gelu(y_ref[...] + d_ref[...] * h, approximate=True).astype(BF16)
    val = jnp.dot(y, wv_ref[...], preferred_element_type=F32)
    gate = jnp.dot(y, wgl_ref[...], preferred_element_type=F32)
    o_ref[...] = x + val * jax.nn.sigmoid(gate)


def _s5_step(x, gain, s0_re, s0_im, lam_re, lam_im, wb, wc, d_skip, w_v, w_g):
    m, d = x.shape
    ns = wb.shape[0]
    nk = wb.shape[2] // (2 * LANES)
    nstate = s0_re.shape[1]
    args = (x, gain.reshape(1, d), s0_re, s0_im, wb, wc, lam_re.reshape(1, nstate),
            lam_im.reshape(1, nstate), d_skip.reshape(1, d), w_v, w_g)
    return pl.pallas_call(
        functools.partial(_s5_step_body, ns=ns, nk=nk),
        out_shape=[
            jax.ShapeDtypeStruct((m, d), F32),
            jax.ShapeDtypeStruct((m, nstate), F32),
            jax.ShapeDtypeStruct((m, nstate), F32),
        ],
        scratch_shapes=[pltpu.VMEM((m, d), F32)],
        compiler_params=pltpu.CompilerParams(vmem_limit_bytes=VMEM_LIMIT),
        name="s5_step",
    )(*args)


def _rope_tables(pos, half):
    inv = ROPE_BASE ** (-jnp.arange(half, dtype=F32) / half)
    ang = pos.astype(F32)[:, None] * inv[None, :]
    return jnp.cos(ang), jnp.sin(ang)


def _kv_compute(x, in_refs, out_refs, *, rank, rope, with_heads, nh):
    if with_heads:
        g_ref, w_ref, cn_ref, cos_ref, sin_ref, wuk_ref, wuvt_ref = in_refs
        c_ref, kpe_ref, k_ref, vt_ref = out_refs
    else:
        g_ref, w_ref, cn_ref, cos_ref, sin_ref = in_refs
        c_ref, kpe_ref = out_refs
    hk = _rms(x, g_ref[...]).astype(BF16)
    ck = jnp.dot(hk, w_ref[...], preferred_element_type=F32)
    c = _rms(ck[:, :rank], cn_ref[...])
    c_ref[...] = c

    pe = ck[:, rank:rank + LANES]
    rot = ck[:, rank + LANES:rank + 2 * LANES]
    r = lax.rsqrt(jnp.sum(pe * pe, axis=-1, keepdims=True) * (1.0 / (rope * KPE_COPIES)) + EPS)
    kpe_blk = (pe * cos_ref[...] + rot * sin_ref[...]) * r
    kpe_ref[...] = kpe_blk[:, :rope]
    if with_heads:
        cb = c.astype(BF16)
        nope = LANES - 2 * rope
        lane = lax.broadcasted_iota(jnp.int32, (1, LANES), 1)
        pe_mid = jnp.where(lane >= nope, kpe_blk, 0.0)
        kn = jnp.dot(cb, wuk_ref[...], preferred_element_type=F32)
        for h in range(nh):
            k_ref[:, LANES * h:LANES * (h + 1)] = (kn[:, LANES * h:LANES * (h + 1)] + pe_mid).astype(BF16)
        vt = lax.dot_general(wuvt_ref[...], cb, (((1,), (1,)), ((), ())), preferred_element_type=F32)
        hv = vt.shape[0] // nh
        hvx = hv + VT_PAD
        ones_row = (lax.broadcasted_iota(jnp.int32, (VT_PAD, vt.shape[1]), 0) == 0).astype(BF16)
        for h in range(nh):
            vt_ref[hvx * h:hvx * h + hv, :] = vt[hv * h:hv * (h + 1)].astype(BF16)
            vt_ref[hvx * h + hv:hvx * (h + 1), :] = ones_row


def _kv_stage(b, t, tm, pos, kv_in_norm, w_dkv, c_norm, kpe_norm, w_uk, w_uv, *, with_heads):
    d = w_dkv.shape[0]
    rank = c_norm.shape[0]
    rope = kpe_norm.shape[0]
    half = rope // 2
    nh, nope = w_uk.shape[1], w_uk.shape[2]
    hv = w_uv.shape[2]
    nt = t // tm
    m = b * t
    assert nope + 2 * rope == LANES and nope >= 2 * rope
    w_dkv = w_dkv.astype(BF16)
    w_pe = w_dkv[:, rank:]
    w_rot = jnp.concatenate([-w_pe[:, half:], w_pe[:, :half]], axis=1)
    gap = jnp.zeros((d, nope - rope), BF16)
    w_ext = jnp.concatenate([w_dkv[:, :rank], w_pe, gap, w_pe, w_pe, w_rot, gap, w_rot, w_rot], axis=1)
    cos, sin = _rope_tables(pos, half)
    g_rot = jnp.concatenate([kpe_norm[half:], kpe_norm[:half]])
    gcos = jnp.concatenate([cos, cos], 1) * kpe_norm[None, :]
    gsin = jnp.concatenate([sin, sin], 1) * g_rot[None, :]
    tgap = jnp.zeros((t, nope - rope), F32)
    cos_t = jnp.concatenate([gcos, tgap, gcos, gcos], axis=1)
    sin_t = jnp.concatenate([gsin, tgap, gsin, gsin], axis=1)

    const = lambda i: (0, 0)
    args = [kv_in_norm.reshape(1, d), w_ext, c_norm.reshape(1, rank), cos_t, sin_t]
    in_specs = [
        _resident((1, d), const),
        _resident(w_ext.shape, const),
        _resident((1, rank), const),
        pl.BlockSpec((tm, LANES), lambda i: (i % nt, 0)),
        pl.BlockSpec((tm, LANES), lambda i: (i % nt, 0)),
    ]
    out_specs = [pl.BlockSpec((tm, rank), lambda i: (i, 0)), pl.BlockSpec((tm, rope), lambda i: (i, 0))]
    out_shape = [jax.ShapeDtypeStruct((m, rank), F32), jax.ShapeDtypeStruct((m, rope), F32)]
    if with_heads:
        wuk = jnp.concatenate([w_uk.astype(BF16), jnp.zeros((rank, nh, LANES - nope), BF16)], axis=2)
        wuk = wuk.reshape(rank, nh * LANES)
        wuvt = jnp.transpose(w_uv.astype(BF16), (1, 2, 0)).reshape(nh * hv, rank)
        in_specs += [_resident(wuk.shape, const), _resident(wuvt.shape, const)]
        args += [wuk, wuvt]
        out_specs += [pl.BlockSpec((tm, nh * LANES), lambda i: (i, 0)),
                      pl.BlockSpec((None, nh * (hv + VT_PAD), tm), lambda i: (i // nt, 0, i % nt))]
        out_shape += [jax.ShapeDtypeStruct((m, nh * LANES), BF16),
                      jax.ShapeDtypeStruct((b, nh * (hv + VT_PAD), t), BF16)]
    fn = functools.partial(_kv_compute, rank=rank, rope=rope, with_heads=with_heads, nh=nh)
    return _Stage(tuple(args), tuple(in_specs), tuple(out_shape), tuple(out_specs), fn)


def _q_compute(x, in_refs, out_refs, *, nh, nope, rope):
    g_ref, wdq_ref, qn_ref, w1t_ref, tabt_ref = in_refs
    qt_ref, = out_refs
    hb = _rms(x, g_ref[...]).astype(BF16)
    cq = jnp.dot(hb, wdq_ref[...], preferred_element_type=F32)
    cqb = _rms(cq, qn_ref[...]).astype(BF16)
    qm = lax.dot_general(w1t_ref[...], cqb, (((1,), (1,)), ((), ())), preferred_element_type=F32)
    tab_n, tab_p = tabt_ref[:nope, :], tabt_ref[nope:, :]
    for h in range(nh):
        q_n = qm[LANES * h:LANES * h + nope, :]
        q_p = qm[LANES * h + nope:LANES * (h + 1), :]
        r_n = lax.rsqrt(jnp.mean(q_n * q_n, axis=0, keepdims=True) + EPS)
        q_pe = q_p[:rope]
        r_p = lax.rsqrt(jnp.mean(q_pe * q_pe, axis=0, keepdims=True) + EPS)
        qt_ref[LANES * h:LANES * h + nope, :] = (q_n * tab_n * r_n).astype(BF16)
        qt_ref[LANES * h + nope:LANES * (h + 1), :] = (q_p * tab_p * r_p).astype(BF16)


def _q_stage(b, t, tm, pos, mix_norm, w_dq, q_norm, w_uq, qn_norm, qr_norm):
    d, qrank = w_dq.shape
    nh = w_uq.shape[1]
    nope, rope = qn_norm.shape[0], qr_norm.shape[0]
    assert nope + 2 * rope == LANES
    half = rope // 2
    scale = (nope + rope) ** -0.5 * math.log2(math.e)
    nt = t // tm
    m = b * t
    w_uq = w_uq.astype(BF16)
    w_n, w_p = w_uq[:, :, :nope], w_uq[:, :, nope:]
    w_rot = jnp.concatenate([-w_p[:, :, half:], w_p[:, :, :half]], axis=2)
    w1t = jnp.concatenate([w_n, w_p, w_rot], axis=2).reshape(qrank, nh * LANES).T
    g_rot = jnp.concatenate([qr_norm[half:], qr_norm[:half]])
    cos, sin = _rope_tables(pos, half)
    tabt = jnp.concatenate([
        jnp.broadcast_to(qn_norm[:, None], (nope, t)),
        jnp.concatenate([cos, cos], 1).T * qr_norm[:, None],
        jnp.concatenate([sin, sin], 1).T * g_rot[:, None]], axis=0) * scale
    const = lambda i: (0, 0)
    args = (mix_norm.reshape(1, d), w_dq.astype(BF16), q_norm.reshape(1, qrank), w1t, tabt)
    in_specs = (
        _resident((1, d), const),
        _resident(w_dq.shape, const),
        _resident((1, qrank), const),
        _resident(w1t.shape, const),
        pl.BlockSpec((LANES, tm), lambda i: (0, i % nt)),
    )
    out_shape = (jax.ShapeDtypeStruct((nh * LANES, m), BF16),)
    out_specs = (pl.BlockSpec((nh * LANES, tm), lambda i: (0, i)),)
    return _Stage(args, in_specs, out_shape, out_specs, functools.partial(_q_compute, nh=nh, nope=nope, rope=rope))


def _attn_prompt_body(q_ref, k_ref, vt_ref, o_ref, *, tq, hp, hv):
    qi = pl.program_id(2)
    hvx = hv + VT_PAD
    qs = [q_ref[LANES * a:LANES * (a + 1), :] for a in range(hp)]

    def block(ki, carry, masked):
        off = pl.multiple_of(ki * tq, tq)
        ss = []
        for a in range(hp):
            k = k_ref[pl.ds(off, tq), LANES * a:LANES * (a + 1)]
            ss.append(jnp.dot(k, qs[a], preferred_element_type=F32))
        ps, stats = [], []
        for a in range(hp):
            m_i = carry[a][0]
            s = ss[a]
            if masked:
                kpos = lax.broadcasted_iota(jnp.int32, (tq, tq), 0)
                qpos = lax.broadcasted_iota(jnp.int32, (tq, tq), 1)
                s = jnp.where(kpos <= qpos, s, NEG_BIG)
            m_new = jnp.maximum(m_i, jnp.max(s, axis=0, keepdims=True))
            stats.append((m_new, jnp.exp2(m_i - m_new)))
            ps.append(jnp.exp2(s - m_new).astype(BF16))
        out = []
        for a in range(hp):
            m_new, alpha = stats[a]
            pv = jnp.dot(vt_ref[hvx * a:hvx * (a + 1), pl.ds(off, tq)], ps[a], preferred_element_type=F32)
            out.append((m_new, alpha * carry[a][1] + pv))
        return tuple(out)

    init = tuple((jnp.full((1, tq), NEG_BIG, F32), jnp.zeros((hvx, tq), F32)) for _ in range(hp))
    carry = lax.fori_loop(0, qi, lambda ki, c: block(ki, c, False), init)
    fin = block(qi, carry, True)
    for a in range(hp):
        acc = fin[a][1]
        o_ref[hv * a:hv * (a + 1), :] = (acc[:hv] / acc[hv:hv + 1]).astype(BF16)


def _attn_prompt(qt, k, vt, *, nh, hv, tq_pref=512, hp=4):
    b, t, _ = k.shape
    tq = _tile(t, tq_pref)
    nq = t // tq
    assert nh % hp == 0
    return pl.pallas_call(
        functools.partial(_attn_prompt_body, tq=tq, hp=hp, hv=hv),
        grid=(b, nh // hp, t // tq),
        in_specs=[
            pl.BlockSpec((hp * LANES, tq), lambda i, h, j: (h, i * nq + j)),
            pl.BlockSpec((None, t, hp * LANES), lambda i, h, j: (i, 0, h)),
            pl.BlockSpec((None, hp * (hv + VT_PAD), t), lambda i, h, j: (i, h, 0)),
        ],
        out_specs=pl.BlockSpec((None, hp * hv, tq), lambda i, h, j: (i, h, j)),
        out_shape=jax.ShapeDtypeStruct((b, nh * hv, t), BF16),
        compiler_params=_cparams(("parallel", "parallel", "arbitrary")),
        name="attn_prompt",
    )(qt, k, vt)


def _absorb_body(qt_ref, w_ref, o_ref):
    nh, _, width = w_ref.shape
    for h in range(nh):
        o_ref[:, width * h:width * (h + 1)] = lax.dot_general(
            qt_ref[LANES * h:LANES * (h + 1), :], w_ref[h], (((0,), (0,)), ((), ())),
            preferred_element_type=F32).astype(BF16)


def _absorb(qt, w_uk, rope):
    m = qt.shape[1]
    rank, nh, nope = w_uk.shape
    width = rank + LANES
    eye = jnp.broadcast_to(jnp.eye(rope, LANES, dtype=BF16), (nh, rope, LANES))
    w = jnp.concatenate([
        jnp.concatenate([jnp.transpose(w_uk.astype(BF16), (1, 2, 0)), jnp.zeros((nh, nope, LANES), BF16)], axis=2),
        jnp.concatenate([jnp.zeros((nh, 2 * rope, rank), BF16), jnp.concatenate([eye, eye], axis=1)], axis=2),
    ], axis=1)
    out = pl.pallas_call(
        _absorb_body,
        out_shape=jax.ShapeDtypeStruct((m, nh * width), BF16),
        compiler_params=pltpu.CompilerParams(vmem_limit_bytes=VMEM_LIMIT),
        name="absorb_queries",
    )(qt, w)
    return out.reshape(m, nh, width)


CHUNK_AHEAD = 3


def _attn_sample_body(pt_ref, q_ref, cn_ref, pn_ref, ckv_hbm, kpe_hbm, o_ref, cbuf, pbuf, sem,
                      *, npg, nch, page, rank, rope):
    b = pl.program_id(0)
    total = pl.num_programs(0) * nch

    def page_copies(chunk, slot):
        out = []
        for i in range(npg):
            pid = pt_ref[chunk * npg + i]
            out.append(pltpu.make_async_copy(ckv_hbm.at[pid], cbuf.at[slot, pl.ds(i * page, page)],
                                             sem.at[0, slot]))
            out.append(pltpu.make_async_copy(kpe_hbm.at[pid], pbuf.at[slot, :, pl.ds(i * page, page)],
                                             sem.at[1, slot]))
        return out

    def start_all(copies):
        for n, cp in enumerate(copies):
            cp.start(priority=(n // 2 + n) % 2)

    @pl.when(b == 0)
    def _():
        for c in range(CHUNK_AHEAD):
            start_all(page_copies(jnp.minimum(c, total - 1), c % nch))

    q_lat = q_ref[0, :, :rank]
    q_pe = q_ref[0, :, rank:rank + rope]
    nh = q_lat.shape[0]
    m_i = jnp.full((nh, 1), NEG_BIG, F32)
    l_i = jnp.zeros((nh, 1), F32)
    half = npg * page // 2

    def add_pv(accs, alpha, p, c):
        return tuple(alpha * accs[i] + jnp.dot(p[:, half * i:half * (i + 1)], c[half * i:half * (i + 1)],
                                               preferred_element_type=F32) for i in range(2))

    accs = (jnp.zeros((nh, rank), F32), jnp.zeros((nh, rank), F32))
    pending = None
    for j in range(nch):
        for cp in page_copies(b * nch + j, j):
            cp.wait()
        c = cbuf[j].astype(BF16)
        s = (lax.dot_general(q_lat, c, (((1,), (1,)), ((), ())), preferred_element_type=F32)
             + jnp.dot(q_pe, pbuf[j].astype(BF16), preferred_element_type=F32))
        if pending is not None:
            accs = add_pv(accs, *pending)
        m_new = jnp.maximum(m_i, jnp.max(s, axis=1, keepdims=True))
        alpha = jnp.exp2(m_i - m_new)
        p = jnp.exp2(s - m_new)
        l_i = alpha * l_i + jnp.sum(p, axis=1, keepdims=True)
        m_i = m_new
        pending = (alpha, p.astype(BF16), c)
        nxt = b * nch + j + CHUNK_AHEAD
        start_all(page_copies(jnp.minimum(nxt, total - 1), (j + CHUNK_AHEAD) % nch))
    accs = add_pv(accs, *pending)

    @pl.when(b == pl.num_programs(0) - 1)
    def _():
        for j in range(nch - CHUNK_AHEAD, nch):
            for cp in page_copies(total - 1, (j + CHUNK_AHEAD) % nch):
                cp.wait()

    c_new = cn_ref[0].astype(BF16).astype(F32)
    p_new = pn_ref[0].astype(BF16).astype(F32)
    s_new = (jnp.sum(q_lat.astype(F32) * c_new, axis=1, keepdims=True)
             + jnp.sum(q_pe.astype(F32) * p_new, axis=1, keepdims=True))
    m_fin = jnp.maximum(m_i, s_new)
    a_old = jnp.exp2(m_i - m_fin)
    p_n = jnp.exp2(s_new - m_fin)
    l_fin = a_old * l_i + p_n
    o_new = p_n.astype(BF16).astype(F32) * c_new
    o_ref[0] = (a_old * accs[0] + (a_old * accs[1] + o_new)) / l_fin


def _attn_sample(qabs, cache_ckv, cache_kpe, page_table, c_new, kpe_new, *, npg_pref=32):
    m, nh, width = qabs.shape
    _, page, rank = cache_ckv.shape
    rope = cache_kpe.shape[2]
    n_pages = page_table.shape[1]
    npg = npg_pref if n_pages % npg_pref == 0 else n_pages
    nch = n_pages // npg
    assert nch > CHUNK_AHEAD, "a slot must not be refilled while its chunk is in use"
    kpe_t = jnp.swapaxes(cache_kpe, 1, 2)
    grid_spec = pltpu.PrefetchScalarGridSpec(
        num_scalar_prefetch=1,
        grid=(m,),
        in_specs=[
            pl.BlockSpec((1, nh, width), lambda b, pt: (b, 0, 0)),
            pl.BlockSpec((1, 1, rank), lambda b, pt: (b, 0, 0)),
            pl.BlockSpec((1, 1, rope), lambda b, pt: (b, 0, 0)),
            pl.BlockSpec(memory_space=pl.ANY),
            pl.BlockSpec(memory_space=pl.ANY),
        ],
        out_specs=pl.BlockSpec((1, nh, rank), lambda b, pt: (b, 0, 0)),
        scratch_shapes=[
            pltpu.VMEM((nch, npg * page, rank), F32),
            pltpu.VMEM((nch, rope, npg * page), F32),
            pltpu.SemaphoreType.DMA((2, nch)),
        ],
    )
    return pl.pallas_call(
        functools.partial(_attn_sample_body, npg=npg, nch=nch, page=page, rank=rank, rope=rope),
        grid_spec=grid_spec,
        out_shape=jax.ShapeDtypeStruct((m, nh, rank), F32),
        compiler_params=_cparams(("arbitrary",)),
        name="attn_sample",
    )(page_table.reshape(-1), qabs, c_new.reshape(m, 1, rank), kpe_new.reshape(m, 1, rope),
      cache_ckv, kpe_t)


def _oproj_lat_body(ol_ref, x_ref, wuv_ref, wo_ref, o_ref, *, nh, rank, hv):
    acc = x_ref[...]
    for h in range(nh):
        oh = jnp.dot(ol_ref[:, rank * h:rank * (h + 1)].astype(BF16), wuv_ref[h], preferred_element_type=F32)
        acc = acc + jnp.dot(oh.astype(BF16), wo_ref[hv * h:hv * (h + 1), :], preferred_element_type=F32)
    o_ref[...] = acc


def _oproj_lat(o_lat, x, w_uv, w_o):
    m, nh, rank = o_lat.shape
    hv = w_uv.shape[2]
    return pl.pallas_call(
        functools.partial(_oproj_lat_body, nh=nh, rank=rank, hv=hv),
        out_shape=jax.ShapeDtypeStruct(x.shape, F32),
        compiler_params=pltpu.CompilerParams(vmem_limit_bytes=VMEM_LIMIT),
        name="attn_out_sample",
    )(o_lat.reshape(m, nh * rank), x, jnp.transpose(w_uv, (1, 0, 2)).astype(BF16), w_o.astype(BF16))


def kernel(x_prompt, x_sample, state_s5_re, state_s5_im, cache_ckv, cache_kpe, page_table, ffn_norm, ffn_w_gate, ffn_w_up, ffn_w_down, mix_norm, s5_a_re, s5_a_im, s5_log_dt, s5_b_re, s5_b_im, s5_c_re, s5_c_im, s5_d, s5_w_glu_v, s5_w_glu_g, mla_w_dq, mla_q_norm, mla_w_uq, mla_qn_norm, mla_qr_norm, mla_w_o, kv_in_norm, kv_w_dkv, kv_c_norm, kv_kpe_norm, kv_w_uk, kv_w_uv):
    bp, tp, d = x_prompt.shape
    bs, ts, _ = x_sample.shape
    assert ts == 1, "the sample group decodes one token per sequence"
    depth = ffn_norm.shape[0]
    n_a = s5_a_re.shape[0]
    assert depth == 2 and n_a == 1 and mla_w_dq.shape[0] == 1, "one S5 layer followed by one MLA layer"
    g, p = s5_a_re.shape[1], s5_a_re.shape[2]
    nh, hv = kv_w_uv.shape[1], kv_w_uv.shape[2]
    rope = kv_kpe_norm.shape[0]
    past_len = page_table.shape[1] * cache_ckv.shape[1]

    wg, wu, wd = ffn_w_gate.astype(BF16), ffn_w_up.astype(BF16), ffn_w_down.astype(BF16)
    tm_p, tm_s = _tile(tp, 512), bs

    def ffn(layer, idx, *groups):
        return _ffn(groups, ffn_norm[layer, idx], wg, wu, wd, layer, idx)

    xp = x_prompt.reshape(bp * tp, d)
    xs = x_sample.reshape(bs, d)
    pos_p = jnp.arange(tp, dtype=jnp.int32)
    pos_s = jnp.full((bs,), past_len, dtype=jnp.int32)
    kv_args = (kv_in_norm, kv_w_dkv, kv_c_norm, kv_kpe_norm, kv_w_uk, kv_w_uv)
    q_args = (mix_norm[1], mla_w_dq[0], mla_q_norm[0], mla_w_uq[0], mla_qn_norm[0], mla_qr_norm[0])

    lam_re, lam_im, wb, wc = _s5_params(s5_a_re[0], s5_a_im[0], s5_log_dt[0], s5_b_re[0], s5_b_im[0],
                                        s5_c_re[0], s5_c_im[0])
    wv, wgl = s5_w_glu_v[0].astype(BF16), s5_w_glu_g[0].astype(BF16)
    (xp,), (xs,) = ffn(0, 0, _Rows(xp, tm_p), _Rows(xs, tm_s))
    xp, sre_p, sim_p = _s5_prompt(xp.reshape(bp, tp, d), mix_norm[0], lam_re, lam_im, wb, wc, s5_d[0], wv, wgl)
    xs, sre_s, sim_s = _s5_step(xs, mix_norm[0], state_s5_re[0].reshape(bs, g * p),
                                state_s5_im[0].reshape(bs, g * p), lam_re, lam_im, wb, wc, s5_d[0], wv, wgl)
    (xp, ckv_p, kpe_p, k_p, vt_p), (xs, ckv_s, kpe_s) = ffn(
        0, 1,
        _Rows(xp.reshape(bp * tp, d), tm_p, post=_kv_stage(bp, tp, tm_p, pos_p, *kv_args, with_heads=True)),
        _Rows(xs, tm_s, post=_kv_stage(1, bs, tm_s, pos_s, *kv_args, with_heads=False)))

    (xp, q_p), (xs, q_s) = ffn(1, 0,
                               _Rows(xp, tm_p, post=_q_stage(bp, tp, tm_p, pos_p, *q_args)),
                               _Rows(xs, tm_s, post=_q_stage(1, bs, tm_s, pos_s, *q_args)))
    ot = _attn_prompt(q_p, k_p.reshape(bp, tp, nh * LANES), vt_p, nh=nh, hv=hv)
    qabs = _absorb(q_s, kv_w_uk, rope)
    o_lat = _attn_sample(qabs, cache_ckv, cache_kpe, page_table, ckv_s, kpe_s)
    xs = _oproj_lat(o_lat, xs, kv_w_uv, mla_w_o[0])
    (xp,), (xs,) = ffn(1, 1, _Rows(xp, tm_p, pre=(ot, mla_w_o[0].astype(BF16))), _Rows(xs, tm_s))

    return (xp.reshape(bp, tp, d), xs.reshape(bs, 1, d),
            sre_p.reshape(n_a, bp, g, p), sim_p.reshape(n_a, bp, g, p),
            sre_s.reshape(n_a, bs, g, p), sim_s.reshape(n_a, bs, g, p),
            ckv_p.reshape(bp, tp, -1), kpe_p.reshape(bp, tp, rope),
            ckv_s.reshape(bs, 1, -1), kpe_s.reshape(bs, 1, rope))
```
